```python
import math
import jax, jax.numpy as jnp
from jax import lax
import numpy as np

D_MODEL = 2048
BATCH = 4
SEQ = 8192
DEPTH = 1
DEC_BATCH = 8
DEC_SEQ = 32
PAST_LEN = 2048

CHUNK = 64
PLE_DIM = 256
RMS_EPS = 1e-6
NEG_INF = -1e30
D_SSM = 1024
SSM_GROUP = 16
N_SSM_GROUPS = D_SSM // SSM_GROUP
SSM_STATE = 64
N_HEADS = 8
HEAD_DIM = 64
V_DIM = 2 * HEAD_DIM
ROPE_THETA = 10000.0
Q_BLOCK = 128
Q_COLS = N_HEADS * 2 * HEAD_DIM
V_COLS = N_HEADS * V_DIM
IN_COLS = D_SSM + 2 * Q_COLS + V_COLS + 2 * D_MODEL
PEER_HEADS = 8
N_KEYS = 128
N_EXPERTS = N_KEYS * N_KEYS
PEER_QUERY = 256
PEER_HALF = PEER_QUERY // 2
PEER_TOPK = 16
PEER_TOKEN_BLOCK = 128

kernel_name = 'hybrid_s5_diffattn_peer_stream_step'


def rms_norm(x, g):
    xf = x.astype(jnp.float32)
    y = xf * lax.rsqrt(jnp.mean(xf * xf, axis=-1, keepdims=True) + RMS_EPS)
    return (y * g.astype(jnp.float32)).astype(x.dtype)


def rope(x, pos):
    half = HEAD_DIM // 2
    inv = ROPE_THETA ** (-jnp.arange(half, dtype=jnp.float32) * 2.0 / HEAD_DIM)
    ang = pos.astype(jnp.float32)[:, None] * inv[None, :]
    cos = jnp.cos(ang)[:, None, None, :]
    sin = jnp.sin(ang)[:, None, None, :]
    xf = x.astype(jnp.float32)
    x1, x2 = xf[..., :half], xf[..., half:]
    return jnp.concatenate([x1 * cos - x2 * sin, x2 * cos + x1 * sin], axis=-1).astype(x.dtype)


def diff_attend(q, k, v, q_pos, k_pos, lam):
    s = jnp.einsum('bqhcd,bkhcd->bhcqk', q.astype(jnp.float32), k.astype(jnp.float32)) * (HEAD_DIM ** -0.5)
    allowed = (k_pos[None, :] // CHUNK) <= (q_pos[:, None] // CHUNK)
    s = jnp.where(allowed, s, NEG_INF)
    a = jax.nn.softmax(s, axis=-1)
    w = a[:, :, 0] - lam * a[:, :, 1]
    return jnp.einsum('bhqk,bkhd->bqhd', w, v.astype(jnp.float32))


def chunk_causal_diff_attention(q, k, v, q_pos, k_pos, lam):
    bn, tq = q.shape[0], q.shape[1]
    if tq > Q_BLOCK and tq % Q_BLOCK == 0:
        nb = tq // Q_BLOCK
        qb = q.reshape(bn, nb, Q_BLOCK, N_HEADS, 2, HEAD_DIM).transpose(1, 0, 2, 3, 4, 5)
        pb = q_pos.reshape(nb, Q_BLOCK)
        out = lax.map(lambda a: diff_attend(a[0], k, v, a[1], k_pos, lam), (qb, pb))
        return out.transpose(1, 0, 2, 3, 4).reshape(bn, tq, N_HEADS, V_DIM)
    return diff_attend(q, k, v, q_pos, k_pos, lam)


def s5_scan(u, h0_re, h0_im, lam_re, lam_im, log_step, b_re, b_im, c_re, c_im, d):
    f32 = jnp.float32
    dt = jnp.exp(log_step.astype(f32))[:, None]
    lr, li = lam_re.astype(f32), lam_im.astype(f32)
    mag = jnp.exp(lr * dt)
    ar, ai = mag * jnp.cos(li * dt), mag * jnp.sin(li * dt)
    den = lr * lr + li * li
    fr = ((ar - 1.0) * lr + ai * li) / den
    fi = (ai * lr - (ar - 1.0) * li) / den
    br, bi = b_re.astype(f32), b_im.astype(f32)
    bbr = fr[..., None] * br - fi[..., None] * bi
    bbi = fr[..., None] * bi + fi[..., None] * br
    uf = u.astype(f32)
    xr = jnp.einsum('blgc,gpc->blgp', uf, bbr)
    xi = jnp.einsum('blgc,gpc->blgp', uf, bbi)
    h0r, h0i = h0_re.astype(f32), h0_im.astype(f32)
    xr = xr.at[:, 0].add(ar * h0r - ai * h0i)
    xi = xi.at[:, 0].add(ar * h0i + ai * h0r)
    length = u.shape[1]
    a_r = jnp.broadcast_to(ar, (1, length) + ar.shape)
    a_i = jnp.broadcast_to(ai, (1, length) + ai.shape)

    def combine(e1, e2):
        a1r, a1i, b1r, b1i = e1
        a2r, a2i, b2r, b2i = e2
        return (a2r * a1r - a2i * a1i, a2r * a1i + a2i * a1r,
                a2r * b1r - a2i * b1i + b2r, a2r * b1i + a2i * b1r + b2i)

    _, _, hr, hi = lax.associative_scan(combine, (a_r, a_i, xr, xi), axis=1)
    y = (jnp.einsum('blgp,gcp->blgc', hr, c_re.astype(f32))
         - jnp.einsum('blgp,gcp->blgc', hi, c_im.astype(f32))
         + d.astype(f32) * uf)
    return y, hr[:, -1], hi[:, -1]


def peer_tokens(h, w_q, keys, u_tab, v_tab):
    t = h.shape[0]
    q = (h @ w_q).reshape(t, PEER_HEADS, 2, PEER_HALF)
    s = jnp.einsum('thcd,hckd->thck', q.astype(jnp.float32), keys.astype(jnp.float32))
    s1, i1 = lax.top_k(s[:, :, 0], PEER_TOPK)
    s2, i2 = lax.top_k(s[:, :, 1], PEER_TOPK)
    cand = (s1[..., :, None] + s2[..., None, :]).reshape(t, PEER_HEADS, PEER_TOPK * PEER_TOPK)
    cidx = (i1[..., :, None] * N_KEYS + i2[..., None, :]).reshape(t, PEER_HEADS, PEER_TOPK * PEER_TOPK)
    sc, sel = lax.top_k(cand, PEER_TOPK)
    idx = jnp.take_along_axis(cidx, sel, axis=-1)
    gate = jax.nn.softmax(sc, axis=-1)
    ue = jnp.take(u_tab, idx, axis=0)
    act = jax.nn.gelu(jnp.einsum('thkd,td->thk', ue, h).astype(jnp.float32), approximate=False)
    coef = (gate * act).astype(h.dtype)
    ve = jnp.take(v_tab, idx, axis=0)
    return jnp.einsum('thk,thkd->td', coef, ve)


def peer_blocked(h, w_q, keys, u_tab, v_tab):
    t = h.shape[0]
    nb = -(-t // PEER_TOKEN_BLOCK)
    pad = nb * PEER_TOKEN_BLOCK - t
    hp = jnp.pad(h, ((0, pad), (0, 0))).reshape(nb, PEER_TOKEN_BLOCK, D_MODEL)
    out = lax.map(lambda hb: peer_tokens(hb, w_q, keys, u_tab, v_tab), hp)
    return out.reshape(nb * PEER_TOKEN_BLOCK, D_MODEL)[:t]


def layer_fwd(x, p, h0_re, h0_im, k_past, v_past, lp, lambda_init):
    bn, t, _ = x.shape
    past = k_past.shape[1]
    h = rms_norm(x, lp['g_mix'])
    z = h @ lp['w_in']
    i0 = D_SSM
    i1 = i0 + Q_COLS
    i2 = i1 + Q_COLS
    i3 = i2 + V_COLS
    i4 = i3 + D_MODEL
    u, q, k, v, g_a, g_b = jnp.split(z, [i0, i1, i2, i3, i4], axis=-1)
    y_ssm, hr, hi = s5_scan(u.reshape(bn, t, N_SSM_GROUPS, SSM_GROUP), h0_re, h0_im,
                            lp['lam_re'], lp['lam_im'], lp['log_step'],
                            lp['b_re'], lp['b_im'], lp['c_re'], lp['c_im'], lp['d'])
    y_ssm = jax.nn.gelu(y_ssm.reshape(bn, t, D_SSM), approximate=False).astype(x.dtype)
    glu_a, glu_b = jnp.split(y_ssm @ lp['w_glu'], 2, axis=-1)
    branch_a = glu_a * jax.nn.sigmoid(glu_b)
    pos_new = past + jnp.arange(t, dtype=jnp.int32)
    q = rope(q.reshape(bn, t, N_HEADS, 2, HEAD_DIM), pos_new)
    k = rope(k.reshape(bn, t, N_HEADS, 2, HEAD_DIM), pos_new)
    v = v.reshape(bn, t, N_HEADS, V_DIM)
    k_all = jnp.concatenate([k_past.reshape(bn, past, N_HEADS, 2, HEAD_DIM), k], axis=1)
    v_all = jnp.concatenate([v_past, v], axis=1)
    k_pos = jnp.arange(past + t, dtype=jnp.int32)
    f32 = jnp.float32
    lam = (jnp.exp(jnp.sum(lp['lq1'].astype(f32) * lp['lk1'].astype(f32)))
           - jnp.exp(jnp.sum(lp['lq2'].astype(f32) * lp['lk2'].astype(f32))) + lambda_init)
    o = chunk_causal_diff_attention(q, k_all, v_all, pos_new, k_pos, lam)
    o = rms_norm(o, lp['g_subln']) * (1.0 - lambda_init)
    branch_b = o.reshape(bn, t, V_COLS).astype(x.dtype) @ lp['w_attn_out']
    merged = jax.nn.sigmoid(g_a) * branch_a + jax.nn.sigmoid(g_b) * branch_b
    x = x + merged @ lp['w_out']
    h2 = rms_norm(x, lp['g_ffn'])
    x = x + peer_blocked(h2.reshape(bn * t, D_MODEL), lp['peer_w_q'], lp['peer_keys'],
                         lp['peer_u'], lp['peer_v']).reshape(bn, t, D_MODEL)
    h3 = rms_norm(x, lp['g_ple'])
    x = x + (p @ lp['w_ple_proj']) * jax.nn.sigmoid(h3 @ lp['w_ple_gate'])
    new_k = k.reshape(bn, t, N_HEADS, 2 * HEAD_DIM)
    return x, new_k, v, hr, hi


def run_trunk(x, p, h0_re, h0_im, k_past, v_past, layers, g_final):
    ks, vs, rs, ims = [], [], [], []
    for l in range(DEPTH):
        lambda_init = 0.8 - 0.6 * math.exp(-0.3 * l)
        x, k_new, v_new, hr, hi = layer_fwd(x, p[l], h0_re[l], h0_im[l], k_past[l], v_past[l],
                                            layers[l], lambda_init)
        ks.append(k_new)
        vs.append(v_new)
        rs.append(hr)
        ims.append(hi)
    return rms_norm(x, g_final), jnp.stack(ks), jnp.stack(vs), jnp.stack(rs), jnp.stack(ims)


def setup_inputs(seed: int = 0) -> dict:
    key = jax.random.key(seed)
    ks = iter(jax.random.split(key, 48))
    f32 = jnp.float32

    def nrm(shape, scale):
        return jax.random.normal(next(ks), shape, f32) * scale

    G, P, C = N_SSM_GROUPS, SSM_STATE, SSM_GROUP
    return {
        'x_prompt': nrm((BATCH, SEQ, D_MODEL), 1.0),
        'x_sample': nrm((DEC_BATCH, DEC_SEQ, D_MODEL), 1.0),
        'p_prompt': nrm((DEPTH, BATCH, SEQ, PLE_DIM), 1.0),
        'p_sample': nrm((DEPTH, DEC_BATCH, DEC_SEQ, PLE_DIM), 1.0),
        'cache_k': nrm((DEPTH, DEC_BATCH, PAST_LEN, N_HEADS, 2 * HEAD_DIM), 1.0),
        'cache_v': nrm((DEPTH, DEC_BATCH, PAST_LEN, N_HEADS, V_DIM), 1.0),
        'state_ssm_re': nrm((DEPTH, DEC_BATCH, G, P), 0.3),
        'state_ssm_im': nrm((DEPTH, DEC_BATCH, G, P), 0.3),
        'g_mix_norm': 1.0 + nrm((DEPTH, D_MODEL), 0.02),
        'w_in': nrm((DEPTH, D_MODEL, IN_COLS), D_MODEL ** -0.5),
        'ssm_lambda_re': -0.5 + nrm((DEPTH, G, P), 0.01),
        'ssm_lambda_im': math.pi * jnp.arange(P, dtype=f32)[None, None, :] + nrm((DEPTH, G, P), 0.01),
        'ssm_log_step': jax.random.uniform(next(ks), (DEPTH, G), f32, math.log(1e-3), math.log(1e-1)),
        'ssm_b_re': nrm((DEPTH, G, P, C), (2 * C) ** -0.5),
        'ssm_b_im': nrm((DEPTH, G, P, C), (2 * C) ** -0.5),
        'ssm_c_re': nrm((DEPTH, G, C, P), P ** -0.5),
        'ssm_c_im': nrm((DEPTH, G, C, P), P ** -0.5),
        'ssm_d': nrm((DEPTH, G, C), 1.0),
        'w_glu': nrm((DEPTH, D_SSM, 2 * D_MODEL), D_SSM ** -0.5),
        'diff_lambda_q1': nrm((DEPTH, HEAD_DIM), 0.1),
        'diff_lambda_k1': nrm((DEPTH, HEAD_DIM), 0.1),
        'diff_lambda_q2': nrm((DEPTH, HEAD_DIM), 0.1),
        'diff_lambda_k2': nrm((DEPTH, HEAD_DIM), 0.1),
        'g_subln': 1.0 + nrm((DEPTH, V_DIM), 0.02),
        'w_attn_out': nrm((DEPTH, V_COLS, D_MODEL), V_COLS ** -0.5),
        'w_out': nrm((DEPTH, D_MODEL, D_MODEL), D_MODEL ** -0.5),
        'g_ffn_norm': 1.0 + nrm((DEPTH, D_MODEL), 0.02),
        'peer_w_q': nrm((DEPTH, D_MODEL, PEER_HEADS * PEER_QUERY), D_MODEL ** -0.5),
        'peer_keys': nrm((DEPTH, PEER_HEADS, 2, N_KEYS, PEER_HALF), PEER_HALF ** -0.5),
        'peer_u': nrm((DEPTH, N_EXPERTS, D_MODEL), D_MODEL ** -0.5),
        'peer_v': nrm((DEPTH, N_EXPERTS, D_MODEL), PEER_HEADS ** -0.5),
        'g_ple_norm': 1.0 + nrm((DEPTH, D_MODEL), 0.02),
        'w_ple_gate': nrm((DEPTH, D_MODEL, D_MODEL), D_MODEL ** -0.5),
        'w_ple_proj': nrm((DEPTH, PLE_DIM, D_MODEL), PLE_DIM ** -0.5),
        'g_final': 1.0 + nrm((D_MODEL,), 0.02),
    }


def reference(x_prompt, x_sample, p_prompt, p_sample, cache_k, cache_v, state_ssm_re, state_ssm_im,
              g_mix_norm, w_in, ssm_lambda_re, ssm_lambda_im, ssm_log_step, ssm_b_re, ssm_b_im,
              ssm_c_re, ssm_c_im, ssm_d, w_glu, diff_lambda_q1, diff_lambda_k1, diff_lambda_q2,
              diff_lambda_k2, g_subln, w_attn_out, w_out, g_ffn_norm, peer_w_q, peer_keys, peer_u,
              peer_v, g_ple_norm, w_ple_gate, w_ple_proj, g_final):
    layers = [dict(g_mix=g_mix_norm[l], w_in=w_in[l], lam_re=ssm_lambda_re[l], lam_im=ssm_lambda_im[l],
                   log_step=ssm_log_step[l], b_re=ssm_b_re[l], b_im=ssm_b_im[l], c_re=ssm_c_re[l],
                   c_im=ssm_c_im[l], d=ssm_d[l], w_glu=w_glu[l], lq1=diff_lambda_q1[l],
                   lk1=diff_lambda_k1[l], lq2=diff_lambda_q2[l], lk2=diff_lambda_k2[l],
                   g_subln=g_subln[l], w_attn_out=w_attn_out[l], w_out=w_out[l], g_ffn=g_ffn_norm[l],
                   peer_w_q=peer_w_q[l], peer_keys=peer_keys[l], peer_u=peer_u[l], peer_v=peer_v[l],
                   g_ple=g_ple_norm[l], w_ple_gate=w_ple_gate[l], w_ple_proj=w_ple_proj[l])
              for l in range(DEPTH)]
    h0_zero = jnp.zeros((DEPTH, BATCH, N_SSM_GROUPS, SSM_STATE), jnp.float32)
    k_none = jnp.zeros((DEPTH, BATCH, 0, N_HEADS, 2 * HEAD_DIM), x_prompt.dtype)
    v_none = jnp.zeros((DEPTH, BATCH, 0, N_HEADS, V_DIM), x_prompt.dtype)
    y_prompt, new_k_prompt, new_v_prompt, new_ssm_re_prompt, new_ssm_im_prompt = run_trunk(
        x_prompt, p_prompt, h0_zero, h0_zero, k_none, v_none, layers, g_final)
    y_sample, new_k_sample, new_v_sample, new_ssm_re_sample, new_ssm_im_sample = run_trunk(
        x_sample, p_sample, state_ssm_re, state_ssm_im, cache_k, cache_v, layers, g_final)
    return (y_prompt, y_sample, new_k_prompt, new_v_prompt, new_ssm_re_prompt, new_ssm_im_prompt,
            new_k_sample, new_v_sample, new_ssm_re_sample, new_ssm_im_sample)
```

```python
import functools
import math

import jax
import jax.numpy as jnp
from jax import lax
from jax.experimental import pallas as pl
from jax.experimental.pallas import tpu as pltpu

F32 = jnp.float32
BF16 = jnp.bfloat16

D_MODEL = 2048
DEPTH = 1
CHUNK = 64
CHUNK_SHIFT = CHUNK.bit_length() - 1
assert 1 << CHUNK_SHIFT == CHUNK
PLE_DIM = 256
RMS_EPS = 1e-6
NEG_INF = -1e30
D_SSM = 1024
SSM_GROUP = 16
N_SSM_GROUPS = D_SSM // SSM_GROUP
SSM_STATE = 64
N_STATE = N_SSM_GROUPS * SSM_STATE
N_HEADS = 8
HEAD_DIM = 64
V_DIM = 2 * HEAD_DIM
ROPE_THETA = 10000.0
Q_COLS = N_HEADS * 2 * HEAD_DIM
V_COLS = N_HEADS * V_DIM
IN_COLS = D_SSM + 2 * Q_COLS + V_COLS + 2 * D_MODEL
PEER_HEADS = 8
N_KEYS = 128
N_EXPERTS = N_KEYS * N_KEYS
PEER_QUERY = 256
PEER_HALF = PEER_QUERY // 2
PEER_TOPK = 16

LANES = 128
SUBLANES = 8
VMEM_LIMIT = 56 * 1024 * 1024


def _cparams(*sem):
    return pltpu.CompilerParams(dimension_semantics=sem, vmem_limit_bytes=VMEM_LIMIT)


def _rms(x, g):
    return x * lax.rsqrt(jnp.mean(x * x, axis=-1, keepdims=True) + RMS_EPS) * g


def _gelu(x):
    return 0.5 * x * (1.0 + lax.erf(x * (2.0 ** -0.5)))


def _sigmoid(x):
    return 1.0 / (1.0 + jnp.exp(-x))


def _dot(a, b):
    return jnp.dot(a, b, preferred_element_type=F32)


def _dot_nt(a, b):
    return lax.dot_general(a, b, (((1,), (1,)), ((), ())), preferred_element_type=F32)


IN_TN = 512
SEG_U = (0, D_SSM // IN_TN)
SEG_Q = (SEG_U[0] + SEG_U[1], Q_COLS // IN_TN)
SEG_K = (SEG_Q[0] + SEG_Q[1], Q_COLS // IN_TN)
SEG_V = (SEG_K[0] + SEG_K[1], V_COLS // IN_TN)
SEG_GA = (SEG_V[0] + SEG_V[1], D_MODEL // IN_TN)
SEG_GB = (SEG_GA[0] + SEG_GA[1], D_MODEL // IN_TN)


def _inproj_kernel(x_ref, g_ref, w_ref, cos_ref, sin_ref,
                   u_ref, q_ref, kf_ref, kb_ref, vf_ref, vb_ref, ga_ref, gb_ref, h_scr):
    j = pl.program_id(1)

    @pl.when(j == 0)
    def _():
        h_scr[...] = _rms(x_ref[...], g_ref[...]).astype(BF16)

    def z():
        return _dot(h_scr[...], w_ref[...])

    def rope(t):
        n = t.shape[-1]
        lane = lax.broadcasted_iota(jnp.int32, t.shape, 1)
        first = jnp.bitwise_and(lane, HEAD_DIM - 1) < (HEAD_DIM // 2)
        partner = jnp.where(first, pltpu.roll(t, n - HEAD_DIM // 2, 1), pltpu.roll(t, HEAD_DIM // 2, 1))
        return t * cos_ref[...] + partner * sin_ref[...]

    def in_seg(seg):
        return jnp.logical_and(j >= seg[0], j < seg[0] + seg[1])

    @pl.when(in_seg(SEG_U))
    def _():
        u_ref[...] = z().astype(BF16)

    @pl.when(in_seg(SEG_Q))
    def _():
        q_ref[...] = (rope(z()) * (HEAD_DIM ** -0.5)).astype(BF16)

    @pl.when(in_seg(SEG_K))
    def _():
        k = rope(z())
        kf_ref[...] = k
        kb_ref[...] = k.astype(BF16)

    @pl.when(in_seg(SEG_V))
    def _():
        v = z()
        vf_ref[...] = v
        vb_ref[...] = v.astype(BF16)

    @pl.when(in_seg(SEG_GA))
    def _():
        ga_ref[...] = _sigmoid(z()).astype(BF16)

    @pl.when(in_seg(SEG_GB))
    def _():
        gb_ref[...] = _sigmoid(z()).astype(BF16)


def _inproj(x2d, g, w_bf, cos_t, sin_t, tm):
    T = x2d.shape[0]
    nj = IN_COLS // IN_TN
    n_tab = cos_t.shape[0] // tm

    def seg_map(seg):
        return lambda i, j: (i, jnp.clip(j - seg[0], 0, seg[1] - 1))

    def out(seg, dtype):
        return (jax.ShapeDtypeStruct((T, seg[1] * IN_TN), dtype), pl.BlockSpec((tm, IN_TN), seg_map(seg)))

    outs = [out(SEG_U, BF16), out(SEG_Q, BF16), out(SEG_K, F32), out(SEG_K, BF16),
            out(SEG_V, F32), out(SEG_V, BF16), out(SEG_GA, BF16), out(SEG_GB, BF16)]
    return pl.pallas_call(
        _inproj_kernel,
        grid=(T // tm, nj),
        in_specs=[
            pl.BlockSpec((tm, D_MODEL), lambda i, j: (i, 0)),
            pl.BlockSpec((1, D_MODEL), lambda i, j: (0, 0)),
            pl.BlockSpec((D_MODEL, IN_TN), lambda i, j: (0, j)),
            pl.BlockSpec((tm, IN_TN), lambda i, j: (i % n_tab, 0)),
            pl.BlockSpec((tm, IN_TN), lambda i, j: (i % n_tab, 0)),
        ],
        out_specs=[o[1] for o in outs],
        out_shape=[o[0] for o in outs],
        scratch_shapes=[pltpu.VMEM((tm, D_MODEL), BF16)],
        compiler_params=_cparams("parallel", "arbitrary"),
        name="inproj",
    )(x2d, g, w_bf, cos_t, sin_t)


def _rope_tables(seq, past, rows):
    half = HEAD_DIM // 2
    inv = ROPE_THETA ** (-jnp.arange(half, dtype=F32) * 2.0 / HEAD_DIM)
    pos = (past + (jnp.arange(rows, dtype=jnp.int32) % seq)).astype(F32)
    ang = pos[:, None] * inv[None, :]
    cos, sin = jnp.cos(ang), jnp.sin(ang)
    reps = IN_TN // HEAD_DIM
    cos_t = jnp.tile(jnp.concatenate([cos, cos], axis=-1), (1, reps))
    sin_t = jnp.tile(jnp.concatenate([-sin, sin], axis=-1), (1, reps))
    return cos_t, sin_t


S5_COLS = 512
S5_UBLK = LANES
S5_N_UBLK = D_SSM // S5_UBLK
S5_XBLK = (S5_UBLK // SSM_GROUP) * SSM_STATE
S5_YBLK = 256
S5_N_YBLK = D_SSM // S5_YBLK
S5_HBLK = (S5_YBLK // SSM_GROUP) * SSM_STATE


def _s5_kernel(u_ref, h0_ref, perm_ref, permt_ref, wx_ref, apow_ref, cr_ref, ci_ref, d_ref,
               y_ref, hfin_ref, x_scr, c_scr, carry_scr, *, seg_len):
    t = pl.program_id(1)
    rows = SUBLANES * seg_len
    n = N_STATE

    @pl.when(t == 0)
    def _():
        carry_scr[...] = h0_ref[...]

    up = _dot(perm_ref[...], u_ref[...])
    upb = up.astype(BF16)
    for a in range(S5_N_UBLK):
        xa = _dot(upb[:, a * S5_UBLK:(a + 1) * S5_UBLK], wx_ref[a])
        x_scr[:, a * S5_XBLK:(a + 1) * S5_XBLK] = xa[:, :S5_XBLK]
        x_scr[:, n + a * S5_XBLK:n + (a + 1) * S5_XBLK] = xa[:, S5_XBLK:]

    for c in range(n // S5_COLS):
        lo = c * S5_COLS
        ar = jnp.broadcast_to(apow_ref[0:1, lo:lo + S5_COLS], (SUBLANES, S5_COLS))
        ai = jnp.broadcast_to(apow_ref[0:1, n + lo:n + lo + S5_COLS], (SUBLANES, S5_COLS))

        def scan_step(k, carry, lo=lo, ar=ar, ai=ai):
            hr, hi = carry
            r0 = pl.multiple_of(k * SUBLANES, SUBLANES)
            xr = x_scr[pl.ds(r0, SUBLANES), lo:lo + S5_COLS]
            xi = x_scr[pl.ds(r0, SUBLANES), n + lo:n + lo + S5_COLS]
            nr = ar * hr - ai * hi + xr
            ni = ar * hi + ai * hr + xi
            x_scr[pl.ds(r0, SUBLANES), lo:lo + S5_COLS] = nr
            x_scr[pl.ds(r0, SUBLANES), n + lo:n + lo + S5_COLS] = ni
            return nr, ni

        zero = jnp.zeros((SUBLANES, S5_COLS), F32)
        lax.fori_loop(0, seg_len, scan_step, (zero, zero))

    alr = apow_ref[seg_len - 1:seg_len, :n]
    ali = apow_ref[seg_len - 1:seg_len, n:]
    cr = carry_scr[:, :n]
    ci = carry_scr[:, n:]
    for s in range(SUBLANES):
        c_scr[s:s + 1, :n] = cr
        c_scr[s:s + 1, n:] = ci
        lr = x_scr[rows - SUBLANES + s:rows - SUBLANES + s + 1, :n]
        li = x_scr[rows - SUBLANES + s:rows - SUBLANES + s + 1, n:]
        cr, ci = alr * cr - ali * ci + lr, alr * ci + ali * cr + li
    carry_scr[:, :n] = cr
    carry_scr[:, n:] = ci

    for c in range(n // S5_COLS):
        lo = c * S5_COLS
        sr = c_scr[:, lo:lo + S5_COLS]
        si = c_scr[:, n + lo:n + lo + S5_COLS]

        def fix_step(k, _, lo=lo, sr=sr, si=si):
            r0 = pl.multiple_of(k * SUBLANES, SUBLANES)
            pr = apow_ref[pl.ds(k, 1), lo:lo + S5_COLS]
            pi = apow_ref[pl.ds(k, 1), n + lo:n + lo + S5_COLS]
            x_scr[pl.ds(r0, SUBLANES), lo:lo + S5_COLS] += pr * sr - pi * si
            x_scr[pl.ds(r0, SUBLANES), n + lo:n + lo + S5_COLS] += pr * si + pi * sr
            return 0

        lax.fori_loop(0, seg_len, fix_step, 0)

    @pl.when(t == pl.num_programs(1) - 1)
    def _():
        hfin_ref[...] = carry_scr[...]

    for j in range(S5_N_YBLK):
        hr = x_scr[:, j * S5_HBLK:(j + 1) * S5_HBLK].astype(BF16)
        hi = x_scr[:, n + j * S5_HBLK:n + (j + 1) * S5_HBLK].astype(BF16)
        yj = (_dot(hr, cr_ref[j]) + _dot(hi, ci_ref[j])
              + d_ref[:, j * S5_YBLK:(j + 1) * S5_YBLK] * up[:, j * S5_YBLK:(j + 1) * S5_YBLK])
        yj = _gelu(yj).astype(BF16)
        y_ref[:, j * S5_YBLK:(j + 1) * S5_YBLK] = _dot(permt_ref[...], yj).astype(BF16)


def _s5_tables(lp, seg_len):
    g, p, c = N_SSM_GROUPS, SSM_STATE, SSM_GROUP
    dt = jnp.exp(lp['log_step'].astype(F32))[:, None]
    lr, li = lp['lam_re'].astype(F32), lp['lam_im'].astype(F32)
    mag = jnp.exp(lr * dt)
    ar, ai = mag * jnp.cos(li * dt), mag * jnp.sin(li * dt)
    den = lr * lr + li * li
    fr = ((ar - 1.0) * lr + ai * li) / den
    fi = (ai * lr - (ar - 1.0) * li) / den
    br, bi = lp['b_re'].astype(F32), lp['b_im'].astype(F32)
    bbr = fr[..., None] * br - fi[..., None] * bi
    bbi = fr[..., None] * bi + fi[..., None] * br

    gpb = S5_UBLK // c
    eye = jnp.eye(gpb, dtype=F32)

    def in_blocks(bb):
        bb = bb.reshape(S5_N_UBLK, gpb, p, c)
        return jnp.einsum('xy,axpc->axcyp', eye, bb).reshape(S5_N_UBLK, S5_UBLK, S5_XBLK)

    wx = jnp.concatenate([in_blocks(bbr), in_blocks(bbi)], axis=-1).astype(BF16)

    pr, pi = ar.reshape(1, g * p), ai.reshape(1, g * p)
    while pr.shape[0] < seg_len:
        tr, ti = pr[-1:], pi[-1:]
        pr, pi = (jnp.concatenate([pr, pr * tr - pi * ti], axis=0),
                  jnp.concatenate([pi, pr * ti + pi * tr], axis=0))
    apow = jnp.concatenate([pr[:seg_len], pi[:seg_len]], axis=-1)

    gpy = S5_YBLK // c
    eye_y = jnp.eye(gpy, dtype=F32)

    def out_blocks(cm):
        cm = cm.astype(F32).reshape(S5_N_YBLK, gpy, c, p)
        return jnp.einsum('xy,jxcp->jxpyc', eye_y, cm).reshape(S5_N_YBLK, S5_HBLK, S5_YBLK)

    cr = out_blocks(lp['c_re']).astype(BF16)
    ci = (-out_blocks(lp['c_im'])).astype(BF16)
    d = lp['d'].astype(F32).reshape(1, D_SSM)
    return wx, apow, cr, ci, d


def _s5(u, h0, tables, seg_len):
    bn, seq, _ = u.shape
    rows = SUBLANES * seg_len
    wx, apow, cr, ci, d = tables
    r = jnp.arange(rows)
    perm = (r[None, :] == ((r % SUBLANES) * seg_len + r // SUBLANES)[:, None]).astype(BF16)
    const2 = lambda b, t: (0, 0)
    const3 = lambda b, t: (0, 0, 0)
    return pl.pallas_call(
        functools.partial(_s5_kernel, seg_len=seg_len),
        grid=(bn, seq // rows),
        in_specs=[
            pl.BlockSpec((None, rows, D_SSM), lambda b, t: (b, t, 0)),
            pl.BlockSpec((None, 1, 2 * N_STATE), lambda b, t: (b, 0, 0)),
            pl.BlockSpec((rows, rows), const2),
            pl.BlockSpec((rows, rows), const2),
            pl.BlockSpec(wx.shape, const3),
            pl.BlockSpec(apow.shape, const2),
            pl.BlockSpec(cr.shape, const3),
            pl.BlockSpec(ci.shape, const3),
            pl.BlockSpec((1, D_SSM), const2),
        ],
        out_specs=[
            pl.BlockSpec((None, rows, D_SSM), lambda b, t: (b, t, 0)),
            pl.BlockSpec((None, 1, 2 * N_STATE), lambda b, t: (b, 0, 0)),
        ],
        out_shape=[jax.ShapeDtypeStruct((bn, seq, D_SSM), BF16),
                   jax.ShapeDtypeStruct((bn, 1, 2 * N_STATE), F32)],
        scratch_shapes=[pltpu.VMEM((rows, 2 * N_STATE), F32),
                        pltpu.VMEM((SUBLANES, 2 * N_STATE), F32),
                        pltpu.VMEM((1, 2 * N_STATE), F32)],
        compiler_params=_cparams("parallel", "arbitrary"),
        name="s5",
    )(u, h0, perm, perm.T, wx, apow, cr, ci, d)


GLU_TN = 512


def _glu_kernel(y_ref, wa_ref, wb_ref, ga_ref, o_ref):
    y = y_ref[...]
    a = _dot(y, wa_ref[...])
    b = _dot(y, wb_ref[...])
    o_ref[...] = (ga_ref[...].astype(F32) * a * _sigmoid(b)).astype(BF16)


def _glu(y, w_glu_bf, sig_a, tm):
    T = y.shape[0]
    nj = D_MODEL // GLU_TN
    return pl.pallas_call(
        _glu_kernel,
        grid=(T // tm, nj),
        in_specs=[
            pl.BlockSpec((tm, D_SSM), lambda i, j: (i, 0)),
            pl.BlockSpec((D_SSM, GLU_TN), lambda i, j: (0, j)),
            pl.BlockSpec((D_SSM, GLU_TN), lambda i, j: (0, j + nj)),
            pl.BlockSpec((tm, GLU_TN), lambda i, j: (i, j)),
        ],
        out_specs=pl.BlockSpec((tm, GLU_TN), lambda i, j: (i, j)),
        out_shape=jax.ShapeDtypeStruct((T, D_MODEL), BF16),
        compiler_params=_cparams("parallel", "arbitrary"),
        name="glu",
    )(y, w_glu_bf, w_glu_bf, sig_a)


def _attn_kernel(q_ref, k_ref, v_ref, lq1_ref, lk1_ref, lq2_ref, lk2_ref, g_ref, o_ref,
                 m_scr, l_scr, acc_scr, *, tq, tk, past, lambda_init):
    i = pl.program_id(2)
    nk = k_ref.shape[0] // tk
    q = q_ref[...]
    lane = lax.broadcasted_iota(jnp.int32, q.shape, 1)
    zero = jnp.zeros_like(q)
    qs = jnp.concatenate([jnp.where(lane < HEAD_DIM, q, zero), jnp.where(lane >= HEAD_DIM, q, zero)], axis=0)

    m_scr[...] = jnp.full(m_scr.shape, NEG_INF, F32)
    l_scr[...] = jnp.zeros(l_scr.shape, F32)
    acc_scr[...] = jnp.zeros(acc_scr.shape, F32)

    q_lo = past + i * tq
    min_qc = q_lo // CHUNK
    max_qc = (q_lo + tq - 1) // CHUNK
    n_proc = jnp.minimum(nk, (max_qc * CHUNK + CHUNK - 1) // tk + 1)
    n_full = jnp.clip((min_qc * CHUNK + CHUNK) // tk, 0, n_proc)

    def block(j, masked):
        k0 = pl.multiple_of(j * tk, tk)
        kb = k_ref[pl.ds(k0, tk), :]
        vb = v_ref[pl.ds(k0, tk), :]
        s = _dot_nt(qs, kb)
        if masked:
            qpos = q_lo + jnp.bitwise_and(lax.broadcasted_iota(jnp.int32, s.shape, 0), tq - 1)
            kpos = k0 + lax.broadcasted_iota(jnp.int32, s.shape, 1)
            s = jnp.where(jnp.right_shift(kpos, CHUNK_SHIFT) <= jnp.right_shift(qpos, CHUNK_SHIFT), s, NEG_INF)
        m_old = m_scr[...]
        m_new = jnp.maximum(m_old, jnp.max(s, axis=-1, keepdims=True))
        alpha = jnp.exp(m_old - m_new)
        p = jnp.exp(s - m_new)
        l_scr[...] = alpha * l_scr[...] + jnp.sum(p, axis=-1, keepdims=True)
        acc_scr[...] = alpha * acc_scr[...] + _dot(p.astype(BF16), vb)
        m_scr[...] = m_new

    def full_body(j, c):
        block(j, False)
        return c

    def masked_body(j, c):
        block(j, True)
        return c

    lax.fori_loop(0, n_full, full_body, 0)
    lax.fori_loop(n_full, n_proc, masked_body, 0)

    lam = (jnp.exp(jnp.sum(lq1_ref[...] * lk1_ref[...], axis=-1, keepdims=True))
           - jnp.exp(jnp.sum(lq2_ref[...] * lk2_ref[...], axis=-1, keepdims=True)) + lambda_init)
    o = acc_scr[:tq, :] / l_scr[:tq, :] - lam * (acc_scr[tq:, :] / l_scr[tq:, :])
    o_ref[...] = (_rms(o, g_ref[...]) * (1.0 - lambda_init)).astype(BF16)


def _attention(q, k, v, lp, tq, tk, past, lambda_init):
    bn, seq, _ = q.shape
    lk = k.shape[1]
    vec = lambda a: a.astype(F32).reshape(1, -1)
    small = lambda n: pl.BlockSpec((1, n), lambda b, h, i: (0, 0))
    return pl.pallas_call(
        functools.partial(_attn_kernel, tq=tq, tk=tk, past=past, lambda_init=lambda_init),
        grid=(bn, N_HEADS, seq // tq),
        in_specs=[
            pl.BlockSpec((None, tq, V_DIM), lambda b, h, i: (b, i, h)),
            pl.BlockSpec((None, lk, V_DIM), lambda b, h, i: (b, 0, h)),
            pl.BlockSpec((None, lk, V_DIM), lambda b, h, i: (b, 0, h)),
            small(HEAD_DIM), small(HEAD_DIM), small(HEAD_DIM), small(HEAD_DIM), small(V_DIM),
        ],
        out_specs=pl.BlockSpec((None, tq, V_DIM), lambda b, h, i: (b, i, h)),
        out_shape=jax.ShapeDtypeStruct((bn, seq, V_COLS), BF16),
        scratch_shapes=[pltpu.VMEM((2 * tq, 1), F32), pltpu.VMEM((2 * tq, 1), F32),
                        pltpu.VMEM((2 * tq, V_DIM), F32)],
        compiler_params=_cparams("parallel", "parallel", "arbitrary"),
        name="diff_attn",
    )(q, k, v, vec(lp['lq1']), vec(lp['lk1']), vec(lp['lq2']), vec(lp['lk2']), vec(lp['g_subln']))


def _merge_kernel(x_ref, o_ref, ga_ref, sb_ref, wa_ref, wo_ref, g_ref, x1_ref, h2_ref):
    branch_b = _dot(o_ref[...], wa_ref[...])
    merged = ga_ref[...].astype(F32) + sb_ref[...].astype(F32) * branch_b
    x1 = x_ref[...] + _dot(merged.astype(BF16), wo_ref[...])
    x1_ref[...] = x1
    h2_ref[...] = _rms(x1, g_ref[...]).astype(BF16)


def _merge(x2d, o, gated_a, sig_b, wa_bf, wo_bf, g_ffn, tm):
    T = x2d.shape[0]
    row = lambda n: pl.BlockSpec((tm, n), lambda i: (i, 0))
    const = lambda s: pl.BlockSpec(s, lambda i: (0, 0))
    return pl.pallas_call(
        _merge_kernel,
        grid=(T // tm,),
        in_specs=[row(D_MODEL), row(V_COLS), row(D_MODEL), row(D_MODEL),
                  const((V_COLS, D_MODEL)), const((D_MODEL, D_MODEL)), const((1, D_MODEL))],
        out_specs=[row(D_MODEL), row(D_MODEL)],
        out_shape=[jax.ShapeDtypeStruct((T, D_MODEL), F32), jax.ShapeDtypeStruct((T, D_MODEL), BF16)],
        compiler_params=_cparams("parallel"),
        name="merge_out",
    )(x2d, o, gated_a, sig_b, wa_bf, wo_bf, g_ffn)


ROUTE_LC = 256
TAKEN = -3.0e38


def _split_bf16(x):
    hi = x.astype(BF16)
    return hi, (x - hi.astype(F32)).astype(BF16)


def _top_rows(s, k):
    out = []
    for _ in range(k):
        m = jnp.max(s, axis=0, keepdims=True)
        out.append(m)
        s = jnp.where(s == m, TAKEN, s)
    return out


def _stack_rows(rows):
    n, w = len(rows), rows[0].shape[1]
    idx = lax.broadcasted_iota(jnp.int32, (n, w), 0)
    out = jnp.zeros((n, w), F32)
    for r, v in enumerate(rows):
        out = jnp.where(idx == r, v, out)
    return out


def _route_kernel(h2_ref, wq_ref, keys_ref, s1_ref, s2_ref, e1_ref, e2_ref, tau_ref):
    tt = h2_ref.shape[0]
    qt = _dot_nt(wq_ref[...], h2_ref[...])
    halves = []
    for c in range(2):
        q_hi, q_lo = _split_bf16(qt[c * PEER_HALF:(c + 1) * PEER_HALF, :])
        k_hi, k_lo = _split_bf16(keys_ref[0, c])
        halves.append(_dot(k_hi, q_hi) + _dot(k_hi, q_lo) + _dot(k_lo, q_hi))
    s1_ref[0] = halves[0]
    s2_ref[0] = halves[1]

    width = min(tt, ROUTE_LC)
    for lc in range(tt // width):
        sl = slice(lc * width, (lc + 1) * width)
        s1 = s1_ref[0, :, sl]
        s2 = s2_ref[0, :, sl]
        a = _top_rows(s1, PEER_TOPK)
        b = _top_rows(s2, PEER_TOPK)
        b_all = _stack_rows(b)
        cand = jnp.concatenate([a[0] + b_all] + [a[i] + b_all[:SUBLANES] for i in range(1, PEER_TOPK)], axis=0)
        top = _top_rows(cand, PEER_TOPK)
        z = jnp.zeros_like(top[0])
        for r in range(PEER_TOPK):
            z = z + jnp.exp(top[r] - top[0])
        tau_ref[0, :, sl] = top[PEER_TOPK - 1]
        e1_ref[0, :, sl] = jnp.exp(s1 - a[0]) / z
        e2_ref[0, :, sl] = jnp.exp(s2 - b[0])


def _route(h2, wq_t_bf, keys, tt):
    T = h2.shape[0]
    arr = jax.ShapeDtypeStruct((PEER_HEADS, N_KEYS, T), F32)
    spec = pl.BlockSpec((1, N_KEYS, tt), lambda i, h: (h, 0, i))
    return pl.pallas_call(
        _route_kernel,
        grid=(T // tt, PEER_HEADS),
        in_specs=[
            pl.BlockSpec((tt, D_MODEL), lambda i, h: (i, 0)),
            pl.BlockSpec((PEER_QUERY, D_MODEL), lambda i, h: (h, 0)),
            pl.BlockSpec((1, 2, N_KEYS, PEER_HALF), lambda i, h: (h, 0, 0, 0)),
        ],
        out_specs=[spec, spec, spec, spec, pl.BlockSpec((1, 1, tt), lambda i, h: (h, 0, i))],
        out_shape=[arr, arr, arr, arr, jax.ShapeDtypeStruct((PEER_HEADS, 1, T), F32)],
        compiler_params=_cparams("parallel", "arbitrary"),
        name="peer_route",
    )(h2, wq_t_bf, keys)


PEER_EB = 512


def _peer_kernel(h2_ref, s1_ref, s2_ref, e1_ref, e2_ref, tau_ref, u_ref, vt_ref, o_ref, acc_scr):
    e = pl.program_id(1)
    tt = h2_ref.shape[0]

    @pl.when(e == 0)
    def _():
        acc_scr[...] = jnp.zeros(acc_scr.shape, F32)

    act = _gelu(_dot_nt(u_ref[...], h2_ref[...]))
    parts = []
    for r in range(PEER_EB // N_KEYS):
        i1 = e * (PEER_EB // N_KEYS) + r
        gate = jnp.zeros((N_KEYS, tt), F32)
        for h in range(PEER_HEADS):
            s = s1_ref[h, pl.ds(i1, 1), :] + s2_ref[h]
            w = e1_ref[h, pl.ds(i1, 1), :] * e2_ref[h]
            gate = gate + jnp.where(s >= tau_ref[h], w, 0.0)
        parts.append((gate * act[r * N_KEYS:(r + 1) * N_KEYS, :]).astype(BF16))
    coef = jnp.concatenate(parts, axis=0)
    acc_scr[...] += _dot(vt_ref[...], coef)

    @pl.when(e == pl.num_programs(1) - 1)
    def _():
        o_ref[...] = acc_scr[...].T.astype(BF16)


def _peer(h2, route, u_bf, vt_bf, tt):
    T = h2.shape[0]
    s1, s2, e1, e2, tau = route
    spec = pl.BlockSpec((PEER_HEADS, N_KEYS, tt), lambda i, e: (0, 0, i))
    return pl.pallas_call(
        _peer_kernel,
        grid=(T // tt, N_EXPERTS // PEER_EB),
        in_specs=[
            pl.BlockSpec((tt, D_MODEL), lambda i, e: (i, 0)),
            spec, spec, spec, spec,
            pl.BlockSpec((PEER_HEADS, 1, tt), lambda i, e: (0, 0, i)),
            pl.BlockSpec((PEER_EB, D_MODEL), lambda i, e: (e, 0)),
            pl.BlockSpec((D_MODEL, PEER_EB), lambda i, e: (0, e)),
        ],
        out_specs=pl.BlockSpec((tt, D_MODEL), lambda i, e: (i, 0)),
        out_shape=jax.ShapeDtypeStruct((T, D_MODEL), BF16),
        scratch_shapes=[pltpu.VMEM((D_MODEL, tt), F32)],
        compiler_params=_cparams("parallel", "arbitrary"),
        name="peer_dense",
    )(h2, s1, s2, e1, e2, tau, u_bf, vt_bf)


def _ple_kernel(x1_ref, po_ref, p_ref, wg_ref, wp_ref, gp_ref, gf_ref, y_ref):
    x2 = x1_ref[...] + po_ref[...].astype(F32)
    h3 = _rms(x2, gp_ref[...]).astype(BF16)
    gate = _sigmoid(_dot(h3, wg_ref[...]))
    proj = _dot(p_ref[...].astype(BF16), wp_ref[...])
    y_ref[...] = _rms(x2 + proj * gate, gf_ref[...])


def _ple(x1, peer_out, p2d, wg_bf, wp_bf, g_ple, g_final, tm):
    T = x1.shape[0]
    row = lambda n: pl.BlockSpec((tm, n), lambda i: (i, 0))
    const = lambda s: pl.BlockSpec(s, lambda i: (0, 0))
    return pl.pallas_call(
        _ple_kernel,
        grid=(T // tm,),
        in_specs=[row(D_MODEL), row(D_MODEL), row(PLE_DIM),
                  const((D_MODEL, D_MODEL)), const((PLE_DIM, D_MODEL)), const((1, D_MODEL)), const((1, D_MODEL))],
        out_specs=row(D_MODEL),
        out_shape=jax.ShapeDtypeStruct((T, D_MODEL), F32),
        compiler_params=_cparams("parallel"),
        name="ple_final",
    )(x1, peer_out, p2d, wg_bf, wp_bf, g_ple, g_final)


def _tile(n, pref):
    t = min(n, pref)
    assert n % t == 0, (n, t)
    return t


def _layer(x, p, h0_re, h0_im, k_past, v_past, lp, lambda_init):
    bn, seq, _ = x.shape
    past = k_past.shape[1]
    T = bn * seq
    x2d = x.reshape(T, D_MODEL)
    row = lambda a: a.astype(F32).reshape(1, -1)

    tm_in = _tile(T, 1024)
    cos_t, sin_t = _rope_tables(seq, past, max(seq, tm_in))
    u, q, k_f32, k_bf, v_f32, v_bf, sig_a, sig_b = _inproj(
        x2d, row(lp['g_mix']), lp['w_in'].astype(BF16), cos_t, sin_t, tm_in)

    seg_len = _tile(seq // SUBLANES, 32)
    h0 = jnp.concatenate([h0_re.reshape(bn, 1, N_STATE), h0_im.reshape(bn, 1, N_STATE)], axis=-1).astype(F32)
    y_ssm, h_fin = _s5(u.reshape(bn, seq, D_SSM), h0, _s5_tables(lp, seg_len), seg_len)
    gated_a = _glu(y_ssm.reshape(T, D_SSM), lp['w_glu'].astype(BF16), sig_a, _tile(T, 1024))

    k_all = jnp.concatenate([k_past.reshape(bn, past, Q_COLS).astype(BF16), k_bf.reshape(bn, seq, Q_COLS)], axis=1)
    v_all = jnp.concatenate([v_past.reshape(bn, past, V_COLS).astype(BF16), v_bf.reshape(bn, seq, V_COLS)], axis=1)
    tq = _tile(seq, 512)
    tk = tq if past == 0 else past + seq
    o = _attention(q.reshape(bn, seq, Q_COLS), k_all, v_all, lp, tq, tk, past, lambda_init)

    x1, h2 = _merge(x2d, o.reshape(T, V_COLS), gated_a, sig_b, lp['w_attn_out'].astype(BF16),
                    lp['w_out'].astype(BF16), row(lp['g_ffn']), _tile(T, 256))

    tt = _tile(T, 512)
    route = _route(h2, lp['peer_w_q'].T.astype(BF16), lp['peer_keys'].astype(F32), tt)
    peer_out = _peer(h2, route, lp['peer_u'].astype(BF16), lp['peer_v'].T.astype(BF16), tt)

    y = _ple(x1, peer_out, p.reshape(T, PLE_DIM), lp['w_ple_gate'].astype(BF16), lp['w_ple_proj'].astype(BF16),
             row(lp['g_ple']), row(lp['g_final']), _tile(T, 256))

    new_k = k_f32.reshape(bn, seq, N_HEADS, 2 * HEAD_DIM)
    new_v = v_f32.reshape(bn, seq, N_HEADS, V_DIM)
    hr = h_fin[:, 0, :N_STATE].reshape(bn, N_SSM_GROUPS, SSM_STATE)
    hi = h_fin[:, 0, N_STATE:].reshape(bn, N_SSM_GROUPS, SSM_STATE)
    return y.reshape(bn, seq, D_MODEL), new_k, new_v, hr, hi


def _trunk(x, p, h0_re, h0_im, k_past, v_past, lp):
    assert DEPTH == 1
    lambda_init = 0.8 - 0.6 * math.exp(-0.3 * 0)
    y, k_new, v_new, hr, hi = _layer(x, p[0], h0_re[0], h0_im[0], k_past[0], v_past[0], lp, lambda_init)
    return y, k_new[None], v_new[None], hr[None], hi[None]


def kernel(x_prompt, x_sample, p_prompt, p_sample, cache_k, cache_v, state_ssm_re, state_ssm_im,
           g_mix_norm, w_in, ssm_lambda_re, ssm_lambda_im, ssm_log_step, ssm_b_re, ssm_b_im,
           ssm_c_re, ssm_c_im, ssm_d, w_glu, diff_lambda_q1, diff_lambda_k1, diff_lambda_q2,
           diff_lambda_k2, g_subln, w_attn_out, w_out, g_ffn_norm, peer_w_q, peer_keys, peer_u,
           peer_v, g_ple_norm, w_ple_gate, w_ple_proj, g_final):
    lp = dict(g_mix=g_mix_norm[0], w_in=w_in[0], lam_re=ssm_lambda_re[0], lam_im=ssm_lambda_im[0],
              log_step=ssm_log_step[0], b_re=ssm_b_re[0], b_im=ssm_b_im[0], c_re=ssm_c_re[0],
              c_im=ssm_c_im[0], d=ssm_d[0], w_glu=w_glu[0], lq1=diff_lambda_q1[0],
              lk1=diff_lambda_k1[0], lq2=diff_lambda_q2[0], lk2=diff_lambda_k2[0],
              g_subln=g_subln[0], w_attn_out=w_attn_out[0], w_out=w_out[0], g_ffn=g_ffn_norm[0],
              peer_w_q=peer_w_q[0], peer_keys=peer_keys[0], peer_u=peer_u[0], peer_v=peer_v[0],
              g_ple=g_ple_norm[0], w_ple_gate=w_ple_gate[0], w_ple_proj=w_ple_proj[0], g_final=g_final)
    bn = x_prompt.shape[0]
    zeros_state = jnp.zeros((DEPTH, bn, N_SSM_GROUPS, SSM_STATE), F32)
    k_none = jnp.zeros((DEPTH, bn, 0, N_HEADS, 2 * HEAD_DIM), x_prompt.dtype)
    v_none = jnp.zeros((DEPTH, bn, 0, N_HEADS, V_DIM), x_prompt.dtype)
    y_p, k_p, v_p, r_p, i_p = _trunk(x_prompt, p_prompt, zeros_state, zeros_state, k_none, v_none, lp)
    y_s, k_s, v_s, r_s, i_s = _trunk(x_sample, p_sample, state_ssm_re, state_ssm_im, cache_k, cache_v, lp)
    return (y_p, y_s, k_p, v_p, r_p, i_p, k_s, v_s, r_s, i_s)
```

```python
import functools
import math

import jax
import jax.numpy as jnp
from jax import lax
from jax.experimental import pallas as pl
from jax.experimental.pallas import tpu as pltpu

F32 = jnp.float32
BF16 = jnp.bfloat16

D_MODEL = 2048
DEPTH = 1
CHUNK = 64
CHUNK_SHIFT = CHUNK.bit_length() - 1
assert 1 << CHUNK_SHIFT == CHUNK
PLE_DIM = 256
RMS_EPS = 1e-6
NEG_INF = -1e30
D_SSM = 1024
SSM_GROUP = 16
N_SSM_GROUPS = D_SSM // SSM_GROUP
SSM_STATE = 64
N_STATE = N_SSM_GROUPS * SSM_STATE
N_HEADS = 8
HEAD_DIM = 64
V_DIM = 2 * HEAD_DIM
ROPE_THETA = 10000.0
Q_COLS = N_HEADS * 2 * HEAD_DIM
V_COLS = N_HEADS * V_DIM
IN_COLS = D_SSM + 2 * Q_COLS + V_COLS + 2 * D_MODEL
PEER_HEADS = 8
N_KEYS = 128
N_EXPERTS = N_KEYS * N_KEYS
PEER_QUERY = 256
PEER_HALF = PEER_QUERY // 2
PEER_TOPK = 16

LANES = 128
SUBLANES = 8
VMEM_LIMIT = 56 * 1024 * 1024


def _cparams(*sem):
    return pltpu.CompilerParams(dimension_semantics=sem, vmem_limit_bytes=VMEM_LIMIT)


def _rms(x, g):
    return x * lax.rsqrt(jnp.mean(x * x, axis=-1, keepdims=True) + RMS_EPS) * g


def _gelu(x):
    return 0.5 * x * (1.0 + lax.erf(x * (2.0 ** -0.5)))


def _sigmoid(x):
    return 1.0 / (1.0 + jnp.exp(-x))


def _dot(a, b):
    return jnp.dot(a, b, preferred_element_type=F32)


def _dot_nt(a, b):
    return lax.dot_general(a, b, (((1,), (1,)), ((), ())), preferred_element_type=F32)


IN_TN = 512
SEG_U = (0, D_SSM // IN_TN)
SEG_Q = (SEG_U[0] + SEG_U[1], Q_COLS // IN_TN)
SEG_K = (SEG_Q[0] + SEG_Q[1], Q_COLS // IN_TN)
SEG_V = (SEG_K[0] + SEG_K[1], V_COLS // IN_TN)
SEG_GA = (SEG_V[0] + SEG_V[1], D_MODEL // IN_TN)
SEG_GB = (SEG_GA[0] + SEG_GA[1], D_MODEL // IN_TN)


def _inproj_kernel(x_ref, g_ref, w_ref, cos_ref, sin_ref,
                   u_ref, q_ref, kf_ref, kb_ref, vf_ref, vb_ref, ga_ref, gb_ref, h_scr):
    j = pl.program_id(1)

    @pl.when(j == 0)
    def _():
        h_scr[...] = _rms(x_ref[...], g_ref[...]).astype(BF16)

    def z():
        return _dot(h_scr[...], w_ref[...])

    def rope(t):
        n = t.shape[-1]
        lane = lax.broadcasted_iota(jnp.int32, t.shape, 1)
        first = jnp.bitwise_and(lane, HEAD_DIM - 1) < (HEAD_DIM // 2)
        partner = jnp.where(first, pltpu.roll(t, n - HEAD_DIM // 2, 1), pltpu.roll(t, HEAD_DIM // 2, 1))
        return t * cos_ref[...] + partner * sin_ref[...]

    def in_seg(seg):
        return jnp.logical_and(j >= seg[0], j < seg[0] + seg[1])

    @pl.when(in_seg(SEG_U))
    def _():
        u_ref[...] = z().astype(BF16)

    @pl.when(in_seg(SEG_Q))
    def _():
        q_ref[...] = (rope(z()) * (HEAD_DIM ** -0.5)).astype(BF16)

    @pl.when(in_seg(SEG_K))
    def _():
        k = rope(z())
        kf_ref[...] = k
        kb_ref[...] = k.astype(BF16)

    @pl.when(in_seg(SEG_V))
    def _():
        v = z()
        vf_ref[...] = v
        vb_ref[...] = v.astype(BF16)

    @pl.when(in_seg(SEG_GA))
    def _():
        ga_ref[...] = _sigmoid(z()).astype(BF16)

    @pl.when(in_seg(SEG_GB))
    def _():
        gb_ref[...] = _sigmoid(z()).astype(BF16)


def _inproj(x2d, g, w_bf, cos_t, sin_t, tm):
    T = x2d.shape[0]
    nj = IN_COLS // IN_TN
    n_tab = cos_t.shape[0] // tm

    def seg_map(seg):
        return lambda i, j: (i, jnp.clip(j - seg[0], 0, seg[1] - 1))

    def out(seg, dtype):
        return (jax.ShapeDtypeStruct((T, seg[1] * IN_TN), dtype), pl.BlockSpec((tm, IN_TN), seg_map(seg)))

    outs = [out(SEG_U, BF16), out(SEG_Q, BF16), out(SEG_K, F32), out(SEG_K, BF16),
            out(SEG_V, F32), out(SEG_V, BF16), out(SEG_GA, BF16), out(SEG_GB, BF16)]
    return pl.pallas_call(
        _inproj_kernel,
        grid=(T // tm, nj),
        in_specs=[
            pl.BlockSpec((tm, D_MODEL), lambda i, j: (i, 0)),
            pl.BlockSpec((1, D_MODEL), lambda i, j: (0, 0)),
            pl.BlockSpec((D_MODEL, IN_TN), lambda i, j: (0, j)),
            pl.BlockSpec((tm, IN_TN), lambda i, j: (i % n_tab, 0)),
            pl.BlockSpec((tm, IN_TN), lambda i, j: (i % n_tab, 0)),
        ],
        out_specs=[o[1] for o in outs],
        out_shape=[o[0] for o in outs],
        scratch_shapes=[pltpu.VMEM((tm, D_MODEL), BF16)],
        compiler_params=_cparams("parallel", "arbitrary"),
        name="inproj",
    )(x2d, g, w_bf, cos_t, sin_t)


def _rope_tables(seq, past, rows):
    half = HEAD_DIM // 2
    inv = ROPE_THETA ** (-jnp.arange(half, dtype=F32) * 2.0 / HEAD_DIM)
    pos = (past + (jnp.arange(rows, dtype=jnp.int32) % seq)).astype(F32)
    ang = pos[:, None] * inv[None, :]
    cos, sin = jnp.cos(ang), jnp.sin(ang)
    reps = IN_TN // HEAD_DIM
    cos_t = jnp.tile(jnp.concatenate([cos, cos], axis=-1), (1, reps))
    sin_t = jnp.tile(jnp.concatenate([-sin, sin], axis=-1), (1, reps))
    return cos_t, sin_t


S5_COLS = 512
S5_UBLK = LANES
S5_N_UBLK = D_SSM // S5_UBLK
S5_XBLK = (S5_UBLK // SSM_GROUP) * SSM_STATE
S5_YBLK = 256
S5_N_YBLK = D_SSM // S5_YBLK
S5_HBLK = (S5_YBLK // SSM_GROUP) * SSM_STATE


def _s5_kernel(u_ref, h0_ref, perm_ref, permt_ref, wx_ref, apow_ref, cr_ref, ci_ref, d_ref,
               y_ref, hfin_ref, x_scr, c_scr, carry_scr, *, seg_len):
    t = pl.program_id(1)
    rows = SUBLANES * seg_len
    n = N_STATE

    @pl.when(t == 0)
    def _():
        carry_scr[...] = h0_ref[...]

    up = _dot(perm_ref[...], u_ref[...])
    upb = up.astype(BF16)
    for a in range(S5_N_UBLK):
        xa = _dot(upb[:, a * S5_UBLK:(a + 1) * S5_UBLK], wx_ref[a])
        x_scr[:, a * S5_XBLK:(a + 1) * S5_XBLK] = xa[:, :S5_XBLK]
        x_scr[:, n + a * S5_XBLK:n + (a + 1) * S5_XBLK] = xa[:, S5_XBLK:]

    for c in range(n // S5_COLS):
        lo = c * S5_COLS
        ar = jnp.broadcast_to(apow_ref[0:1, lo:lo + S5_COLS], (SUBLANES, S5_COLS))
        ai = jnp.broadcast_to(apow_ref[0:1, n + lo:n + lo + S5_COLS], (SUBLANES, S5_COLS))

        def scan_step(k, carry, lo=lo, ar=ar, ai=ai):
            hr, hi = carry
            r0 = pl.multiple_of(k * SUBLANES, SUBLANES)
            xr = x_scr[pl.ds(r0, SUBLANES), lo:lo + S5_COLS]
            xi = x_scr[pl.ds(r0, SUBLANES), n + lo:n + lo + S5_COLS]
            nr = ar * hr - ai * hi + xr
            ni = ar * hi + ai * hr + xi
            x_scr[pl.ds(r0, SUBLANES), lo:lo + S5_COLS] = nr
            x_scr[pl.ds(r0, SUBLANES), n + lo:n + lo + S5_COLS] = ni
            return nr, ni

        zero = jnp.zeros((SUBLANES, S5_COLS), F32)
        lax.fori_loop(0, seg_len, scan_step, (zero, zero))

    alr = apow_ref[seg_len - 1:seg_len, :n]
    ali = apow_ref[seg_len - 1:seg_len, n:]
    cr = carry_scr[:, :n]
    ci = carry_scr[:, n:]
    for s in range(SUBLANES):
        c_scr[s:s + 1, :n] = cr
        c_scr[s:s + 1, n:] = ci
        lr = x_scr[rows - SUBLANES + s:rows - SUBLANES + s + 1, :n]
        li = x_scr[rows - SUBLANES + s:rows - SUBLANES + s + 1, n:]
        cr, ci = alr * cr - ali * ci + lr, alr * ci + ali * cr + li
    carry_scr[:, :n] = cr
    carry_scr[:, n:] = ci

    for c in range(n // S5_COLS):
        lo = c * S5_COLS
        sr = c_scr[:, lo:lo + S5_COLS]
        si = c_scr[:, n + lo:n + lo + S5_COLS]

        def fix_step(k, _, lo=lo, sr=sr, si=si):
            r0 = pl.multiple_of(k * SUBLANES, SUBLANES)
            pr = apow_ref[pl.ds(k, 1), lo:lo + S5_COLS]
            pi = apow_ref[pl.ds(k, 1), n + lo:n + lo + S5_COLS]
            x_scr[pl.ds(r0, SUBLANES), lo:lo + S5_COLS] += pr * sr - pi * si
            x_scr[pl.ds(r0, SUBLANES), n + lo:n + lo + S5_COLS] += pr * si + pi * sr
            return 0

        lax.fori_loop(0, seg_len, fix_step, 0)

    @pl.when(t == pl.num_programs(1) - 1)
    def _():
        hfin_ref[...] = carry_scr[...]

    for j in range(S5_N_YBLK):
        hr = x_scr[:, j * S5_HBLK:(j + 1) * S5_HBLK].astype(BF16)
        hi = x_scr[:, n + j * S5_HBLK:n + (j + 1) * S5_HBLK].astype(BF16)
        yj = (_dot(hr, cr_ref[j]) + _dot(hi, ci_ref[j])
              + d_ref[:, j * S5_YBLK:(j + 1) * S5_YBLK] * up[:, j * S5_YBLK:(j + 1) * S5_YBLK])
        yj = _gelu(yj).astype(BF16)
        y_ref[:, j * S5_YBLK:(j + 1) * S5_YBLK] = _dot(permt_ref[...], yj).astype(BF16)


def _s5_tables(lp, seg_len):
    g, p, c = N_SSM_GROUPS, SSM_STATE, SSM_GROUP
    dt = jnp.exp(lp['log_step'].astype(F32))[:, None]
    lr, li = lp['lam_re'].astype(F32), lp['lam_im'].astype(F32)
    mag = jnp.exp(lr * dt)
    ar, ai = mag * jnp.cos(li * dt), mag * jnp.sin(li * dt)
    den = lr * lr + li * li
    fr = ((ar - 1.0) * lr + ai * li) / den
    fi = (ai * lr - (ar - 1.0) * li) / den
    br, bi = lp['b_re'].astype(F32), lp['b_im'].astype(F32)
    bbr = fr[..., None] * br - fi[..., None] * bi
    bbi = fr[..., None] * bi + fi[..., None] * br

    gpb = S5_UBLK // c
    eye = jnp.eye(gpb, dtype=F32)

    def in_blocks(bb):
        bb = bb.reshape(S5_N_UBLK, gpb, p, c)
        return jnp.einsum('xy,axpc->axcyp', eye, bb).reshape(S5_N_UBLK, S5_UBLK, S5_XBLK)

    wx = jnp.concatenate([in_blocks(bbr), in_blocks(bbi)], axis=-1).astype(BF16)

    pr, pi = ar.reshape(1, g * p), ai.reshape(1, g * p)
    while pr.shape[0] < seg_len:
        tr, ti = pr[-1:], pi[-1:]
        pr, pi = (jnp.concatenate([pr, pr * tr - pi * ti], axis=0),
                  jnp.concatenate([pi, pr * ti + pi * tr], axis=0))
    apow = jnp.concatenate([pr[:seg_len], pi[:seg_len]], axis=-1)

    gpy = S5_YBLK // c
    eye_y = jnp.eye(gpy, dtype=F32)

    def out_blocks(cm):
        cm = cm.astype(F32).reshape(S5_N_YBLK, gpy, c, p)
        return jnp.einsum('xy,jxcp->jxpyc', eye_y, cm).reshape(S5_N_YBLK, S5_HBLK, S5_YBLK)

    cr = out_blocks(lp['c_re']).astype(BF16)
    ci = (-out_blocks(lp['c_im'])).astype(BF16)
    d = lp['d'].astype(F32).reshape(1, D_SSM)
    return wx, apow, cr, ci, d


def _s5(u, h0, tables, seg_len):
    bn, seq, _ = u.shape
    rows = SUBLANES * seg_len
    wx, apow, cr, ci, d = tables
    r = jnp.arange(rows)
    perm = (r[None, :] == ((r % SUBLANES) * seg_len + r // SUBLANES)[:, None]).astype(BF16)
    const2 = lambda b, t: (0, 0)
    const3 = lambda b, t: (0, 0, 0)
    return pl.pallas_call(
        functools.partial(_s5_kernel, seg_len=seg_len),
        grid=(bn, seq // rows),
        in_specs=[
            pl.BlockSpec((None, rows, D_SSM), lambda b, t: (b, t, 0)),
            pl.BlockSpec((None, 1, 2 * N_STATE), lambda b, t: (b, 0, 0)),
            pl.BlockSpec((rows, rows), const2),
            pl.BlockSpec((rows, rows), const2),
            pl.BlockSpec(wx.shape, const3),
            pl.BlockSpec(apow.shape, const2),
            pl.BlockSpec(cr.shape, const3),
            pl.BlockSpec(ci.shape, const3),
            pl.BlockSpec((1, D_SSM), const2),
        ],
        out_specs=[
            pl.BlockSpec((None, rows, D_SSM), lambda b, t: (b, t, 0)),
            pl.BlockSpec((None, 1, 2 * N_STATE), lambda b, t: (b, 0, 0)),
        ],
        out_shape=[jax.ShapeDtypeStruct((bn, seq, D_SSM), BF16),
                   jax.ShapeDtypeStruct((bn, 1, 2 * N_STATE), F32)],
        scratch_shapes=[pltpu.VMEM((rows, 2 * N_STATE), F32),
                        pltpu.VMEM((SUBLANES, 2 * N_STATE), F32),
                        pltpu.VMEM((1, 2 * N_STATE), F32)],
        compiler_params=_cparams("parallel", "arbitrary"),
        name="s5",
    )(u, h0, perm, perm.T, wx, apow, cr, ci, d)


GLU_TN = 512


def _glu_kernel(y_ref, wa_ref, wb_ref, ga_ref, o_ref):
    y = y_ref[...]
    a = _dot(y, wa_ref[...])
    b = _dot(y, wb_ref[...])
    o_ref[...] = (ga_ref[...].astype(F32) * a * _sigmoid(b)).astype(BF16)


def _glu(y, w_glu_bf, sig_a, tm):
    T = y.shape[0]
    nj = D_MODEL // GLU_TN
    return pl.pallas_call(
        _glu_kernel,
        grid=(T // tm, nj),
        in_specs=[
            pl.BlockSpec((tm, D_SSM), lambda i, j: (i, 0)),
            pl.BlockSpec((D_SSM, GLU_TN), lambda i, j: (0, j)),
            pl.BlockSpec((D_SSM, GLU_TN), lambda i, j: (0, j + nj)),
            pl.BlockSpec((tm, GLU_TN), lambda i, j: (i, j)),
        ],
        out_specs=pl.BlockSpec((tm, GLU_TN), lambda i, j: (i, j)),
        out_shape=jax.ShapeDtypeStruct((T, D_MODEL), BF16),
        compiler_params=_cparams("parallel", "arbitrary"),
        name="glu",
    )(y, w_glu_bf, w_glu_bf, sig_a)


def _attn_kernel(q_ref, k_ref, v_ref, lq1_ref, lk1_ref, lq2_ref, lk2_ref, g_ref, o_ref,
                 qt_scr, m_scr, l_scr, acc_scr, *, tq, tk, past, lambda_init):
    i = pl.program_id(2)
    nk = k_ref.shape[0] // tk
    q = q_ref[...].astype(F32)
    lane = lax.broadcasted_iota(jnp.int32, q.shape, 1)
    qs = jnp.concatenate([jnp.where(lane < HEAD_DIM, q, 0.0), jnp.where(lane >= HEAD_DIM, q, 0.0)], axis=0)
    qt_scr[...] = qs.T.astype(BF16)

    m_scr[...] = jnp.full(m_scr.shape, NEG_INF, F32)
    l_scr[...] = jnp.zeros(l_scr.shape, F32)
    acc_scr[...] = jnp.zeros(acc_scr.shape, F32)

    q_lo = past + i * tq
    min_qc = q_lo // CHUNK
    max_qc = (q_lo + tq - 1) // CHUNK
    n_proc = jnp.minimum(nk, (max_qc * CHUNK + CHUNK - 1) // tk + 1)
    n_full = jnp.clip((min_qc * CHUNK + CHUNK) // tk, 0, n_proc)

    def block(j, masked):
        k0 = pl.multiple_of(j * tk, tk)
        kb = k_ref[pl.ds(k0, tk), :]
        vb = v_ref[pl.ds(k0, tk), :]
        s = _dot(kb, qt_scr[...])
        if masked:
            kpos = k0 + lax.broadcasted_iota(jnp.int32, s.shape, 0)
            qpos = q_lo + jnp.bitwise_and(lax.broadcasted_iota(jnp.int32, s.shape, 1), tq - 1)
            s = jnp.where(jnp.right_shift(kpos, CHUNK_SHIFT) <= jnp.right_shift(qpos, CHUNK_SHIFT), s, NEG_INF)
        m_old = m_scr[...]
        m_new = jnp.maximum(m_old, jnp.max(s, axis=0, keepdims=True))
        alpha = jnp.exp(m_old - m_new)
        p = jnp.exp(s - m_new)
        l_scr[...] = alpha * l_scr[...] + jnp.sum(p, axis=0, keepdims=True)
        pv = lax.dot_general(vb, p.astype(BF16), (((0,), (0,)), ((), ())), preferred_element_type=F32)
        acc_scr[...] = alpha * acc_scr[...] + pv
        m_scr[...] = m_new

    def full_body(j, c):
        block(j, False)
        return c

    def masked_body(j, c):
        block(j, True)
        return c

    lax.fori_loop(0, n_full, full_body, 0)
    lax.fori_loop(n_full, n_proc, masked_body, 0)

    lam = (jnp.exp(jnp.sum(lq1_ref[...] * lk1_ref[...], axis=-1, keepdims=True))
           - jnp.exp(jnp.sum(lq2_ref[...] * lk2_ref[...], axis=-1, keepdims=True)) + lambda_init)
    ot = acc_scr[:, :tq] / l_scr[:, :tq] - lam * (acc_scr[:, tq:] / l_scr[:, tq:])
    o_ref[...] = (_rms(ot.T, g_ref[...]) * (1.0 - lambda_init)).astype(BF16)


def _attention(q, k, v, lp, tq, tk, past, lambda_init):
    bn, seq, _ = q.shape
    lk = k.shape[1]
    vec = lambda a: a.astype(F32).reshape(1, -1)
    small = lambda n: pl.BlockSpec((1, n), lambda b, h, i: (0, 0))
    return pl.pallas_call(
        functools.partial(_attn_kernel, tq=tq, tk=tk, past=past, lambda_init=lambda_init),
        grid=(bn, N_HEADS, seq // tq),
        in_specs=[
            pl.BlockSpec((None, tq, V_DIM), lambda b, h, i: (b, i, h)),
            pl.BlockSpec((None, lk, V_DIM), lambda b, h, i: (b, 0, h)),
            pl.BlockSpec((None, lk, V_DIM), lambda b, h, i: (b, 0, h)),
            small(HEAD_DIM), small(HEAD_DIM), small(HEAD_DIM), small(HEAD_DIM), small(V_DIM),
        ],
        out_specs=pl.BlockSpec((None, tq, V_DIM), lambda b, h, i: (b, i, h)),
        out_shape=jax.ShapeDtypeStruct((bn, seq, V_COLS), BF16),
        scratch_shapes=[pltpu.VMEM((V_DIM, 2 * tq), BF16), pltpu.VMEM((1, 2 * tq), F32),
                        pltpu.VMEM((1, 2 * tq), F32), pltpu.VMEM((V_DIM, 2 * tq), F32)],
        compiler_params=_cparams("parallel", "parallel", "arbitrary"),
        name="diff_attn",
    )(q, k, v, vec(lp['lq1']), vec(lp['lk1']), vec(lp['lq2']), vec(lp['lk2']), vec(lp['g_subln']))


def _merge_kernel(x_ref, o_ref, ga_ref, sb_ref, wa_ref, wo_ref, g_ref, x1_ref, h2_ref):
    branch_b = _dot(o_ref[...], wa_ref[...])
    merged = ga_ref[...].astype(F32) + sb_ref[...].astype(F32) * branch_b
    x1 = x_ref[...] + _dot(merged.astype(BF16), wo_ref[...])
    x1_ref[...] = x1
    h2_ref[...] = _rms(x1, g_ref[...]).astype(BF16)


def _merge(x2d, o, gated_a, sig_b, wa_bf, wo_bf, g_ffn, tm):
    T = x2d.shape[0]
    row = lambda n: pl.BlockSpec((tm, n), lambda i: (i, 0))
    const = lambda s: pl.BlockSpec(s, lambda i: (0, 0))
    return pl.pallas_call(
        _merge_kernel,
        grid=(T // tm,),
        in_specs=[row(D_MODEL), row(V_COLS), row(D_MODEL), row(D_MODEL),
                  const((V_COLS, D_MODEL)), const((D_MODEL, D_MODEL)), const((1, D_MODEL))],
        out_specs=[row(D_MODEL), row(D_MODEL)],
        out_shape=[jax.ShapeDtypeStruct((T, D_MODEL), F32), jax.ShapeDtypeStruct((T, D_MODEL), BF16)],
        compiler_params=_cparams("parallel"),
        name="merge_out",
    )(x2d, o, gated_a, sig_b, wa_bf, wo_bf, g_ffn)


ROUTE_LC = 256
TAKEN = -3.0e38


def _split_bf16(x):
    hi = x.astype(BF16)
    return hi, (x - hi.astype(F32)).astype(BF16)


def _top_rows(s, k):
    out = []
    for _ in range(k):
        m = jnp.max(s, axis=0, keepdims=True)
        out.append(m)
        s = jnp.where(s == m, TAKEN, s)
    return out


def _stack_rows(rows):
    n, w = len(rows), rows[0].shape[1]
    idx = lax.broadcasted_iota(jnp.int32, (n, w), 0)
    out = jnp.zeros((n, w), F32)
    for r, v in enumerate(rows):
        out = jnp.where(idx == r, v, out)
    return out


def _route_kernel(h2_ref, wq_ref, keys_ref, s1_ref, s2_ref, e1_ref, e2_ref, tau_ref):
    tt = h2_ref.shape[0]
    qt = _dot_nt(wq_ref[...], h2_ref[...])
    halves = []
    for c in range(2):
        q_hi, q_lo = _split_bf16(qt[c * PEER_HALF:(c + 1) * PEER_HALF, :])
        k_hi, k_lo = _split_bf16(keys_ref[0, c])
        halves.append(_dot(k_hi, q_hi) + _dot(k_hi, q_lo) + _dot(k_lo, q_hi))
    s1_ref[0] = halves[0]
    s2_ref[0] = halves[1]

    width = min(tt, ROUTE_LC)
    for lc in range(tt // width):
        sl = slice(lc * width, (lc + 1) * width)
        s1 = s1_ref[0, :, sl]
        s2 = s2_ref[0, :, sl]
        a = _top_rows(s1, PEER_TOPK)
        b = _top_rows(s2, PEER_TOPK)
        b_all = _stack_rows(b)
        cand = jnp.concatenate([a[0] + b_all] + [a[i] + b_all[:SUBLANES] for i in range(1, PEER_TOPK)], axis=0)
        top = _top_rows(cand, PEER_TOPK)
        z = jnp.zeros_like(top[0])
        for r in range(PEER_TOPK):
            z = z + jnp.exp(top[r] - top[0])
        tau_ref[0, :, sl] = top[PEER_TOPK - 1]
        e1_ref[0, :, sl] = jnp.exp(s1 - a[0]) / z
        e2_ref[0, :, sl] = jnp.exp(s2 - b[0])


def _route(h2, wq_t_bf, keys, tt):
    T = h2.shape[0]
    arr = jax.ShapeDtypeStruct((PEER_HEADS, N_KEYS, T), F32)
    spec = pl.BlockSpec((1, N_KEYS, tt), lambda i, h: (h, 0, i))
    return pl.pallas_call(
        _route_kernel,
        grid=(T // tt, PEER_HEADS),
        in_specs=[
            pl.BlockSpec((tt, D_MODEL), lambda i, h: (i, 0)),
            pl.BlockSpec((PEER_QUERY, D_MODEL), lambda i, h: (h, 0)),
            pl.BlockSpec((1, 2, N_KEYS, PEER_HALF), lambda i, h: (h, 0, 0, 0)),
        ],
        out_specs=[spec, spec, spec, spec, pl.BlockSpec((1, 1, tt), lambda i, h: (h, 0, i))],
        out_shape=[arr, arr, arr, arr, jax.ShapeDtypeStruct((PEER_HEADS, 1, T), F32)],
        compiler_params=_cparams("parallel", "arbitrary"),
        name="peer_route",
    )(h2, wq_t_bf, keys)


PEER_EB = 512


def _peer_kernel(h2_ref, s1_ref, s2_ref, e1_ref, e2_ref, tau_ref, u_ref, vt_ref, o_ref, acc_scr):
    e = pl.program_id(1)
    tt = h2_ref.shape[0]

    @pl.when(e == 0)
    def _():
        acc_scr[...] = jnp.zeros(acc_scr.shape, F32)

    act = _gelu(_dot_nt(u_ref[...], h2_ref[...]))
    parts = []
    for r in range(PEER_EB // N_KEYS):
        i1 = e * (PEER_EB // N_KEYS) + r
        gate = jnp.zeros((N_KEYS, tt), F32)
        for h in range(PEER_HEADS):
            s = s1_ref[h, pl.ds(i1, 1), :] + s2_ref[h]
            w = e1_ref[h, pl.ds(i1, 1), :] * e2_ref[h]
            gate = gate + jnp.where(s >= tau_ref[h], w, 0.0)
        parts.append((gate * act[r * N_KEYS:(r + 1) * N_KEYS, :]).astype(BF16))
    coef = jnp.concatenate(parts, axis=0)
    acc_scr[...] += _dot(vt_ref[...], coef)

    @pl.when(e == pl.num_programs(1) - 1)
    def _():
        o_ref[...] = acc_scr[...].T.astype(BF16)


def _peer(h2, route, u_bf, vt_bf, tt):
    T = h2.shape[0]
    s1, s2, e1, e2, tau = route
    spec = pl.BlockSpec((PEER_HEADS, N_KEYS, tt), lambda i, e: (0, 0, i))
    return pl.pallas_call(
        _peer_kernel,
        grid=(T // tt, N_EXPERTS // PEER_EB),
        in_specs=[
            pl.BlockSpec((tt, D_MODEL), lambda i, e: (i, 0)),
            spec, spec, spec, spec,
            pl.BlockSpec((PEER_HEADS, 1, tt), lambda i, e: (0, 0, i)),
            pl.BlockSpec((PEER_EB, D_MODEL), lambda i, e: (e, 0)),
            pl.BlockSpec((D_MODEL, PEER_EB), lambda i, e: (0, e)),
        ],
        out_specs=pl.BlockSpec((tt, D_MODEL), lambda i, e: (i, 0)),
        out_shape=jax.ShapeDtypeStruct((T, D_MODEL), BF16),
        scratch_shapes=[pltpu.VMEM((D_MODEL, tt), F32)],
        compiler_params=_cparams("parallel", "arbitrary"),
        name="peer_dense",
    )(h2, s1, s2, e1, e2, tau, u_bf, vt_bf)


def _ple_kernel(x1_ref, po_ref, p_ref, wg_ref, wp_ref, gp_ref, gf_ref, y_ref):
    x2 = x1_ref[...] + po_ref[...].astype(F32)
    h3 = _rms(x2, gp_ref[...]).astype(BF16)
    gate = _sigmoid(_dot(h3, wg_ref[...]))
    proj = _dot(p_ref[...].astype(BF16), wp_ref[...])
    y_ref[...] = _rms(x2 + proj * gate, gf_ref[...])


def _ple(x1, peer_out, p2d, wg_bf, wp_bf, g_ple, g_final, tm):
    T = x1.shape[0]
    row = lambda n: pl.BlockSpec((tm, n), lambda i: (i, 0))
    const = lambda s: pl.BlockSpec(s, lambda i: (0, 0))
    return pl.pallas_call(
        _ple_kernel,
        grid=(T // tm,),
        in_specs=[row(D_MODEL), row(D_MODEL), row(PLE_DIM),
                  const((D_MODEL, D_MODEL)), const((PLE_DIM, D_MODEL)), const((1, D_MODEL)), const((1, D_MODEL))],
        out_specs=row(D_MODEL),
        out_shape=jax.ShapeDtypeStruct((T, D_MODEL), F32),
        compiler_params=_cparams("parallel"),
        name="ple_final",
    )(x1, peer_out, p2d, wg_bf, wp_bf, g_ple, g_final)


def _tile(n, pref):
    t = min(n, pref)
    assert n % t == 0, (n, t)
    return t


def _layer(x, p, h0_re, h0_im, k_past, v_past, lp, lambda_init):
    bn, seq, _ = x.shape
    past = k_past.shape[1]
    T = bn * seq
    x2d = x.reshape(T, D_MODEL)
    row = lambda a: a.astype(F32).reshape(1, -1)

    tm_in = _tile(T, 1024)
    cos_t, sin_t = _rope_tables(seq, past, max(seq, tm_in))
    u, q, k_f32, k_bf, v_f32, v_bf, sig_a, sig_b = _inproj(
        x2d, row(lp['g_mix']), lp['w_in'].astype(BF16), cos_t, sin_t, tm_in)

    seg_len = _tile(seq // SUBLANES, 32)
    h0 = jnp.concatenate([h0_re.reshape(bn, 1, N_STATE), h0_im.reshape(bn, 1, N_STATE)], axis=-1).astype(F32)
    y_ssm, h_fin = _s5(u.reshape(bn, seq, D_SSM), h0, _s5_tables(lp, seg_len), seg_len)
    gated_a = _glu(y_ssm.reshape(T, D_SSM), lp['w_glu'].astype(BF16), sig_a, _tile(T, 1024))

    k_all = jnp.concatenate([k_past.reshape(bn, past, Q_COLS).astype(BF16), k_bf.reshape(bn, seq, Q_COLS)], axis=1)
    v_all = jnp.concatenate([v_past.reshape(bn, past, V_COLS).astype(BF16), v_bf.reshape(bn, seq, V_COLS)], axis=1)
    seq_q = max(seq, LANES // 2)
    q3 = jnp.pad(q.reshape(bn, seq, Q_COLS), ((0, 0), (0, seq_q - seq), (0, 0)))
    tq = _tile(seq_q, 512)
    tk = tq if past == 0 else past + seq
    o = _attention(q3, k_all, v_all, lp, tq, tk, past, lambda_init)[:, :seq]

    x1, h2 = _merge(x2d, o.reshape(T, V_COLS), gated_a, sig_b, lp['w_attn_out'].astype(BF16),
                    lp['w_out'].astype(BF16), row(lp['g_ffn']), _tile(T, 256))

    tt = _tile(T, 512)
    route = _route(h2, lp['peer_w_q'].T.astype(BF16), lp['peer_keys'].astype(F32), tt)
    peer_out = _peer(h2, route, lp['peer_u'].astype(BF16), lp['peer_v'].T.astype(BF16), tt)

    y = _ple(x1, peer_out, p.reshape(T, PLE_DIM), lp['w_ple_gate'].astype(BF16), lp['w_ple_proj'].astype(BF16),
             row(lp['g_ple']), row(lp['g_final']), _tile(T, 256))

    new_k = k_f32.reshape(bn, seq, N_HEADS, 2 * HEAD_DIM)
    new_v = v_f32.reshape(bn, seq, N_HEADS, V_DIM)
    hr = h_fin[:, 0, :N_STATE].reshape(bn, N_SSM_GROUPS, SSM_STATE)
    hi = h_fin[:, 0, N_STATE:].reshape(bn, N_SSM_GROUPS, SSM_STATE)
    return y.reshape(bn, seq, D_MODEL), new_k, new_v, hr, hi


def _trunk(x, p, h0_re, h0_im, k_past, v_past, lp):
    assert DEPTH == 1
    lambda_init = 0.8 - 0.6 * math.exp(-0.3 * 0)
    y, k_new, v_new, hr, hi = _layer(x, p[0], h0_re[0], h0_im[0], k_past[0], v_past[0], lp, lambda_init)
    return y, k_new[None], v_new[None], hr[None], hi[None]


def kernel(x_prompt, x_sample, p_prompt, p_sample, cache_k, cache_v, state_ssm_re, state_ssm_im,
           g_mix_norm, w_in, ssm_lambda_re, ssm_lambda_im, ssm_log_step, ssm_b_re, ssm_b_im,
           ssm_c_re, ssm_c_im, ssm_d, w_glu, diff_lambda_q1, diff_lambda_k1, diff_lambda_q2,
           diff_lambda_k2, g_subln, w_attn_out, w_out, g_ffn_norm, peer_w_q, peer_keys, peer_u,
           peer_v, g_ple_norm, w_ple_gate, w_ple_proj, g_final):
    lp = dict(g_mix=g_mix_norm[0], w_in=w_in[0], lam_re=ssm_lambda_re[0], lam_im=ssm_lambda_im[0],
              log_step=ssm_log_step[0], b_re=ssm_b_re[0], b_im=ssm_b_im[0], c_re=ssm_c_re[0],
              c_im=ssm_c_im[0], d=ssm_d[0], w_glu=w_glu[0], lq1=diff_lambda_q1[0],
              lk1=diff_lambda_k1[0], lq2=diff_lambda_q2[0], lk2=diff_lambda_k2[0],
              g_subln=g_subln[0], w_attn_out=w_attn_out[0], w_out=w_out[0], g_ffn=g_ffn_norm[0],
              peer_w_q=peer_w_q[0], peer_keys=peer_keys[0], peer_u=peer_u[0], peer_v=peer_v[0],
              g_ple=g_ple_norm[0], w_ple_gate=w_ple_gate[0], w_ple_proj=w_ple_proj[0], g_final=g_final)
    bn = x_prompt.shape[0]
    zeros_state = jnp.zeros((DEPTH, bn, N_SSM_GROUPS, SSM_STATE), F32)
    k_none = jnp.zeros((DEPTH, bn, 0, N_HEADS, 2 * HEAD_DIM), x_prompt.dtype)
    v_none = jnp.zeros((DEPTH, bn, 0, N_HEADS, V_DIM), x_prompt.dtype)
    y_p, k_p, v_p, r_p, i_p = _trunk(x_prompt, p_prompt, zeros_state, zeros_state, k_none, v_none, lp)
    y_s, k_s, v_s, r_s, i_s = _trunk(x_sample, p_sample, state_ssm_re, state_ssm_im, cache_k, cache_v, lp)
    return (y_p, y_s, k_p, v_p, r_p, i_p, k_s, v_s, r_s, i_s)
```

```python
import functools
import math

import jax
import jax.numpy as jnp
from jax import lax
from jax.experimental import pallas as pl
from jax.experimental.pallas import tpu as pltpu

F32 = jnp.float32
BF16 = jnp.bfloat16

D_MODEL = 2048
DEPTH = 1
CHUNK = 64
CHUNK_SHIFT = CHUNK.bit_length() - 1
assert 1 << CHUNK_SHIFT == CHUNK
PLE_DIM = 256
RMS_EPS = 1e-6
NEG_INF = -1e30
D_SSM = 1024
SSM_GROUP = 16
N_SSM_GROUPS = D_SSM // SSM_GROUP
SSM_STATE = 64
N_STATE = N_SSM_GROUPS * SSM_STATE
N_HEADS = 8
HEAD_DIM = 64
V_DIM = 2 * HEAD_DIM
ROPE_THETA = 10000.0
Q_COLS = N_HEADS * 2 * HEAD_DIM
V_COLS = N_HEADS * V_DIM
IN_COLS = D_SSM + 2 * Q_COLS + V_COLS + 2 * D_MODEL
PEER_HEADS = 8
N_KEYS = 128
N_EXPERTS = N_KEYS * N_KEYS
PEER_QUERY = 256
PEER_HALF = PEER_QUERY // 2
PEER_TOPK = 16

LANES = 128
SUBLANES = 8
VMEM_LIMIT = 56 * 1024 * 1024


def _cparams(*sem):
    return pltpu.CompilerParams(dimension_semantics=sem, vmem_limit_bytes=VMEM_LIMIT)


def _rms(x, g):
    return x * lax.rsqrt(jnp.mean(x * x, axis=-1, keepdims=True) + RMS_EPS) * g


def _gelu(x):
    return 0.5 * x * (1.0 + lax.erf(x * (2.0 ** -0.5)))


def _sigmoid(x):
    return 1.0 / (1.0 + jnp.exp(-x))


def _dot(a, b):
    return jnp.dot(a, b, preferred_element_type=F32)


def _dot_nt(a, b):
    return lax.dot_general(a, b, (((1,), (1,)), ((), ())), preferred_element_type=F32)


IN_TN = 512
SEG_U = (0, D_SSM // IN_TN)
SEG_Q = (SEG_U[0] + SEG_U[1], Q_COLS // IN_TN)
SEG_K = (SEG_Q[0] + SEG_Q[1], Q_COLS // IN_TN)
SEG_V = (SEG_K[0] + SEG_K[1], V_COLS // IN_TN)
SEG_GA = (SEG_V[0] + SEG_V[1], D_MODEL // IN_TN)
SEG_GB = (SEG_GA[0] + SEG_GA[1], D_MODEL // IN_TN)


def _inproj_kernel(x_ref, g_ref, w_ref, cos_ref, sin_ref,
                   u_ref, q_ref, kf_ref, kb_ref, vf_ref, vb_ref, ga_ref, gb_ref, h_scr):
    j = pl.program_id(1)

    @pl.when(j == 0)
    def _():
        h_scr[...] = _rms(x_ref[...], g_ref[...]).astype(BF16)

    def z():
        return _dot(h_scr[...], w_ref[...])

    def rope(t):
        n = t.shape[-1]
        lane = lax.broadcasted_iota(jnp.int32, t.shape, 1)
        first = jnp.bitwise_and(lane, HEAD_DIM - 1) < (HEAD_DIM // 2)
        partner = jnp.where(first, pltpu.roll(t, n - HEAD_DIM // 2, 1), pltpu.roll(t, HEAD_DIM // 2, 1))
        return t * cos_ref[...] + partner * sin_ref[...]

    def in_seg(seg):
        return jnp.logical_and(j >= seg[0], j < seg[0] + seg[1])

    @pl.when(in_seg(SEG_U))
    def _():
        u_ref[...] = z().astype(BF16)

    @pl.when(in_seg(SEG_Q))
    def _():
        q_ref[...] = (rope(z()) * (HEAD_DIM ** -0.5)).astype(BF16)

    @pl.when(in_seg(SEG_K))
    def _():
        k = rope(z())
        kf_ref[...] = k
        kb_ref[...] = k.astype(BF16)

    @pl.when(in_seg(SEG_V))
    def _():
        v = z()
        vf_ref[...] = v
        vb_ref[...] = v.astype(BF16)

    @pl.when(in_seg(SEG_GA))
    def _():
        ga_ref[...] = _sigmoid(z()).astype(BF16)

    @pl.when(in_seg(SEG_GB))
    def _():
        gb_ref[...] = _sigmoid(z()).astype(BF16)


def _inproj(x2d, g, w_bf, cos_t, sin_t, tm):
    T = x2d.shape[0]
    nj = IN_COLS // IN_TN
    n_tab = cos_t.shape[0] // tm

    def seg_map(seg):
        return lambda i, j: (i, jnp.clip(j - seg[0], 0, seg[1] - 1))

    def out(seg, dtype):
        return (jax.ShapeDtypeStruct((T, seg[1] * IN_TN), dtype), pl.BlockSpec((tm, IN_TN), seg_map(seg)))

    outs = [out(SEG_U, BF16), out(SEG_Q, BF16), out(SEG_K, F32), out(SEG_K, BF16),
            out(SEG_V, F32), out(SEG_V, BF16), out(SEG_GA, BF16), out(SEG_GB, BF16)]
    return pl.pallas_call(
        _inproj_kernel,
        grid=(T // tm, nj),
        in_specs=[
            pl.BlockSpec((tm, D_MODEL), lambda i, j: (i, 0)),
            pl.BlockSpec((1, D_MODEL), lambda i, j: (0, 0)),
            pl.BlockSpec((D_MODEL, IN_TN), lambda i, j: (0, j)),
            pl.BlockSpec((tm, IN_TN), lambda i, j: (i % n_tab, 0)),
            pl.BlockSpec((tm, IN_TN), lambda i, j: (i % n_tab, 0)),
        ],
        out_specs=[o[1] for o in outs],
        out_shape=[o[0] for o in outs],
        scratch_shapes=[pltpu.VMEM((tm, D_MODEL), BF16)],
        compiler_params=_cparams("parallel", "arbitrary"),
        name="inproj",
    )(x2d, g, w_bf, cos_t, sin_t)


def _rope_tables(seq, past, rows):
    half = HEAD_DIM // 2
    inv = ROPE_THETA ** (-jnp.arange(half, dtype=F32) * 2.0 / HEAD_DIM)
    pos = (past + (jnp.arange(rows, dtype=jnp.int32) % seq)).astype(F32)
    ang = pos[:, None] * inv[None, :]
    cos, sin = jnp.cos(ang), jnp.sin(ang)
    reps = IN_TN // HEAD_DIM
    cos_t = jnp.tile(jnp.concatenate([cos, cos], axis=-1), (1, reps))
    sin_t = jnp.tile(jnp.concatenate([-sin, sin], axis=-1), (1, reps))
    return cos_t, sin_t


S5_COLS = 512
S5_UBLK = LANES
S5_N_UBLK = D_SSM // S5_UBLK
S5_XBLK = (S5_UBLK // SSM_GROUP) * SSM_STATE
S5_YBLK = 256
S5_N_YBLK = D_SSM // S5_YBLK
S5_HBLK = (S5_YBLK // SSM_GROUP) * SSM_STATE


def _s5_kernel(u_ref, h0_ref, perm_ref, permt_ref, wx_ref, apow_ref, cr_ref, ci_ref, d_ref,
               y_ref, hfin_ref, x_scr, c_scr, carry_scr, *, seg_len):
    t = pl.program_id(1)
    rows = SUBLANES * seg_len
    n = N_STATE

    @pl.when(t == 0)
    def _():
        carry_scr[...] = h0_ref[...]

    up = _dot(perm_ref[...], u_ref[...])
    upb = up.astype(BF16)
    for a in range(S5_N_UBLK):
        xa = _dot(upb[:, a * S5_UBLK:(a + 1) * S5_UBLK], wx_ref[a])
        x_scr[:, a * S5_XBLK:(a + 1) * S5_XBLK] = xa[:, :S5_XBLK]
        x_scr[:, n + a * S5_XBLK:n + (a + 1) * S5_XBLK] = xa[:, S5_XBLK:]

    for c in range(n // S5_COLS):
        lo = c * S5_COLS
        ar = jnp.broadcast_to(apow_ref[0:1, lo:lo + S5_COLS], (SUBLANES, S5_COLS))
        ai = jnp.broadcast_to(apow_ref[0:1, n + lo:n + lo + S5_COLS], (SUBLANES, S5_COLS))

        def scan_step(k, carry, lo=lo, ar=ar, ai=ai):
            hr, hi = carry
            r0 = pl.multiple_of(k * SUBLANES, SUBLANES)
            xr = x_scr[pl.ds(r0, SUBLANES), lo:lo + S5_COLS]
            xi = x_scr[pl.ds(r0, SUBLANES), n + lo:n + lo + S5_COLS]
            nr = ar * hr - ai * hi + xr
            ni = ar * hi + ai * hr + xi
            x_scr[pl.ds(r0, SUBLANES), lo:lo + S5_COLS] = nr
            x_scr[pl.ds(r0, SUBLANES), n + lo:n + lo + S5_COLS] = ni
            return nr, ni

        zero = jnp.zeros((SUBLANES, S5_COLS), F32)
        lax.fori_loop(0, seg_len, scan_step, (zero, zero))

    alr = apow_ref[seg_len - 1:seg_len, :n]
    ali = apow_ref[seg_len - 1:seg_len, n:]
    cr = carry_scr[:, :n]
    ci = carry_scr[:, n:]
    for s in range(SUBLANES):
        c_scr[s:s + 1, :n] = cr
        c_scr[s:s + 1, n:] = ci
        lr = x_scr[rows - SUBLANES + s:rows - SUBLANES + s + 1, :n]
        li = x_scr[rows - SUBLANES + s:rows - SUBLANES + s + 1, n:]
        cr, ci = alr * cr - ali * ci + lr, alr * ci + ali * cr + li
    carry_scr[:, :n] = cr
    carry_scr[:, n:] = ci

    for c in range(n // S5_COLS):
        lo = c * S5_COLS
        sr = c_scr[:, lo:lo + S5_COLS]
        si = c_scr[:, n + lo:n + lo + S5_COLS]

        def fix_step(k, _, lo=lo, sr=sr, si=si):
            r0 = pl.multiple_of(k * SUBLANES, SUBLANES)
            pr = apow_ref[pl.ds(k, 1), lo:lo + S5_COLS]
            pi = apow_ref[pl.ds(k, 1), n + lo:n + lo + S5_COLS]
            x_scr[pl.ds(r0, SUBLANES), lo:lo + S5_COLS] += pr * sr - pi * si
            x_scr[pl.ds(r0, SUBLANES), n + lo:n + lo + S5_COLS] += pr * si + pi * sr
            return 0

        lax.fori_loop(0, seg_len, fix_step, 0)

    @pl.when(t == pl.num_programs(1) - 1)
    def _():
        hfin_ref[...] = carry_scr[...]

    for j in range(S5_N_YBLK):
        hr = x_scr[:, j * S5_HBLK:(j + 1) * S5_HBLK].astype(BF16)
        hi = x_scr[:, n + j * S5_HBLK:n + (j + 1) * S5_HBLK].astype(BF16)
        yj = (_dot(hr, cr_ref[j]) + _dot(hi, ci_ref[j])
              + d_ref[:, j * S5_YBLK:(j + 1) * S5_YBLK] * up[:, j * S5_YBLK:(j + 1) * S5_YBLK])
        yj = _gelu(yj).astype(BF16)
        y_ref[:, j * S5_YBLK:(j + 1) * S5_YBLK] = _dot(permt_ref[...], yj).astype(BF16)


def _s5_tables(lp, seg_len):
    g, p, c = N_SSM_GROUPS, SSM_STATE, SSM_GROUP
    dt = jnp.exp(lp['log_step'].astype(F32))[:, None]
    lr, li = lp['lam_re'].astype(F32), lp['lam_im'].astype(F32)
    mag = jnp.exp(lr * dt)
    ar, ai = mag * jnp.cos(li * dt), mag * jnp.sin(li * dt)
    den = lr * lr + li * li
    fr = ((ar - 1.0) * lr + ai * li) / den
    fi = (ai * lr - (ar - 1.0) * li) / den
    br, bi = lp['b_re'].astype(F32), lp['b_im'].astype(F32)
    bbr = fr[..., None] * br - fi[..., None] * bi
    bbi = fr[..., None] * bi + fi[..., None] * br

    gpb = S5_UBLK // c
    eye = jnp.eye(gpb, dtype=F32)

    def in_blocks(bb):
        bb = bb.reshape(S5_N_UBLK, gpb, p, c)
        return jnp.einsum('xy,axpc->axcyp', eye, bb).reshape(S5_N_UBLK, S5_UBLK, S5_XBLK)

    wx = jnp.concatenate([in_blocks(bbr), in_blocks(bbi)], axis=-1).astype(BF16)

    pr, pi = ar.reshape(1, g * p), ai.reshape(1, g * p)
    while pr.shape[0] < seg_len:
        tr, ti = pr[-1:], pi[-1:]
        pr, pi = (jnp.concatenate([pr, pr * tr - pi * ti], axis=0),
                  jnp.concatenate([pi, pr * ti + pi * tr], axis=0))
    apow = jnp.concatenate([pr[:seg_len], pi[:seg_len]], axis=-1)

    gpy = S5_YBLK // c
    eye_y = jnp.eye(gpy, dtype=F32)

    def out_blocks(cm):
        cm = cm.astype(F32).reshape(S5_N_YBLK, gpy, c, p)
        return jnp.einsum('xy,jxcp->jxpyc', eye_y, cm).reshape(S5_N_YBLK, S5_HBLK, S5_YBLK)

    cr = out_blocks(lp['c_re']).astype(BF16)
    ci = (-out_blocks(lp['c_im'])).astype(BF16)
    d = lp['d'].astype(F32).reshape(1, D_SSM)
    return wx, apow, cr, ci, d


def _s5(u, h0, tables, seg_len):
    bn, seq, _ = u.shape
    rows = SUBLANES * seg_len
    wx, apow, cr, ci, d = tables
    r = jnp.arange(rows)
    perm = (r[None, :] == ((r % SUBLANES) * seg_len + r // SUBLANES)[:, None]).astype(BF16)
    const2 = lambda b, t: (0, 0)
    const3 = lambda b, t: (0, 0, 0)
    return pl.pallas_call(
        functools.partial(_s5_kernel, seg_len=seg_len),
        grid=(bn, seq // rows),
        in_specs=[
            pl.BlockSpec((None, rows, D_SSM), lambda b, t: (b, t, 0)),
            pl.BlockSpec((None, 1, 2 * N_STATE), lambda b, t: (b, 0, 0)),
            pl.BlockSpec((rows, rows), const2),
            pl.BlockSpec((rows, rows), const2),
            pl.BlockSpec(wx.shape, const3),
            pl.BlockSpec(apow.shape, const2),
            pl.BlockSpec(cr.shape, const3),
            pl.BlockSpec(ci.shape, const3),
            pl.BlockSpec((1, D_SSM), const2),
        ],
        out_specs=[
            pl.BlockSpec((None, rows, D_SSM), lambda b, t: (b, t, 0)),
            pl.BlockSpec((None, 1, 2 * N_STATE), lambda b, t: (b, 0, 0)),
        ],
        out_shape=[jax.ShapeDtypeStruct((bn, seq, D_SSM), BF16),
                   jax.ShapeDtypeStruct((bn, 1, 2 * N_STATE), F32)],
        scratch_shapes=[pltpu.VMEM((rows, 2 * N_STATE), F32),
                        pltpu.VMEM((SUBLANES, 2 * N_STATE), F32),
                        pltpu.VMEM((1, 2 * N_STATE), F32)],
        compiler_params=_cparams("parallel", "arbitrary"),
        name="s5",
    )(u, h0, perm, perm.T, wx, apow, cr, ci, d)


GLU_TN = 512


def _glu_kernel(y_ref, wa_ref, wb_ref, ga_ref, o_ref):
    y = y_ref[...]
    a = _dot(y, wa_ref[...])
    b = _dot(y, wb_ref[...])
    o_ref[...] = (ga_ref[...].astype(F32) * a * _sigmoid(b)).astype(BF16)


def _glu(y, w_glu_bf, sig_a, tm):
    T = y.shape[0]
    nj = D_MODEL // GLU_TN
    return pl.pallas_call(
        _glu_kernel,
        grid=(T // tm, nj),
        in_specs=[
            pl.BlockSpec((tm, D_SSM), lambda i, j: (i, 0)),
            pl.BlockSpec((D_SSM, GLU_TN), lambda i, j: (0, j)),
            pl.BlockSpec((D_SSM, GLU_TN), lambda i, j: (0, j + nj)),
            pl.BlockSpec((tm, GLU_TN), lambda i, j: (i, j)),
        ],
        out_specs=pl.BlockSpec((tm, GLU_TN), lambda i, j: (i, j)),
        out_shape=jax.ShapeDtypeStruct((T, D_MODEL), BF16),
        compiler_params=_cparams("parallel", "arbitrary"),
        name="glu",
    )(y, w_glu_bf, w_glu_bf, sig_a)


def _attn_kernel(q_ref, k_ref, v_ref, lq1_ref, lk1_ref, lq2_ref, lk2_ref, g_ref, o_ref,
                 qt_scr, m_scr, l_scr, acc_scr, *, tq, tk, past, lambda_init):
    i = pl.program_id(2)
    nk = k_ref.shape[0] // tk
    q = q_ref[...].astype(F32)
    lane = lax.broadcasted_iota(jnp.int32, q.shape, 1)
    qs = jnp.concatenate([jnp.where(lane < HEAD_DIM, q, 0.0), jnp.where(lane >= HEAD_DIM, q, 0.0)], axis=0)
    qt_scr[...] = qs.T.astype(BF16)

    m_scr[...] = jnp.full(m_scr.shape, NEG_INF, F32)
    l_scr[...] = jnp.zeros(l_scr.shape, F32)
    acc_scr[...] = jnp.zeros(acc_scr.shape, F32)

    q_lo = past + i * tq
    min_qc = q_lo // CHUNK
    max_qc = (q_lo + tq - 1) // CHUNK
    n_proc = jnp.minimum(nk, (max_qc * CHUNK + CHUNK - 1) // tk + 1)
    n_full = jnp.clip((min_qc * CHUNK + CHUNK) // tk, 0, n_proc)

    def block(j, masked):
        k0 = pl.multiple_of(j * tk, tk)
        kb = k_ref[pl.ds(k0, tk), :]
        vb = v_ref[pl.ds(k0, tk), :]
        s = _dot(kb, qt_scr[...])
        if masked:
            kpos = k0 + lax.broadcasted_iota(jnp.int32, s.shape, 0)
            qpos = q_lo + jnp.bitwise_and(lax.broadcasted_iota(jnp.int32, s.shape, 1), tq - 1)
            s = jnp.where(jnp.right_shift(kpos, CHUNK_SHIFT) <= jnp.right_shift(qpos, CHUNK_SHIFT), s, NEG_INF)
        m_old = m_scr[...]
        m_new = jnp.maximum(m_old, jnp.max(s, axis=0, keepdims=True))
        alpha = jnp.exp(m_old - m_new)
        p = jnp.exp(s - m_new)
        l_scr[...] = alpha * l_scr[...] + jnp.sum(p, axis=0, keepdims=True)
        pv = lax.dot_general(vb, p.astype(BF16), (((0,), (0,)), ((), ())), preferred_element_type=F32)
        acc_scr[...] = alpha * acc_scr[...] + pv
        m_scr[...] = m_new

    def full_body(j, c):
        block(j, False)
        return c

    def masked_body(j, c):
        block(j, True)
        return c

    lax.fori_loop(0, n_full, full_body, 0)
    lax.fori_loop(n_full, n_proc, masked_body, 0)

    lam = (jnp.exp(jnp.sum(lq1_ref[...] * lk1_ref[...], axis=-1, keepdims=True))
           - jnp.exp(jnp.sum(lq2_ref[...] * lk2_ref[...], axis=-1, keepdims=True)) + lambda_init)
    ot = acc_scr[:, :tq] / l_scr[:, :tq] - lam * (acc_scr[:, tq:] / l_scr[:, tq:])
    o_ref[...] = (_rms(ot.T, g_ref[...]) * (1.0 - lambda_init)).astype(BF16)


def _attention(q, k, v, lp, tq, tk, past, lambda_init):
    bn, seq, _ = q.shape
    lk = k.shape[1]
    vec = lambda a: a.astype(F32).reshape(1, -1)
    small = lambda n: pl.BlockSpec((1, n), lambda b, h, i: (0, 0))
    return pl.pallas_call(
        functools.partial(_attn_kernel, tq=tq, tk=tk, past=past, lambda_init=lambda_init),
        grid=(bn, N_HEADS, seq // tq),
        in_specs=[
            pl.BlockSpec((None, tq, V_DIM), lambda b, h, i: (b, i, h)),
            pl.BlockSpec((None, lk, V_DIM), lambda b, h, i: (b, 0, h)),
            pl.BlockSpec((None, lk, V_DIM), lambda b, h, i: (b, 0, h)),
            small(HEAD_DIM), small(HEAD_DIM), small(HEAD_DIM), small(HEAD_DIM), small(V_DIM),
        ],
        out_specs=pl.BlockSpec((None, tq, V_DIM), lambda b, h, i: (b, i, h)),
        out_shape=jax.ShapeDtypeStruct((bn, seq, V_COLS), BF16),
        scratch_shapes=[pltpu.VMEM((V_DIM, 2 * tq), BF16), pltpu.VMEM((1, 2 * tq), F32),
                        pltpu.VMEM((1, 2 * tq), F32), pltpu.VMEM((V_DIM, 2 * tq), F32)],
        compiler_params=_cparams("parallel", "parallel", "arbitrary"),
        name="diff_attn",
    )(q, k, v, vec(lp['lq1']), vec(lp['lk1']), vec(lp['lq2']), vec(lp['lk2']), vec(lp['g_subln']))


def _merge_kernel(x_ref, o_ref, ga_ref, sb_ref, wa_ref, wo_ref, g_ref, x1_ref, h2t_ref):
    branch_b = _dot(o_ref[...], wa_ref[...])
    merged = ga_ref[...].astype(F32) + sb_ref[...].astype(F32) * branch_b
    x1 = x_ref[...] + _dot(merged.astype(BF16), wo_ref[...])
    x1_ref[...] = x1
    h2t_ref[...] = _rms(x1, g_ref[...]).T.astype(BF16)


def _merge(x2d, o, gated_a, sig_b, wa_bf, wo_bf, g_ffn, tm):
    T = x2d.shape[0]
    row = lambda n: pl.BlockSpec((tm, n), lambda i: (i, 0))
    const = lambda s: pl.BlockSpec(s, lambda i: (0, 0))
    return pl.pallas_call(
        _merge_kernel,
        grid=(T // tm,),
        in_specs=[row(D_MODEL), row(V_COLS), row(D_MODEL), row(D_MODEL),
                  const((V_COLS, D_MODEL)), const((D_MODEL, D_MODEL)), const((1, D_MODEL))],
        out_specs=[row(D_MODEL), pl.BlockSpec((D_MODEL, tm), lambda i: (0, i))],
        out_shape=[jax.ShapeDtypeStruct((T, D_MODEL), F32), jax.ShapeDtypeStruct((D_MODEL, T), BF16)],
        compiler_params=_cparams("parallel"),
        name="merge_out",
    )(x2d, o, gated_a, sig_b, wa_bf, wo_bf, g_ffn)


ROUTE_LC = 256
TAKEN = -3.0e38


def _split_bf16(x):
    hi = x.astype(BF16)
    return hi, (x - hi.astype(F32)).astype(BF16)


def _top_rows(s, k):
    out = []
    for _ in range(k):
        m = jnp.max(s, axis=0, keepdims=True)
        out.append(m)
        s = jnp.where(s == m, TAKEN, s)
    return out


def _stack_rows(rows):
    n, w = len(rows), rows[0].shape[1]
    idx = lax.broadcasted_iota(jnp.int32, (n, w), 0)
    out = jnp.zeros((n, w), F32)
    for r, v in enumerate(rows):
        out = jnp.where(idx == r, v, out)
    return out


def _route_kernel(h2t_ref, wq_ref, keys_ref, thr_ref, s2_ref, e1_ref, e2_ref, s1_scr):
    tt = h2t_ref.shape[1]
    n_top = PEER_TOPK + 1
    qt = _dot(wq_ref[...], h2t_ref[...])
    halves = []
    for c in range(2):
        q_hi, q_lo = _split_bf16(qt[c * PEER_HALF:(c + 1) * PEER_HALF, :])
        k_hi, k_lo = _split_bf16(keys_ref[0, c])
        halves.append(_dot(k_hi, q_hi) + _dot(k_hi, q_lo) + _dot(k_lo, q_hi))
    s1_scr[...] = halves[0]
    s2_ref[0] = halves[1]

    width = min(tt, ROUTE_LC)
    for lc in range(tt // width):
        sl = slice(lc * width, (lc + 1) * width)
        s1 = s1_scr[:, sl]
        s2 = s2_ref[0, :, sl]
        a = _top_rows(s1, n_top)
        b = _top_rows(s2, n_top)
        taken = jnp.full_like(b[0], TAKEN)
        b_all = _stack_rows(b + [taken] * (3 * SUBLANES - n_top))
        cand = jnp.concatenate([a[0] + b_all] + [a[i] + b_all[:SUBLANES] for i in range(1, n_top)], axis=0)
        top = _top_rows(cand, n_top)
        z = jnp.zeros_like(top[0])
        for r in range(PEER_TOPK):
            z = z + jnp.exp(top[r] - top[0])
        tau = 0.5 * (top[PEER_TOPK - 1] + top[PEER_TOPK])
        grouped = (N_KEYS // SUBLANES, SUBLANES, width)
        thr_ref[0, :, :, sl] = (tau - s1).reshape(grouped)
        e1_ref[0, :, :, sl] = (jnp.exp(s1 - a[0]) / z).reshape(grouped)
        e2_ref[0, :, sl] = jnp.exp(s2 - b[0])


def _route(h2t, wq_t_bf, keys, tt):
    T = h2t.shape[1]
    arr = jax.ShapeDtypeStruct((PEER_HEADS, N_KEYS, T), F32)
    spec = pl.BlockSpec((1, N_KEYS, tt), lambda i, h: (h, 0, i))
    n_grp = N_KEYS // SUBLANES
    garr = jax.ShapeDtypeStruct((PEER_HEADS, n_grp, SUBLANES, T), F32)
    gspec = pl.BlockSpec((1, n_grp, SUBLANES, tt), lambda i, h: (h, 0, 0, i))
    return pl.pallas_call(
        _route_kernel,
        grid=(T // tt, PEER_HEADS),
        in_specs=[
            pl.BlockSpec((D_MODEL, tt), lambda i, h: (0, i)),
            pl.BlockSpec((PEER_QUERY, D_MODEL), lambda i, h: (h, 0)),
            pl.BlockSpec((1, 2, N_KEYS, PEER_HALF), lambda i, h: (h, 0, 0, 0)),
        ],
        out_specs=[gspec, spec, gspec, spec],
        out_shape=[garr, arr, garr, arr],
        scratch_shapes=[pltpu.VMEM((N_KEYS, tt), F32)],
        compiler_params=_cparams("parallel", "arbitrary"),
        name="peer_route",
    )(h2t, wq_t_bf, keys)


PEER_SUB = 512
PEER_EB = 2 * PEER_SUB
PEER_ROWS = PEER_SUB // N_KEYS
N_SUB = N_EXPERTS // PEER_SUB
PEER_JH = 32


def _peer_kernel(h2t_ref, thr_ref, s2_ref, e1_ref, e2_ref, u_ref, vt_ref, o_ref,
                 act0, act1, coef0, coef1, acc_scr):
    g = pl.program_id(1)
    tt = h2t_ref.shape[1]

    @pl.when(g == 0)
    def _():
        act1[...] = jnp.zeros(act1.shape, F32)
        coef0[...] = jnp.zeros(coef0.shape, F32)
        coef1[...] = jnp.zeros(coef1.shape, F32)
        acc_scr[...] = jnp.zeros(acc_scr.shape, F32)

    def stage_a(half, act):
        act[...] = _dot(u_ref[half * PEER_SUB:(half + 1) * PEER_SUB, :], h2t_ref[...])

    def stage_b(b, act, coef):
        b = jnp.clip(b, 0, N_SUB - 1)
        grp = b // 2
        odd = (b % 2) == 1
        chain = jnp.zeros((PEER_ROWS, LANES), F32)
        for lc in range(tt // LANES):
            ln = slice(lc * LANES, (lc + 1) * LANES)
            for jh in range(N_KEYS // PEER_JH):
                js = slice(jh * PEER_JH, (jh + 1) * PEER_JH)
                gates = [jnp.zeros((PEER_JH, LANES), F32) for _ in range(PEER_ROWS)]
                for h in range(PEER_HEADS):
                    s2 = s2_ref[h, js, ln]
                    e2 = e2_ref[h, js, ln]
                    thr_grp = thr_ref[h, grp, :, ln]
                    e1_grp = e1_ref[h, grp, :, ln]
                    thr_rows = jnp.where(odd, thr_grp[PEER_ROWS:], thr_grp[:PEER_ROWS]) + chain
                    e1_rows = jnp.where(odd, e1_grp[PEER_ROWS:], e1_grp[:PEER_ROWS]) + chain
                    for r in range(PEER_ROWS):
                        gates[r] = gates[r] + jnp.where(s2 >= thr_rows[r:r + 1], e1_rows[r:r + 1] * e2, 0.0)
                chain = jnp.minimum(gates[0][:PEER_ROWS], 0.0)
                for r in range(PEER_ROWS):
                    rows = slice(r * N_KEYS + jh * PEER_JH, r * N_KEYS + (jh + 1) * PEER_JH)
                    coef[rows, ln] = gates[r] * _gelu(act[rows, ln])

    def stage_c(half, coef):
        acc_scr[...] += _dot(vt_ref[:, half * PEER_SUB:(half + 1) * PEER_SUB], coef[...].astype(BF16))

    stage_a(0, act0)
    stage_b(2 * g - 1, act1, coef1)
    stage_c(0, coef0)
    stage_a(1, act1)
    stage_b(2 * g, act0, coef0)
    stage_c(1, coef1)

    @pl.when(g == pl.num_programs(1) - 1)
    def _():
        o_ref[...] = acc_scr[...].T.astype(BF16)


def _peer(h2t, route, u_bf, vt_bf, tt):
    T = h2t.shape[1]
    thr, s2, e1, e2 = route
    n_eb = N_EXPERTS // PEER_EB
    spec = pl.BlockSpec((PEER_HEADS, N_KEYS, tt), lambda i, g: (0, 0, i))
    gspec = pl.BlockSpec((PEER_HEADS, N_KEYS // SUBLANES, SUBLANES, tt), lambda i, g: (0, 0, 0, i))
    assert SUBLANES == 2 * PEER_ROWS
    return pl.pallas_call(
        _peer_kernel,
        grid=(T // tt, n_eb + 1),
        in_specs=[
            pl.BlockSpec((D_MODEL, tt), lambda i, g: (0, i)),
            gspec, spec, gspec, spec,
            pl.BlockSpec((PEER_EB, D_MODEL), lambda i, g: (jnp.minimum(g, n_eb - 1), 0)),
            pl.BlockSpec((D_MODEL, PEER_EB), lambda i, g: (0, jnp.maximum(g - 1, 0))),
        ],
        out_specs=pl.BlockSpec((tt, D_MODEL), lambda i, g: (i, 0)),
        out_shape=jax.ShapeDtypeStruct((T, D_MODEL), BF16),
        scratch_shapes=[pltpu.VMEM((PEER_SUB, tt), F32), pltpu.VMEM((PEER_SUB, tt), F32),
                        pltpu.VMEM((PEER_SUB, tt), F32), pltpu.VMEM((PEER_SUB, tt), F32),
                        pltpu.VMEM((D_MODEL, tt), F32)],
        compiler_params=_cparams("parallel", "arbitrary"),
        name="peer_dense",
    )(h2t, thr, s2, e1, e2, u_bf, vt_bf)


def _ple_kernel(x1_ref, po_ref, p_ref, wg_ref, wp_ref, gp_ref, gf_ref, y_ref):
    x2 = x1_ref[...] + po_ref[...].astype(F32)
    h3 = _rms(x2, gp_ref[...]).astype(BF16)
    gate = _sigmoid(_dot(h3, wg_ref[...]))
    proj = _dot(p_ref[...].astype(BF16), wp_ref[...])
    y_ref[...] = _rms(x2 + proj * gate, gf_ref[...])


def _ple(x1, peer_out, p2d, wg_bf, wp_bf, g_ple, g_final, tm):
    T = x1.shape[0]
    row = lambda n: pl.BlockSpec((tm, n), lambda i: (i, 0))
    const = lambda s: pl.BlockSpec(s, lambda i: (0, 0))
    return pl.pallas_call(
        _ple_kernel,
        grid=(T // tm,),
        in_specs=[row(D_MODEL), row(D_MODEL), row(PLE_DIM),
                  const((D_MODEL, D_MODEL)), const((PLE_DIM, D_MODEL)), const((1, D_MODEL)), const((1, D_MODEL))],
        out_specs=row(D_MODEL),
        out_shape=jax.ShapeDtypeStruct((T, D_MODEL), F32),
        compiler_params=_cparams("parallel"),
        name="ple_final",
    )(x1, peer_out, p2d, wg_bf, wp_bf, g_ple, g_final)


def _tile(n, pref):
    t = min(n, pref)
    assert n % t == 0, (n, t)
    return t


def _layer(x, p, h0_re, h0_im, k_past, v_past, lp, lambda_init):
    bn, seq, _ = x.shape
    past = k_past.shape[1]
    T = bn * seq
    x2d = x.reshape(T, D_MODEL)
    row = lambda a: a.astype(F32).reshape(1, -1)

    tm_in = _tile(T, 1024)
    cos_t, sin_t = _rope_tables(seq, past, max(seq, tm_in))
    u, q, k_f32, k_bf, v_f32, v_bf, sig_a, sig_b = _inproj(
        x2d, row(lp['g_mix']), lp['w_in'].astype(BF16), cos_t, sin_t, tm_in)

    seg_len = _tile(seq // SUBLANES, 32)
    h0 = jnp.concatenate([h0_re.reshape(bn, 1, N_STATE), h0_im.reshape(bn, 1, N_STATE)], axis=-1).astype(F32)
    y_ssm, h_fin = _s5(u.reshape(bn, seq, D_SSM), h0, _s5_tables(lp, seg_len), seg_len)
    gated_a = _glu(y_ssm.reshape(T, D_SSM), lp['w_glu'].astype(BF16), sig_a, _tile(T, 1024))

    k_all = jnp.concatenate([k_past.reshape(bn, past, Q_COLS).astype(BF16), k_bf.reshape(bn, seq, Q_COLS)], axis=1)
    v_all = jnp.concatenate([v_past.reshape(bn, past, V_COLS).astype(BF16), v_bf.reshape(bn, seq, V_COLS)], axis=1)
    seq_q = max(seq, LANES // 2)
    q3 = jnp.pad(q.reshape(bn, seq, Q_COLS), ((0, 0), (0, seq_q - seq), (0, 0)))
    tq = _tile(seq_q, 512)
    tk = tq if past == 0 else past + seq
    o = _attention(q3, k_all, v_all, lp, tq, tk, past, lambda_init)[:, :seq]

    x1, h2t = _merge(x2d, o.reshape(T, V_COLS), gated_a, sig_b, lp['w_attn_out'].astype(BF16),
                    lp['w_out'].astype(BF16), row(lp['g_ffn']), _tile(T, 256))

    tt = _tile(T, 512)
    route = _route(h2t, lp['peer_w_q'].T.astype(BF16), lp['peer_keys'].astype(F32), tt)
    peer_out = _peer(h2t, route, lp['peer_u'].astype(BF16), lp['peer_v'].T.astype(BF16), tt)

    y = _ple(x1, peer_out, p.reshape(T, PLE_DIM), lp['w_ple_gate'].astype(BF16), lp['w_ple_proj'].astype(BF16),
             row(lp['g_ple']), row(lp['g_final']), _tile(T, 256))

    new_k = k_f32.reshape(bn, seq, N_HEADS, 2 * HEAD_DIM)
    new_v = v_f32.reshape(bn, seq, N_HEADS, V_DIM)
    hr = h_fin[:, 0, :N_STATE].reshape(bn, N_SSM_GROUPS, SSM_STATE)
    hi = h_fin[:, 0, N_STATE:].reshape(bn, N_SSM_GROUPS, SSM_STATE)
    return y.reshape(bn, seq, D_MODEL), new_k, new_v, hr, hi


def _trunk(x, p, h0_re, h0_im, k_past, v_past, lp):
    assert DEPTH == 1
    lambda_init = 0.8 - 0.6 * math.exp(-0.3 * 0)
    y, k_new, v_new, hr, hi = _layer(x, p[0], h0_re[0], h0_im[0], k_past[0], v_past[0], lp, lambda_init)
    return y, k_new[None], v_new[None], hr[None], hi[None]


def kernel(x_prompt, x_sample, p_prompt, p_sample, cache_k, cache_v, state_ssm_re, state_ssm_im,
           g_mix_norm, w_in, ssm_lambda_re, ssm_lambda_im, ssm_log_step, ssm_b_re, ssm_b_im,
           ssm_c_re, ssm_c_im, ssm_d, w_glu, diff_lambda_q1, diff_lambda_k1, diff_lambda_q2,
           diff_lambda_k2, g_subln, w_attn_out, w_out, g_ffn_norm, peer_w_q, peer_keys, peer_u,
           peer_v, g_ple_norm, w_ple_gate, w_ple_proj, g_final):
    lp = dict(g_mix=g_mix_norm[0], w_in=w_in[0], lam_re=ssm_lambda_re[0], lam_im=ssm_lambda_im[0],
              log_step=ssm_log_step[0], b_re=ssm_b_re[0], b_im=ssm_b_im[0], c_re=ssm_c_re[0],
              c_im=ssm_c_im[0], d=ssm_d[0], w_glu=w_glu[0], lq1=diff_lambda_q1[0],
              lk1=diff_lambda_k1[0], lq2=diff_lambda_q2[0], lk2=diff_lambda_k2[0],
              g_subln=g_subln[0], w_attn_out=w_attn_out[0], w_out=w_out[0], g_ffn=g_ffn_norm[0],
              peer_w_q=peer_w_q[0], peer_keys=peer_keys[0], peer_u=peer_u[0], peer_v=peer_v[0],
              g_ple=g_ple_norm[0], w_ple_gate=w_ple_gate[0], w_ple_proj=w_ple_proj[0], g_final=g_final)
    bn = x_prompt.shape[0]
    zeros_state = jnp.zeros((DEPTH, bn, N_SSM_GROUPS, SSM_STATE), F32)
    k_none = jnp.zeros((DEPTH, bn, 0, N_HEADS, 2 * HEAD_DIM), x_prompt.dtype)
    v_none = jnp.zeros((DEPTH, bn, 0, N_HEADS, V_DIM), x_prompt.dtype)
    y_p, k_p, v_p, r_p, i_p = _trunk(x_prompt, p_prompt, zeros_state, zeros_state, k_none, v_none, lp)
    y_s, k_s, v_s, r_s, i_s = _trunk(x_sample, p_sample, state_ssm_re, state_ssm_im, cache_k, cache_v, lp)
    return (y_p, y_s, k_p, v_p, r_p, i_p, k_s, v_s, r_s, i_s)
```

```python
import functools
import math

import jax
import jax.numpy as jnp
from jax import lax
from jax.experimental import pallas as pl
from jax.experimental.pallas import tpu as pltpu

F32 = jnp.float32
BF16 = jnp.bfloat16

D_MODEL = 2048
DEPTH = 1
CHUNK = 64
CHUNK_SHIFT = CHUNK.bit_length() - 1
assert 1 << CHUNK_SHIFT == CHUNK
PLE_DIM = 256
RMS_EPS = 1e-6
NEG_INF = -1e30
D_SSM = 1024
SSM_GROUP = 16
N_SSM_GROUPS = D_SSM // SSM_GROUP
SSM_STATE = 64
N_STATE = N_SSM_GROUPS * SSM_STATE
N_HEADS = 8
HEAD_DIM = 64
V_DIM = 2 * HEAD_DIM
ROPE_THETA = 10000.0
Q_COLS = N_HEADS * 2 * HEAD_DIM
V_COLS = N_HEADS * V_DIM
IN_COLS = D_SSM + 2 * Q_COLS + V_COLS + 2 * D_MODEL
PEER_HEADS = 8
N_KEYS = 128
N_EXPERTS = N_KEYS * N_KEYS
PEER_QUERY = 256
PEER_HALF = PEER_QUERY // 2
PEER_TOPK = 16

LANES = 128
SUBLANES = 8
VMEM_LIMIT = 56 * 1024 * 1024


def _cparams(*sem):
    return pltpu.CompilerParams(dimension_semantics=sem, vmem_limit_bytes=VMEM_LIMIT)


def _rms(x, g):
    return x * lax.rsqrt(jnp.mean(x * x, axis=-1, keepdims=True) + RMS_EPS) * g


def _gelu(x):
    return 0.5 * x * (1.0 + lax.erf(x * (2.0 ** -0.5)))


def _sigmoid(x):
    return 1.0 / (1.0 + jnp.exp(-x))


def _dot(a, b):
    return jnp.dot(a, b, preferred_element_type=F32)


def _dot_nt(a, b):
    return lax.dot_general(a, b, (((1,), (1,)), ((), ())), preferred_element_type=F32)


IN_TN = 512
Q_SCALE = HEAD_DIM ** -0.5 * math.log2(math.e)
SEG_U = (0, D_SSM // IN_TN)
SEG_Q = (SEG_U[0] + SEG_U[1], Q_COLS // IN_TN)
SEG_K = (SEG_Q[0] + SEG_Q[1], Q_COLS // IN_TN)
SEG_V = (SEG_K[0] + SEG_K[1], V_COLS // IN_TN)
SEG_GA = (SEG_V[0] + SEG_V[1], D_MODEL // IN_TN)
SEG_GB = (SEG_GA[0] + SEG_GA[1], D_MODEL // IN_TN)


def _inproj_kernel(x_ref, g_ref, w_ref, cos_ref, sin_ref,
                   u_ref, q_ref, kf_ref, kb_ref, vf_ref, vb_ref, ga_ref, gb_ref, h_scr):
    j = pl.program_id(1)

    @pl.when(j == 0)
    def _():
        h_scr[...] = _rms(x_ref[...], g_ref[...]).astype(BF16)

    def z():
        return _dot(h_scr[...], w_ref[...])

    def rope(t):
        n = t.shape[-1]
        lane = lax.broadcasted_iota(jnp.int32, t.shape, 1)
        first = jnp.bitwise_and(lane, HEAD_DIM - 1) < (HEAD_DIM // 2)
        partner = jnp.where(first, pltpu.roll(t, n - HEAD_DIM // 2, 1), pltpu.roll(t, HEAD_DIM // 2, 1))
        return t * cos_ref[...] + partner * sin_ref[...]

    def in_seg(seg):
        return jnp.logical_and(j >= seg[0], j < seg[0] + seg[1])

    @pl.when(in_seg(SEG_U))
    def _():
        u_ref[...] = z().astype(BF16)

    @pl.when(in_seg(SEG_Q))
    def _():
        q_ref[...] = (rope(z()) * Q_SCALE).astype(BF16)

    @pl.when(in_seg(SEG_K))
    def _():
        k = rope(z())
        kf_ref[...] = k
        kb_ref[...] = k.astype(BF16)

    @pl.when(in_seg(SEG_V))
    def _():
        v = z()
        vf_ref[...] = v
        vb_ref[...] = v.astype(BF16)

    @pl.when(in_seg(SEG_GA))
    def _():
        ga_ref[...] = _sigmoid(z()).astype(BF16)

    @pl.when(in_seg(SEG_GB))
    def _():
        gb_ref[...] = _sigmoid(z()).astype(BF16)


def _inproj(x2d, g, w_bf, cos_t, sin_t, tm):
    T = x2d.shape[0]
    nj = IN_COLS // IN_TN
    n_tab = cos_t.shape[0] // tm

    def seg_map(seg):
        return lambda i, j: (i, jnp.clip(j - seg[0], 0, seg[1] - 1))

    def out(seg, dtype):
        return (jax.ShapeDtypeStruct((T, seg[1] * IN_TN), dtype), pl.BlockSpec((tm, IN_TN), seg_map(seg)))

    outs = [out(SEG_U, BF16), out(SEG_Q, BF16), out(SEG_K, F32), out(SEG_K, BF16),
            out(SEG_V, F32), out(SEG_V, BF16), out(SEG_GA, BF16), out(SEG_GB, BF16)]
    return pl.pallas_call(
        _inproj_kernel,
        grid=(T // tm, nj),
        in_specs=[
            pl.BlockSpec((tm, D_MODEL), lambda i, j: (i, 0)),
            pl.BlockSpec((1, D_MODEL), lambda i, j: (0, 0)),
            pl.BlockSpec((D_MODEL, IN_TN), lambda i, j: (0, j)),
            pl.BlockSpec((tm, IN_TN), lambda i, j: (i % n_tab, 0)),
            pl.BlockSpec((tm, IN_TN), lambda i, j: (i % n_tab, 0)),
        ],
        out_specs=[o[1] for o in outs],
        out_shape=[o[0] for o in outs],
        scratch_shapes=[pltpu.VMEM((tm, D_MODEL), BF16)],
        compiler_params=_cparams("parallel", "arbitrary"),
        name="inproj",
    )(x2d, g, w_bf, cos_t, sin_t)


def _rope_tables(seq, past, rows):
    half = HEAD_DIM // 2
    inv = ROPE_THETA ** (-jnp.arange(half, dtype=F32) * 2.0 / HEAD_DIM)
    pos = (past + (jnp.arange(rows, dtype=jnp.int32) % seq)).astype(F32)
    ang = pos[:, None] * inv[None, :]
    cos, sin = jnp.cos(ang), jnp.sin(ang)
    reps = IN_TN // HEAD_DIM
    cos_t = jnp.tile(jnp.concatenate([cos, cos], axis=-1), (1, reps))
    sin_t = jnp.tile(jnp.concatenate([-sin, sin], axis=-1), (1, reps))
    return cos_t, sin_t


S5_COLS = 512
S5_UBLK = LANES
S5_N_UBLK = D_SSM // S5_UBLK
S5_XBLK = (S5_UBLK // SSM_GROUP) * SSM_STATE
S5_YBLK = 256
S5_N_YBLK = D_SSM // S5_YBLK
S5_HBLK = (S5_YBLK // SSM_GROUP) * SSM_STATE


def _s5_kernel(u_ref, h0_ref, perm_ref, permt_ref, wx_ref, apow_ref, cr_ref, ci_ref, d_ref,
               y_ref, hfin_ref, x_scr, c_scr, carry_scr, *, seg_len):
    t = pl.program_id(1)
    rows = SUBLANES * seg_len
    n = N_STATE

    @pl.when(t == 0)
    def _():
        carry_scr[...] = h0_ref[...]

    up = _dot(perm_ref[...], u_ref[...])
    upb = up.astype(BF16)
    for a in range(S5_N_UBLK):
        xa = _dot(upb[:, a * S5_UBLK:(a + 1) * S5_UBLK], wx_ref[a])
        x_scr[:, a * S5_XBLK:(a + 1) * S5_XBLK] = xa[:, :S5_XBLK]
        x_scr[:, n + a * S5_XBLK:n + (a + 1) * S5_XBLK] = xa[:, S5_XBLK:]

    for c in range(n // S5_COLS):
        lo = c * S5_COLS
        ar = jnp.broadcast_to(apow_ref[0:1, lo:lo + S5_COLS], (SUBLANES, S5_COLS))
        ai = jnp.broadcast_to(apow_ref[0:1, n + lo:n + lo + S5_COLS], (SUBLANES, S5_COLS))

        def scan_step(k, carry, lo=lo, ar=ar, ai=ai):
            hr, hi = carry
            r0 = pl.multiple_of(k * SUBLANES, SUBLANES)
            xr = x_scr[pl.ds(r0, SUBLANES), lo:lo + S5_COLS]
            xi = x_scr[pl.ds(r0, SUBLANES), n + lo:n + lo + S5_COLS]
            nr = ar * hr - ai * hi + xr
            ni = ar * hi + ai * hr + xi
            x_scr[pl.ds(r0, SUBLANES), lo:lo + S5_COLS] = nr
            x_scr[pl.ds(r0, SUBLANES), n + lo:n + lo + S5_COLS] = ni
            return nr, ni

        zero = jnp.zeros((SUBLANES, S5_COLS), F32)
        lax.fori_loop(0, seg_len, scan_step, (zero, zero))

    alr = apow_ref[seg_len - 1:seg_len, :n]
    ali = apow_ref[seg_len - 1:seg_len, n:]
    cr = carry_scr[:, :n]
    ci = carry_scr[:, n:]
    for s in range(SUBLANES):
        c_scr[s:s + 1, :n] = cr
        c_scr[s:s + 1, n:] = ci
        lr = x_scr[rows - SUBLANES + s:rows - SUBLANES + s + 1, :n]
        li = x_scr[rows - SUBLANES + s:rows - SUBLANES + s + 1, n:]
        cr, ci = alr * cr - ali * ci + lr, alr * ci + ali * cr + li
    carry_scr[:, :n] = cr
    carry_scr[:, n:] = ci

    for c in range(n // S5_COLS):
        lo = c * S5_COLS
        sr = c_scr[:, lo:lo + S5_COLS]
        si = c_scr[:, n + lo:n + lo + S5_COLS]

        def fix_step(k, _, lo=lo, sr=sr, si=si):
            r0 = pl.multiple_of(k * SUBLANES, SUBLANES)
            pr = apow_ref[pl.ds(k, 1), lo:lo + S5_COLS]
            pi = apow_ref[pl.ds(k, 1), n + lo:n + lo + S5_COLS]
            x_scr[pl.ds(r0, SUBLANES), lo:lo + S5_COLS] += pr * sr - pi * si
            x_scr[pl.ds(r0, SUBLANES), n + lo:n + lo + S5_COLS] += pr * si + pi * sr
            return 0

        lax.fori_loop(0, seg_len, fix_step, 0)

    @pl.when(t == pl.num_programs(1) - 1)
    def _():
        hfin_ref[...] = carry_scr[...]

    for j in range(S5_N_YBLK):
        hr = x_scr[:, j * S5_HBLK:(j + 1) * S5_HBLK].astype(BF16)
        hi = x_scr[:, n + j * S5_HBLK:n + (j + 1) * S5_HBLK].astype(BF16)
        yj = (_dot(hr, cr_ref[j]) + _dot(hi, ci_ref[j])
              + d_ref[:, j * S5_YBLK:(j + 1) * S5_YBLK] * up[:, j * S5_YBLK:(j + 1) * S5_YBLK])
        yj = _gelu(yj).astype(BF16)
        y_ref[:, j * S5_YBLK:(j + 1) * S5_YBLK] = _dot(permt_ref[...], yj).astype(BF16)


def _s5_tables(lp, seg_len):
    g, p, c = N_SSM_GROUPS, SSM_STATE, SSM_GROUP
    dt = jnp.exp(lp['log_step'].astype(F32))[:, None]
    lr, li = lp['lam_re'].astype(F32), lp['lam_im'].astype(F32)
    mag = jnp.exp(lr * dt)
    ar, ai = mag * jnp.cos(li * dt), mag * jnp.sin(li * dt)
    den = lr * lr + li * li
    fr = ((ar - 1.0) * lr + ai * li) / den
    fi = (ai * lr - (ar - 1.0) * li) / den
    br, bi = lp['b_re'].astype(F32), lp['b_im'].astype(F32)
    bbr = fr[..., None] * br - fi[..., None] * bi
    bbi = fr[..., None] * bi + fi[..., None] * br

    gpb = S5_UBLK // c
    eye = jnp.eye(gpb, dtype=F32)

    def in_blocks(bb):
        bb = bb.reshape(S5_N_UBLK, gpb, p, c)
        return jnp.einsum('xy,axpc->axcyp', eye, bb).reshape(S5_N_UBLK, S5_UBLK, S5_XBLK)

    wx = jnp.concatenate([in_blocks(bbr), in_blocks(bbi)], axis=-1).astype(BF16)

    pr, pi = ar.reshape(1, g * p), ai.reshape(1, g * p)
    while pr.shape[0] < seg_len:
        tr, ti = pr[-1:], pi[-1:]
        pr, pi = (jnp.concatenate([pr, pr * tr - pi * ti], axis=0),
                  jnp.concatenate([pi, pr * ti + pi * tr], axis=0))
    apow = jnp.concatenate([pr[:seg_len], pi[:seg_len]], axis=-1)

    gpy = S5_YBLK // c
    eye_y = jnp.eye(gpy, dtype=F32)

    def out_blocks(cm):
        cm = cm.astype(F32).reshape(S5_N_YBLK, gpy, c, p)
        return jnp.einsum('xy,jxcp->jxpyc', eye_y, cm).reshape(S5_N_YBLK, S5_HBLK, S5_YBLK)

    cr = out_blocks(lp['c_re']).astype(BF16)
    ci = (-out_blocks(lp['c_im'])).astype(BF16)
    d = lp['d'].astype(F32).reshape(1, D_SSM)
    return wx, apow, cr, ci, d


def _s5(u, h0, tables, seg_len):
    bn, seq, _ = u.shape
    rows = SUBLANES * seg_len
    wx, apow, cr, ci, d = tables
    r = jnp.arange(rows)
    perm = (r[None, :] == ((r % SUBLANES) * seg_len + r // SUBLANES)[:, None]).astype(BF16)
    const2 = lambda b, t: (0, 0)
    const3 = lambda b, t: (0, 0, 0)
    return pl.pallas_call(
        functools.partial(_s5_kernel, seg_len=seg_len),
        grid=(bn, seq // rows),
        in_specs=[
            pl.BlockSpec((None, rows, D_SSM), lambda b, t: (b, t, 0)),
            pl.BlockSpec((None, 1, 2 * N_STATE), lambda b, t: (b, 0, 0)),
            pl.BlockSpec((rows, rows), const2),
            pl.BlockSpec((rows, rows), const2),
            pl.BlockSpec(wx.shape, const3),
            pl.BlockSpec(apow.shape, const2),
            pl.BlockSpec(cr.shape, const3),
            pl.BlockSpec(ci.shape, const3),
            pl.BlockSpec((1, D_SSM), const2),
        ],
        out_specs=[
            pl.BlockSpec((None, rows, D_SSM), lambda b, t: (b, t, 0)),
            pl.BlockSpec((None, 1, 2 * N_STATE), lambda b, t: (b, 0, 0)),
        ],
        out_shape=[jax.ShapeDtypeStruct((bn, seq, D_SSM), BF16),
                   jax.ShapeDtypeStruct((bn, 1, 2 * N_STATE), F32)],
        scratch_shapes=[pltpu.VMEM((rows, 2 * N_STATE), F32),
                        pltpu.VMEM((SUBLANES, 2 * N_STATE), F32),
                        pltpu.VMEM((1, 2 * N_STATE), F32)],
        compiler_params=_cparams("parallel", "arbitrary"),
        name="s5",
    )(u, h0, perm, perm.T, wx, apow, cr, ci, d)


GLU_TN = 512


def _glu_kernel(y_ref, wa_ref, wb_ref, ga_ref, o_ref):
    y = y_ref[...]
    a = _dot(y, wa_ref[...])
    b = _dot(y, wb_ref[...])
    o_ref[...] = (ga_ref[...].astype(F32) * a * _sigmoid(b)).astype(BF16)


def _glu(y, w_glu_bf, sig_a, tm):
    T = y.shape[0]
    nj = D_MODEL // GLU_TN
    return pl.pallas_call(
        _glu_kernel,
        grid=(T // tm, nj),
        in_specs=[
            pl.BlockSpec((tm, D_SSM), lambda i, j: (i, 0)),
            pl.BlockSpec((D_SSM, GLU_TN), lambda i, j: (0, j)),
            pl.BlockSpec((D_SSM, GLU_TN), lambda i, j: (0, j + nj)),
            pl.BlockSpec((tm, GLU_TN), lambda i, j: (i, j)),
        ],
        out_specs=pl.BlockSpec((tm, GLU_TN), lambda i, j: (i, j)),
        out_shape=jax.ShapeDtypeStruct((T, D_MODEL), BF16),
        compiler_params=_cparams("parallel", "arbitrary"),
        name="glu",
    )(y, w_glu_bf, w_glu_bf, sig_a)


def _attn_kernel(q_ref, k_ref, v_ref, lq1_ref, lk1_ref, lq2_ref, lk2_ref, g_ref, o_ref,
                 qt_scr, s0_scr, s1_scr, m_scr, l_scr, acc_scr, *, tq, tk, past, paired, lambda_init):
    i = pl.program_id(2)
    nk = k_ref.shape[0] // tk
    q = q_ref[...].astype(F32)
    lane = lax.broadcasted_iota(jnp.int32, q.shape, 1)
    qs = jnp.concatenate([jnp.where(lane < HEAD_DIM, q, 0.0), jnp.where(lane >= HEAD_DIM, q, 0.0)], axis=0)
    qt_scr[...] = qs.T.astype(BF16)

    m_scr[...] = jnp.full(m_scr.shape, NEG_INF, F32)
    l_scr[...] = jnp.zeros(l_scr.shape, F32)
    acc_scr[...] = jnp.zeros(acc_scr.shape, F32)

    q_lo = past + i * tq
    min_qc = q_lo // CHUNK
    max_qc = (q_lo + tq - 1) // CHUNK
    n_proc = jnp.minimum(nk, (max_qc * CHUNK + CHUNK - 1) // tk + 1)
    n_full = jnp.clip((min_qc * CHUNK + CHUNK) // tk, 0, n_proc)

    def scores(j, s_ref):
        k0 = pl.multiple_of(j * tk, tk)
        s_ref[...] = _dot(k_ref[pl.ds(k0, tk), :], qt_scr[...])

    def update(j, s_ref, masked):
        k0 = pl.multiple_of(j * tk, tk)
        s = s_ref[...]
        if masked:
            kpos = k0 + lax.broadcasted_iota(jnp.int32, s.shape, 0)
            qpos = q_lo + jnp.bitwise_and(lax.broadcasted_iota(jnp.int32, s.shape, 1), tq - 1)
            s = jnp.where(jnp.right_shift(kpos, CHUNK_SHIFT) <= jnp.right_shift(qpos, CHUNK_SHIFT), s, NEG_INF)
        m_old = m_scr[...]
        m_new = jnp.maximum(m_old, jnp.max(s, axis=0, keepdims=True))
        alpha = jnp.exp2(m_old - m_new)
        p = jnp.exp2(s - m_new)
        l_scr[...] = alpha * l_scr[...] + jnp.sum(p, axis=0, keepdims=True)
        pv = lax.dot_general(v_ref[pl.ds(k0, tk), :], p.astype(BF16), (((0,), (0,)), ((), ())),
                             preferred_element_type=F32)
        acc_scr[...] = alpha * acc_scr[...] + pv
        m_scr[...] = m_new

    if paired:
        scores(0, s0_scr)

        def pair(p, c):
            scores(2 * p + 1, s1_scr)
            update(2 * p, s0_scr, False)
            scores(2 * p + 2, s0_scr)
            update(2 * p + 1, s1_scr, False)
            return c

        lax.fori_loop(0, i, pair, 0)
        scores(2 * i + 1, s1_scr)
        update(2 * i, s0_scr, True)
        update(2 * i + 1, s1_scr, True)
    else:
        def full_body(j, c):
            scores(j, s0_scr)
            update(j, s0_scr, False)
            return c

        def masked_body(j, c):
            scores(j, s0_scr)
            update(j, s0_scr, True)
            return c

        lax.fori_loop(0, n_full, full_body, 0)
        lax.fori_loop(n_full, n_proc, masked_body, 0)

    lam = (jnp.exp(jnp.sum(lq1_ref[...] * lk1_ref[...], axis=-1, keepdims=True))
           - jnp.exp(jnp.sum(lq2_ref[...] * lk2_ref[...], axis=-1, keepdims=True)) + lambda_init)
    ot = acc_scr[:, :tq] / l_scr[:, :tq] - lam * (acc_scr[:, tq:] / l_scr[:, tq:])
    o_ref[...] = (_rms(ot.T, g_ref[...]) * (1.0 - lambda_init)).astype(BF16)


def _attention(q, k, v, lp, tq, tk, past, lambda_init):
    bn, seq, _ = q.shape
    lk = k.shape[1]
    vec = lambda a: a.astype(F32).reshape(1, -1)
    small = lambda n: pl.BlockSpec((1, n), lambda b, h, i: (0, 0))
    paired = past == 0 and tq == 2 * tk and lk == seq
    s_scr = pltpu.VMEM((tk, 2 * tq), F32)
    return pl.pallas_call(
        functools.partial(_attn_kernel, tq=tq, tk=tk, past=past, paired=paired, lambda_init=lambda_init),
        grid=(bn, N_HEADS, seq // tq),
        in_specs=[
            pl.BlockSpec((None, tq, V_DIM), lambda b, h, i: (b, i, h)),
            pl.BlockSpec((None, lk, V_DIM), lambda b, h, i: (b, 0, h)),
            pl.BlockSpec((None, lk, V_DIM), lambda b, h, i: (b, 0, h)),
            small(HEAD_DIM), small(HEAD_DIM), small(HEAD_DIM), small(HEAD_DIM), small(V_DIM),
        ],
        out_specs=pl.BlockSpec((None, tq, V_DIM), lambda b, h, i: (b, i, h)),
        out_shape=jax.ShapeDtypeStruct((bn, seq, V_COLS), BF16),
        scratch_shapes=[pltpu.VMEM((V_DIM, 2 * tq), BF16), s_scr, s_scr, pltpu.VMEM((1, 2 * tq), F32),
                        pltpu.VMEM((1, 2 * tq), F32), pltpu.VMEM((V_DIM, 2 * tq), F32)],
        compiler_params=_cparams("parallel", "parallel", "arbitrary"),
        name="diff_attn",
    )(q, k, v, vec(lp['lq1']), vec(lp['lk1']), vec(lp['lq2']), vec(lp['lk2']), vec(lp['g_subln']))


def _merge_kernel(x_ref, o_ref, ga_ref, sb_ref, wa_ref, wo_ref, g_ref, x1_ref, h2t_ref):
    branch_b = _dot(o_ref[...], wa_ref[...])
    merged = ga_ref[...].astype(F32) + sb_ref[...].astype(F32) * branch_b
    x1 = x_ref[...] + _dot(merged.astype(BF16), wo_ref[...])
    x1_ref[...] = x1
    h2t_ref[...] = _rms(x1, g_ref[...]).T.astype(BF16)


def _merge(x2d, o, gated_a, sig_b, wa_bf, wo_bf, g_ffn, tm):
    T = x2d.shape[0]
    row = lambda n: pl.BlockSpec((tm, n), lambda i: (i, 0))
    const = lambda s: pl.BlockSpec(s, lambda i: (0, 0))
    return pl.pallas_call(
        _merge_kernel,
        grid=(T // tm,),
        in_specs=[row(D_MODEL), row(V_COLS), row(D_MODEL), row(D_MODEL),
                  const((V_COLS, D_MODEL)), const((D_MODEL, D_MODEL)), const((1, D_MODEL))],
        out_specs=[row(D_MODEL), pl.BlockSpec((D_MODEL, tm), lambda i: (0, i))],
        out_shape=[jax.ShapeDtypeStruct((T, D_MODEL), F32), jax.ShapeDtypeStruct((D_MODEL, T), BF16)],
        compiler_params=_cparams("parallel"),
        name="merge_out",
    )(x2d, o, gated_a, sig_b, wa_bf, wo_bf, g_ffn)


ROUTE_LC = 256
TAKEN = -3.0e38


def _split_bf16(x):
    hi = x.astype(BF16)
    return hi, (x - hi.astype(F32)).astype(BF16)


def _top_rows(s, k):
    out = []
    for _ in range(k):
        m = jnp.max(s, axis=0, keepdims=True)
        out.append(m)
        s = jnp.where(s == m, TAKEN, s)
    return out


def _stack_rows(rows):
    n, w = len(rows), rows[0].shape[1]
    idx = lax.broadcasted_iota(jnp.int32, (n, w), 0)
    out = jnp.zeros((n, w), F32)
    for r, v in enumerate(rows):
        out = jnp.where(idx == r, v, out)
    return out


def _route_kernel(h2t_ref, wq_ref, keys_ref, thr_ref, s2_ref, e1_ref, e2_ref, s1_scr):
    tt = h2t_ref.shape[1]
    n_top = PEER_TOPK + 1
    qt = _dot(wq_ref[...], h2t_ref[...])
    halves = []
    for c in range(2):
        q_hi, q_lo = _split_bf16(qt[c * PEER_HALF:(c + 1) * PEER_HALF, :])
        k_hi, k_lo = _split_bf16(keys_ref[0, c])
        halves.append(_dot(k_hi, q_hi) + _dot(k_hi, q_lo) + _dot(k_lo, q_hi))
    s1_scr[...] = halves[0]
    s2_ref[0] = halves[1]

    width = min(tt, ROUTE_LC)
    for lc in range(tt // width):
        sl = slice(lc * width, (lc + 1) * width)
        s1 = s1_scr[:, sl]
        s2 = s2_ref[0, :, sl]
        a = _top_rows(s1, n_top)
        b = _top_rows(s2, n_top)
        taken = jnp.full_like(b[0], TAKEN)
        b_all = _stack_rows(b + [taken] * (3 * SUBLANES - n_top))
        cand = jnp.concatenate([a[0] + b_all] + [a[i] + b_all[:SUBLANES] for i in range(1, n_top)], axis=0)
        top = _top_rows(cand, n_top)
        z = jnp.zeros_like(top[0])
        for r in range(PEER_TOPK):
            z = z + jnp.exp(top[r] - top[0])
        tau = 0.5 * (top[PEER_TOPK - 1] + top[PEER_TOPK])
        grouped = (N_KEYS // SUBLANES, SUBLANES, width)
        thr_ref[0, :, :, sl] = (tau - s1).reshape(grouped)
        e1_ref[0, :, :, sl] = (jnp.exp(s1 - a[0]) / z).reshape(grouped)
        e2_ref[0, :, sl] = jnp.exp(s2 - b[0])


def _route(h2t, wq_t_bf, keys, tt):
    T = h2t.shape[1]
    arr = jax.ShapeDtypeStruct((PEER_HEADS, N_KEYS, T), F32)
    spec = pl.BlockSpec((1, N_KEYS, tt), lambda i, h: (h, 0, i))
    n_grp = N_KEYS // SUBLANES
    garr = jax.ShapeDtypeStruct((PEER_HEADS, n_grp, SUBLANES, T), F32)
    gspec = pl.BlockSpec((1, n_grp, SUBLANES, tt), lambda i, h: (h, 0, 0, i))
    return pl.pallas_call(
        _route_kernel,
        grid=(T // tt, PEER_HEADS),
        in_specs=[
            pl.BlockSpec((D_MODEL, tt), lambda i, h: (0, i)),
            pl.BlockSpec((PEER_QUERY, D_MODEL), lambda i, h: (h, 0)),
            pl.BlockSpec((1, 2, N_KEYS, PEER_HALF), lambda i, h: (h, 0, 0, 0)),
        ],
        out_specs=[gspec, spec, gspec, spec],
        out_shape=[garr, arr, garr, arr],
        scratch_shapes=[pltpu.VMEM((N_KEYS, tt), F32)],
        compiler_params=_cparams("parallel", "arbitrary"),
        name="peer_route",
    )(h2t, wq_t_bf, keys)


PEER_SUB = 512
PEER_EB = 2 * PEER_SUB
PEER_ROWS = PEER_SUB // N_KEYS
N_SUB = N_EXPERTS // PEER_SUB
PEER_JH = 32


def _peer_kernel(h2t_ref, thr_ref, s2_ref, e1_ref, e2_ref, u_ref, vt_ref, o_ref,
                 act0, act1, coef0, coef1, acc_scr):
    g = pl.program_id(1)
    tt = h2t_ref.shape[1]

    @pl.when(g == 0)
    def _():
        act1[...] = jnp.zeros(act1.shape, F32)
        coef0[...] = jnp.zeros(coef0.shape, F32)
        coef1[...] = jnp.zeros(coef1.shape, F32)
        acc_scr[...] = jnp.zeros(acc_scr.shape, F32)

    def stage_a(half, act):
        act[...] = _dot(u_ref[half * PEER_SUB:(half + 1) * PEER_SUB, :], h2t_ref[...])

    def stage_b(b, act, coef):
        b = jnp.clip(b, 0, N_SUB - 1)
        grp = b // 2
        odd = (b % 2) == 1
        for lc in range(tt // LANES):
            ln = slice(lc * LANES, (lc + 1) * LANES)
            for jh in range(N_KEYS // PEER_JH):
                js = slice(jh * PEER_JH, (jh + 1) * PEER_JH)
                gates = [jnp.zeros((PEER_JH, LANES), F32) for _ in range(PEER_ROWS)]
                for h in range(PEER_HEADS):
                    s2 = s2_ref[h, js, ln]
                    e2 = e2_ref[h, js, ln]
                    thr_grp = thr_ref[h, grp, :, ln]
                    e1_grp = e1_ref[h, grp, :, ln]
                    thr_rows = jnp.where(odd, thr_grp[PEER_ROWS:], thr_grp[:PEER_ROWS])
                    e1_rows = jnp.where(odd, e1_grp[PEER_ROWS:], e1_grp[:PEER_ROWS])
                    for r in range(PEER_ROWS):
                        gates[r] = gates[r] + jnp.where(s2 >= thr_rows[r:r + 1], e1_rows[r:r + 1] * e2, 0.0)
                for r in range(PEER_ROWS):
                    rows = slice(r * N_KEYS + jh * PEER_JH, r * N_KEYS + (jh + 1) * PEER_JH)
                    coef[rows, ln] = gates[r] * _gelu(act[rows, ln])

    def stage_c(half, coef):
        acc_scr[...] += _dot(vt_ref[:, half * PEER_SUB:(half + 1) * PEER_SUB], coef[...].astype(BF16))

    stage_a(0, act0)
    stage_b(2 * g - 1, act1, coef1)
    stage_c(0, coef0)
    stage_a(1, act1)
    stage_b(2 * g, act0, coef0)
    stage_c(1, coef1)

    @pl.when(g == pl.num_programs(1) - 1)
    def _():
        o_ref[...] = acc_scr[...].T.astype(BF16)


def _peer(h2t, route, u_bf, vt_bf, tt):
    T = h2t.shape[1]
    thr, s2, e1, e2 = route
    n_eb = N_EXPERTS // PEER_EB
    spec = pl.BlockSpec((PEER_HEADS, N_KEYS, tt), lambda i, g: (0, 0, i))
    gspec = pl.BlockSpec((PEER_HEADS, N_KEYS // SUBLANES, SUBLANES, tt), lambda i, g: (0, 0, 0, i))
    assert SUBLANES == 2 * PEER_ROWS
    return pl.pallas_call(
        _peer_kernel,
        grid=(T // tt, n_eb + 1),
        in_specs=[
            pl.BlockSpec((D_MODEL, tt), lambda i, g: (0, i)),
            gspec, spec, gspec, spec,
            pl.BlockSpec((PEER_EB, D_MODEL), lambda i, g: (jnp.minimum(g, n_eb - 1), 0)),
            pl.BlockSpec((D_MODEL, PEER_EB), lambda i, g: (0, jnp.maximum(g - 1, 0))),
        ],
        out_specs=pl.BlockSpec((tt, D_MODEL), lambda i, g: (i, 0)),
        out_shape=jax.ShapeDtypeStruct((T, D_MODEL), BF16),
        scratch_shapes=[pltpu.VMEM((PEER_SUB, tt), F32), pltpu.VMEM((PEER_SUB, tt), F32),
                        pltpu.VMEM((PEER_SUB, tt), F32), pltpu.VMEM((PEER_SUB, tt), F32),
                        pltpu.VMEM((D_MODEL, tt), F32)],
        compiler_params=_cparams("parallel", "arbitrary"),
        name="peer_dense",
    )(h2t, thr, s2, e1, e2, u_bf, vt_bf)


def _ple_kernel(x1_ref, po_ref, p_ref, wg_ref, wp_ref, gp_ref, gf_ref, y_ref):
    x2 = x1_ref[...] + po_ref[...].astype(F32)
    h3 = _rms(x2, gp_ref[...]).astype(BF16)
    gate = _sigmoid(_dot(h3, wg_ref[...]))
    proj = _dot(p_ref[...].astype(BF16), wp_ref[...])
    y_ref[...] = _rms(x2 + proj * gate, gf_ref[...])


def _ple(x1, peer_out, p2d, wg_bf, wp_bf, g_ple, g_final, tm):
    T = x1.shape[0]
    row = lambda n: pl.BlockSpec((tm, n), lambda i: (i, 0))
    const = lambda s: pl.BlockSpec(s, lambda i: (0, 0))
    return pl.pallas_call(
        _ple_kernel,
        grid=(T // tm,),
        in_specs=[row(D_MODEL), row(D_MODEL), row(PLE_DIM),
                  const((D_MODEL, D_MODEL)), const((PLE_DIM, D_MODEL)), const((1, D_MODEL)), const((1, D_MODEL))],
        out_specs=row(D_MODEL),
        out_shape=jax.ShapeDtypeStruct((T, D_MODEL), F32),
        compiler_params=_cparams("parallel"),
        name="ple_final",
    )(x1, peer_out, p2d, wg_bf, wp_bf, g_ple, g_final)


def _tile(n, pref):
    t = min(n, pref)
    assert n % t == 0, (n, t)
    return t


def _layer(x, p, h0_re, h0_im, k_past, v_past, lp, lambda_init):
    bn, seq, _ = x.shape
    past = k_past.shape[1]
    T = bn * seq
    x2d = x.reshape(T, D_MODEL)
    row = lambda a: a.astype(F32).reshape(1, -1)

    tm_in = _tile(T, 1024)
    cos_t, sin_t = _rope_tables(seq, past, max(seq, tm_in))
    u, q, k_f32, k_bf, v_f32, v_bf, sig_a, sig_b = _inproj(
        x2d, row(lp['g_mix']), lp['w_in'].astype(BF16), cos_t, sin_t, tm_in)

    seg_len = _tile(seq // SUBLANES, 32)
    h0 = jnp.concatenate([h0_re.reshape(bn, 1, N_STATE), h0_im.reshape(bn, 1, N_STATE)], axis=-1).astype(F32)
    y_ssm, h_fin = _s5(u.reshape(bn, seq, D_SSM), h0, _s5_tables(lp, seg_len), seg_len)
    gated_a = _glu(y_ssm.reshape(T, D_SSM), lp['w_glu'].astype(BF16), sig_a, _tile(T, 1024))

    k_all = jnp.concatenate([k_past.reshape(bn, past, Q_COLS).astype(BF16), k_bf.reshape(bn, seq, Q_COLS)], axis=1)
    v_all = jnp.concatenate([v_past.reshape(bn, past, V_COLS).astype(BF16), v_bf.reshape(bn, seq, V_COLS)], axis=1)
    seq_q = max(seq, LANES // 2)
    q3 = jnp.pad(q.reshape(bn, seq, Q_COLS), ((0, 0), (0, seq_q - seq), (0, 0)))
    tq = _tile(seq_q, 512)
    tk = tq // 2 if past == 0 else past + seq
    o = _attention(q3, k_all, v_all, lp, tq, tk, past, lambda_init)[:, :seq]

    x1, h2t = _merge(x2d, o.reshape(T, V_COLS), gated_a, sig_b, lp['w_attn_out'].astype(BF16),
                    lp['w_out'].astype(BF16), row(lp['g_ffn']), _tile(T, 256))

    tt = _tile(T, 512)
    route = _route(h2t, lp['peer_w_q'].T.astype(BF16), lp['peer_keys'].astype(F32), tt)
    peer_out = _peer(h2t, route, lp['peer_u'].astype(BF16), lp['peer_v'].T.astype(BF16), tt)

    y = _ple(x1, peer_out, p.reshape(T, PLE_DIM), lp['w_ple_gate'].astype(BF16), lp['w_ple_proj'].astype(BF16),
             row(lp['g_ple']), row(lp['g_final']), _tile(T, 256))

    new_k = k_f32.reshape(bn, seq, N_HEADS, 2 * HEAD_DIM)
    new_v = v_f32.reshape(bn, seq, N_HEADS, V_DIM)
    hr = h_fin[:, 0, :N_STATE].reshape(bn, N_SSM_GROUPS, SSM_STATE)
    hi = h_fin[:, 0, N_STATE:].reshape(bn, N_SSM_GROUPS, SSM_STATE)
    return y.reshape(bn, seq, D_MODEL), new_k, new_v, hr, hi


def _trunk(x, p, h0_re, h0_im, k_past, v_past, lp):
    assert DEPTH == 1
    lambda_init = 0.8 - 0.6 * math.exp(-0.3 * 0)
    y, k_new, v_new, hr, hi = _layer(x, p[0], h0_re[0], h0_im[0], k_past[0], v_past[0], lp, lambda_init)
    return y, k_new[None], v_new[None], hr[None], hi[None]


def kernel(x_prompt, x_sample, p_prompt, p_sample, cache_k, cache_v, state_ssm_re, state_ssm_im,
           g_mix_norm, w_in, ssm_lambda_re, ssm_lambda_im, ssm_log_step, ssm_b_re, ssm_b_im,
           ssm_c_re, ssm_c_im, ssm_d, w_glu, diff_lambda_q1, diff_lambda_k1, diff_lambda_q2,
           diff_lambda_k2, g_subln, w_attn_out, w_out, g_ffn_norm, peer_w_q, peer_keys, peer_u,
           peer_v, g_ple_norm, w_ple_gate, w_ple_proj, g_final):
    lp = dict(g_mix=g_mix_norm[0], w_in=w_in[0], lam_re=ssm_lambda_re[0], lam_im=ssm_lambda_im[0],
              log_step=ssm_log_step[0], b_re=ssm_b_re[0], b_im=ssm_b_im[0], c_re=ssm_c_re[0],
              c_im=ssm_c_im[0], d=ssm_d[0], w_glu=w_glu[0], lq1=diff_lambda_q1[0],
              lk1=diff_lambda_k1[0], lq2=diff_lambda_q2[0], lk2=diff_lambda_k2[0],
              g_subln=g_subln[0], w_attn_out=w_attn_out[0], w_out=w_out[0], g_ffn=g_ffn_norm[0],
              peer_w_q=peer_w_q[0], peer_keys=peer_keys[0], peer_u=peer_u[0], peer_v=peer_v[0],
              g_ple=g_ple_norm[0], w_ple_gate=w_ple_gate[0], w_ple_proj=w_ple_proj[0], g_final=g_final)
    bn = x_prompt.shape[0]
    zeros_state = jnp.zeros((DEPTH, bn, N_SSM_GROUPS, SSM_STATE), F32)
    k_none = jnp.zeros((DEPTH, bn, 0, N_HEADS, 2 * HEAD_DIM), x_prompt.dtype)
    v_none = jnp.zeros((DEPTH, bn, 0, N_HEADS, V_DIM), x_prompt.dtype)
    y_p, k_p, v_p, r_p, i_p = _trunk(x_prompt, p_prompt, zeros_state, zeros_state, k_none, v_none, lp)
    y_s, k_s, v_s, r_s, i_s = _trunk(x_sample, p_sample, state_ssm_re, state_ssm_im, cache_k, cache_v, lp)
    return (y_p, y_s, k_p, v_p, r_p, i_p, k_s, v_s, r_s, i_s)
```

```python
import functools
import math

import jax
import jax.numpy as jnp
from jax import lax
from jax.experimental import pallas as pl
from jax.experimental.pallas import tpu as pltpu

F32 = jnp.float32
BF16 = jnp.bfloat16

D_MODEL = 2048
DEPTH = 1
CHUNK = 64
CHUNK_SHIFT = CHUNK.bit_length() - 1
assert 1 << CHUNK_SHIFT == CHUNK
PLE_DIM = 256
RMS_EPS = 1e-6
NEG_INF = -1e30
D_SSM = 1024
SSM_GROUP = 16
N_SSM_GROUPS = D_SSM // SSM_GROUP
SSM_STATE = 64
N_STATE = N_SSM_GROUPS * SSM_STATE
N_HEADS = 8
HEAD_DIM = 64
V_DIM = 2 * HEAD_DIM
ROPE_THETA = 10000.0
Q_COLS = N_HEADS * 2 * HEAD_DIM
V_COLS = N_HEADS * V_DIM
IN_COLS = D_SSM + 2 * Q_COLS + V_COLS + 2 * D_MODEL
PEER_HEADS = 8
N_KEYS = 128
N_EXPERTS = N_KEYS * N_KEYS
PEER_QUERY = 256
PEER_HALF = PEER_QUERY // 2
PEER_TOPK = 16

LANES = 128
SUBLANES = 8
VMEM_LIMIT = 56 * 1024 * 1024


def _cparams(*sem):
    return pltpu.CompilerParams(dimension_semantics=sem, vmem_limit_bytes=VMEM_LIMIT)


def _rms(x, g):
    return x * lax.rsqrt(jnp.mean(x * x, axis=-1, keepdims=True) + RMS_EPS) * g


def _gelu(x):
    return 0.5 * x * (1.0 + lax.erf(x * (2.0 ** -0.5)))


def _sigmoid(x):
    return 1.0 / (1.0 + jnp.exp(-x))


def _dot(a, b):
    return jnp.dot(a, b, preferred_element_type=F32)


def _dot_nt(a, b):
    return lax.dot_general(a, b, (((1,), (1,)), ((), ())), preferred_element_type=F32)


IN_TN = 512
Q_SCALE = HEAD_DIM ** -0.5 * math.log2(math.e)
SEG_U = (0, D_SSM // IN_TN)
SEG_Q = (SEG_U[0] + SEG_U[1], Q_COLS // IN_TN)
SEG_K = (SEG_Q[0] + SEG_Q[1], Q_COLS // IN_TN)
SEG_V = (SEG_K[0] + SEG_K[1], V_COLS // IN_TN)
SEG_GA = (SEG_V[0] + SEG_V[1], D_MODEL // IN_TN)
SEG_GB = (SEG_GA[0] + SEG_GA[1], D_MODEL // IN_TN)


def _inproj_kernel(x_ref, g_ref, w_ref, cos_ref, sin_ref,
                   u_ref, q_ref, kf_ref, kb_ref, vf_ref, vb_ref, ga_ref, gb_ref, h_scr):
    j = pl.program_id(1)

    @pl.when(j == 0)
    def _():
        h_scr[...] = _rms(x_ref[...], g_ref[...]).astype(BF16)

    def z():
        return _dot(h_scr[...], w_ref[...])

    def rope(t):
        n = t.shape[-1]
        lane = lax.broadcasted_iota(jnp.int32, t.shape, 1)
        first = jnp.bitwise_and(lane, HEAD_DIM - 1) < (HEAD_DIM // 2)
        partner = jnp.where(first, pltpu.roll(t, n - HEAD_DIM // 2, 1), pltpu.roll(t, HEAD_DIM // 2, 1))
        return t * cos_ref[...] + partner * sin_ref[...]

    def in_seg(seg):
        return jnp.logical_and(j >= seg[0], j < seg[0] + seg[1])

    @pl.when(in_seg(SEG_U))
    def _():
        u_ref[...] = z().astype(BF16)

    @pl.when(in_seg(SEG_Q))
    def _():
        q_ref[...] = (rope(z()) * Q_SCALE).astype(BF16)

    @pl.when(in_seg(SEG_K))
    def _():
        k = rope(z())
        kf_ref[...] = k
        kb_ref[...] = k.astype(BF16)

    @pl.when(in_seg(SEG_V))
    def _():
        v = z()
        vf_ref[...] = v
        vb_ref[...] = v.astype(BF16)

    @pl.when(in_seg(SEG_GA))
    def _():
        ga_ref[...] = _sigmoid(z()).astype(BF16)

    @pl.when(in_seg(SEG_GB))
    def _():
        gb_ref[...] = _sigmoid(z()).astype(BF16)


def _inproj(x2d, g, w_bf, cos_t, sin_t, tm):
    T = x2d.shape[0]
    nj = IN_COLS // IN_TN
    n_tab = cos_t.shape[0] // tm

    def seg_map(seg):
        return lambda i, j: (i, jnp.clip(j - seg[0], 0, seg[1] - 1))

    def out(seg, dtype):
        return (jax.ShapeDtypeStruct((T, seg[1] * IN_TN), dtype), pl.BlockSpec((tm, IN_TN), seg_map(seg)))

    outs = [out(SEG_U, BF16), out(SEG_Q, BF16), out(SEG_K, F32), out(SEG_K, BF16),
            out(SEG_V, F32), out(SEG_V, BF16), out(SEG_GA, BF16), out(SEG_GB, BF16)]
    return pl.pallas_call(
        _inproj_kernel,
        grid=(T // tm, nj),
        in_specs=[
            pl.BlockSpec((tm, D_MODEL), lambda i, j: (i, 0)),
            pl.BlockSpec((1, D_MODEL), lambda i, j: (0, 0)),
            pl.BlockSpec((D_MODEL, IN_TN), lambda i, j: (0, j)),
            pl.BlockSpec((tm, IN_TN), lambda i, j: (i % n_tab, 0)),
            pl.BlockSpec((tm, IN_TN), lambda i, j: (i % n_tab, 0)),
        ],
        out_specs=[o[1] for o in outs],
        out_shape=[o[0] for o in outs],
        scratch_shapes=[pltpu.VMEM((tm, D_MODEL), BF16)],
        compiler_params=_cparams("parallel", "arbitrary"),
        name="inproj",
    )(x2d, g, w_bf, cos_t, sin_t)


def _rope_tables(seq, past, rows):
    half = HEAD_DIM // 2
    inv = ROPE_THETA ** (-jnp.arange(half, dtype=F32) * 2.0 / HEAD_DIM)
    pos = (past + (jnp.arange(rows, dtype=jnp.int32) % seq)).astype(F32)
    ang = pos[:, None] * inv[None, :]
    cos, sin = jnp.cos(ang), jnp.sin(ang)
    reps = IN_TN // HEAD_DIM
    cos_t = jnp.tile(jnp.concatenate([cos, cos], axis=-1), (1, reps))
    sin_t = jnp.tile(jnp.concatenate([-sin, sin], axis=-1), (1, reps))
    return cos_t, sin_t


S5_COLS = 512
S5_UBLK = LANES
S5_N_UBLK = D_SSM // S5_UBLK
S5_XBLK = (S5_UBLK // SSM_GROUP) * SSM_STATE
S5_YBLK = 256
S5_N_YBLK = D_SSM // S5_YBLK
S5_HBLK = (S5_YBLK // SSM_GROUP) * SSM_STATE


def _s5_kernel(u_ref, h0_ref, perm_ref, permt_ref, wx_ref, apow_ref, cr_ref, ci_ref, d_ref,
               y_ref, hfin_ref, x_scr, c_scr, carry_scr, *, seg_len):
    t = pl.program_id(1)
    rows = SUBLANES * seg_len
    n = N_STATE

    @pl.when(t == 0)
    def _():
        carry_scr[...] = h0_ref[...]

    up = _dot(perm_ref[...], u_ref[...])
    upb = up.astype(BF16)
    for a in range(S5_N_UBLK):
        xa = _dot(upb[:, a * S5_UBLK:(a + 1) * S5_UBLK], wx_ref[a])
        x_scr[:, a * S5_XBLK:(a + 1) * S5_XBLK] = xa[:, :S5_XBLK]
        x_scr[:, n + a * S5_XBLK:n + (a + 1) * S5_XBLK] = xa[:, S5_XBLK:]

    for c in range(n // S5_COLS):
        lo = c * S5_COLS
        ar = jnp.broadcast_to(apow_ref[0:1, lo:lo + S5_COLS], (SUBLANES, S5_COLS))
        ai = jnp.broadcast_to(apow_ref[0:1, n + lo:n + lo + S5_COLS], (SUBLANES, S5_COLS))

        def scan_step(k, carry, lo=lo, ar=ar, ai=ai):
            hr, hi = carry
            r0 = pl.multiple_of(k * SUBLANES, SUBLANES)
            xr = x_scr[pl.ds(r0, SUBLANES), lo:lo + S5_COLS]
            xi = x_scr[pl.ds(r0, SUBLANES), n + lo:n + lo + S5_COLS]
            nr = ar * hr - ai * hi + xr
            ni = ar * hi + ai * hr + xi
            x_scr[pl.ds(r0, SUBLANES), lo:lo + S5_COLS] = nr
            x_scr[pl.ds(r0, SUBLANES), n + lo:n + lo + S5_COLS] = ni
            return nr, ni

        zero = jnp.zeros((SUBLANES, S5_COLS), F32)
        lax.fori_loop(0, seg_len, scan_step, (zero, zero))

    alr = apow_ref[seg_len - 1:seg_len, :n]
    ali = apow_ref[seg_len - 1:seg_len, n:]
    cr = carry_scr[:, :n]
    ci = carry_scr[:, n:]
    for s in range(SUBLANES):
        c_scr[s:s + 1, :n] = cr
        c_scr[s:s + 1, n:] = ci
        lr = x_scr[rows - SUBLANES + s:rows - SUBLANES + s + 1, :n]
        li = x_scr[rows - SUBLANES + s:rows - SUBLANES + s + 1, n:]
        cr, ci = alr * cr - ali * ci + lr, alr * ci + ali * cr + li
    carry_scr[:, :n] = cr
    carry_scr[:, n:] = ci

    for c in range(n // S5_COLS):
        lo = c * S5_COLS
        sr = c_scr[:, lo:lo + S5_COLS]
        si = c_scr[:, n + lo:n + lo + S5_COLS]

        def fix_step(k, _, lo=lo, sr=sr, si=si):
            r0 = pl.multiple_of(k * SUBLANES, SUBLANES)
            pr = apow_ref[pl.ds(k, 1), lo:lo + S5_COLS]
            pi = apow_ref[pl.ds(k, 1), n + lo:n + lo + S5_COLS]
            x_scr[pl.ds(r0, SUBLANES), lo:lo + S5_COLS] += pr * sr - pi * si
            x_scr[pl.ds(r0, SUBLANES), n + lo:n + lo + S5_COLS] += pr * si + pi * sr
            return 0

        lax.fori_loop(0, seg_len, fix_step, 0)

    @pl.when(t == pl.num_programs(1) - 1)
    def _():
        hfin_ref[...] = carry_scr[...]

    for j in range(S5_N_YBLK):
        hr = x_scr[:, j * S5_HBLK:(j + 1) * S5_HBLK].astype(BF16)
        hi = x_scr[:, n + j * S5_HBLK:n + (j + 1) * S5_HBLK].astype(BF16)
        yj = (_dot(hr, cr_ref[j]) + _dot(hi, ci_ref[j])
              + d_ref[:, j * S5_YBLK:(j + 1) * S5_YBLK] * up[:, j * S5_YBLK:(j + 1) * S5_YBLK])
        yj = _gelu(yj).astype(BF16)
        y_ref[:, j * S5_YBLK:(j + 1) * S5_YBLK] = _dot(permt_ref[...], yj).astype(BF16)


def _s5_tables(lp, seg_len):
    g, p, c = N_SSM_GROUPS, SSM_STATE, SSM_GROUP
    dt = jnp.exp(lp['log_step'].astype(F32))[:, None]
    lr, li = lp['lam_re'].astype(F32), lp['lam_im'].astype(F32)
    mag = jnp.exp(lr * dt)
    ar, ai = mag * jnp.cos(li * dt), mag * jnp.sin(li * dt)
    den = lr * lr + li * li
    fr = ((ar - 1.0) * lr + ai * li) / den
    fi = (ai * lr - (ar - 1.0) * li) / den
    br, bi = lp['b_re'].astype(F32), lp['b_im'].astype(F32)
    bbr = fr[..., None] * br - fi[..., None] * bi
    bbi = fr[..., None] * bi + fi[..., None] * br

    gpb = S5_UBLK // c
    eye = jnp.eye(gpb, dtype=F32)

    def in_blocks(bb):
        bb = bb.reshape(S5_N_UBLK, gpb, p, c)
        return jnp.einsum('xy,axpc->axcyp', eye, bb).reshape(S5_N_UBLK, S5_UBLK, S5_XBLK)

    wx = jnp.concatenate([in_blocks(bbr), in_blocks(bbi)], axis=-1).astype(BF16)

    pr, pi = ar.reshape(1, g * p), ai.reshape(1, g * p)
    while pr.shape[0] < seg_len:
        tr, ti = pr[-1:], pi[-1:]
        pr, pi = (jnp.concatenate([pr, pr * tr - pi * ti], axis=0),
                  jnp.concatenate([pi, pr * ti + pi * tr], axis=0))
    apow = jnp.concatenate([pr[:seg_len], pi[:seg_len]], axis=-1)

    gpy = S5_YBLK // c
    eye_y = jnp.eye(gpy, dtype=F32)

    def out_blocks(cm):
        cm = cm.astype(F32).reshape(S5_N_YBLK, gpy, c, p)
        return jnp.einsum('xy,jxcp->jxpyc', eye_y, cm).reshape(S5_N_YBLK, S5_HBLK, S5_YBLK)

    cr = out_blocks(lp['c_re']).astype(BF16)
    ci = (-out_blocks(lp['c_im'])).astype(BF16)
    d = lp['d'].astype(F32).reshape(1, D_SSM)
    return wx, apow, cr, ci, d


def _s5(u, h0, tables, seg_len):
    bn, seq, _ = u.shape
    rows = SUBLANES * seg_len
    wx, apow, cr, ci, d = tables
    r = jnp.arange(rows)
    perm = (r[None, :] == ((r % SUBLANES) * seg_len + r // SUBLANES)[:, None]).astype(BF16)
    const2 = lambda b, t: (0, 0)
    const3 = lambda b, t: (0, 0, 0)
    return pl.pallas_call(
        functools.partial(_s5_kernel, seg_len=seg_len),
        grid=(bn, seq // rows),
        in_specs=[
            pl.BlockSpec((None, rows, D_SSM), lambda b, t: (b, t, 0)),
            pl.BlockSpec((None, 1, 2 * N_STATE), lambda b, t: (b, 0, 0)),
            pl.BlockSpec((rows, rows), const2),
            pl.BlockSpec((rows, rows), const2),
            pl.BlockSpec(wx.shape, const3),
            pl.BlockSpec(apow.shape, const2),
            pl.BlockSpec(cr.shape, const3),
            pl.BlockSpec(ci.shape, const3),
            pl.BlockSpec((1, D_SSM), const2),
        ],
        out_specs=[
            pl.BlockSpec((None, rows, D_SSM), lambda b, t: (b, t, 0)),
            pl.BlockSpec((None, 1, 2 * N_STATE), lambda b, t: (b, 0, 0)),
        ],
        out_shape=[jax.ShapeDtypeStruct((bn, seq, D_SSM), BF16),
                   jax.ShapeDtypeStruct((bn, 1, 2 * N_STATE), F32)],
        scratch_shapes=[pltpu.VMEM((rows, 2 * N_STATE), F32),
                        pltpu.VMEM((SUBLANES, 2 * N_STATE), F32),
                        pltpu.VMEM((1, 2 * N_STATE), F32)],
        compiler_params=_cparams("parallel", "arbitrary"),
        name="s5",
    )(u, h0, perm, perm.T, wx, apow, cr, ci, d)


GLU_TN = 512


def _glu_kernel(y_ref, wa_ref, wb_ref, ga_ref, o_ref):
    y = y_ref[...]
    a = _dot(y, wa_ref[...])
    b = _dot(y, wb_ref[...])
    o_ref[...] = (ga_ref[...].astype(F32) * a * _sigmoid(b)).astype(BF16)


def _glu(y, w_glu_bf, sig_a, tm):
    T = y.shape[0]
    nj = D_MODEL // GLU_TN
    return pl.pallas_call(
        _glu_kernel,
        grid=(T // tm, nj),
        in_specs=[
            pl.BlockSpec((tm, D_SSM), lambda i, j: (i, 0)),
            pl.BlockSpec((D_SSM, GLU_TN), lambda i, j: (0, j)),
            pl.BlockSpec((D_SSM, GLU_TN), lambda i, j: (0, j + nj)),
            pl.BlockSpec((tm, GLU_TN), lambda i, j: (i, j)),
        ],
        out_specs=pl.BlockSpec((tm, GLU_TN), lambda i, j: (i, j)),
        out_shape=jax.ShapeDtypeStruct((T, D_MODEL), BF16),
        compiler_params=_cparams("parallel", "arbitrary"),
        name="glu",
    )(y, w_glu_bf, w_glu_bf, sig_a)


def _attn_kernel(q_ref, k_ref, v_ref, lq1_ref, lk1_ref, lq2_ref, lk2_ref, g_ref, o_ref,
                 qt_scr, s0_scr, s1_scr, m_scr, l_scr, acc_scr, *, tq, tk, past, paired, lambda_init):
    i = pl.program_id(2)
    nk = k_ref.shape[0] // tk
    q = q_ref[...].astype(F32)
    lane = lax.broadcasted_iota(jnp.int32, q.shape, 1)
    qs = jnp.concatenate([jnp.where(lane < HEAD_DIM, q, 0.0), jnp.where(lane >= HEAD_DIM, q, 0.0)], axis=0)
    qt_scr[...] = qs.T.astype(BF16)

    m_scr[...] = jnp.full(m_scr.shape, NEG_INF, F32)
    l_scr[...] = jnp.zeros(l_scr.shape, F32)
    acc_scr[...] = jnp.zeros(acc_scr.shape, F32)

    q_lo = past + i * tq
    min_qc = q_lo // CHUNK
    max_qc = (q_lo + tq - 1) // CHUNK
    n_proc = jnp.minimum(nk, (max_qc * CHUNK + CHUNK - 1) // tk + 1)
    n_full = jnp.clip((min_qc * CHUNK + CHUNK) // tk, 0, n_proc)

    def scores(j, s_ref):
        k0 = pl.multiple_of(j * tk, tk)
        s_ref[...] = _dot(k_ref[pl.ds(k0, tk), :], qt_scr[...])

    def update(j, s_ref, masked):
        k0 = pl.multiple_of(j * tk, tk)
        s = s_ref[...]
        if masked:
            kpos = k0 + lax.broadcasted_iota(jnp.int32, s.shape, 0)
            qpos = q_lo + jnp.bitwise_and(lax.broadcasted_iota(jnp.int32, s.shape, 1), tq - 1)
            s = jnp.where(jnp.right_shift(kpos, CHUNK_SHIFT) <= jnp.right_shift(qpos, CHUNK_SHIFT), s, NEG_INF)
        m_old = m_scr[...]
        m_new = jnp.maximum(m_old, jnp.max(s, axis=0, keepdims=True))
        alpha = jnp.exp2(m_old - m_new)
        p = jnp.exp2(s - m_new)
        l_scr[...] = alpha * l_scr[...] + jnp.sum(p, axis=0, keepdims=True)
        pv = lax.dot_general(v_ref[pl.ds(k0, tk), :], p.astype(BF16), (((0,), (0,)), ((), ())),
                             preferred_element_type=F32)
        acc_scr[...] = alpha * acc_scr[...] + pv
        m_scr[...] = m_new

    if paired:
        scores(0, s0_scr)

        def pair(p, c):
            scores(2 * p + 1, s1_scr)
            update(2 * p, s0_scr, False)
            scores(2 * p + 2, s0_scr)
            update(2 * p + 1, s1_scr, False)
            return c

        lax.fori_loop(0, i, pair, 0)
        scores(2 * i + 1, s1_scr)
        update(2 * i, s0_scr, True)
        update(2 * i + 1, s1_scr, True)
    else:
        def full_body(j, c):
            scores(j, s0_scr)
            update(j, s0_scr, False)
            return c

        def masked_body(j, c):
            scores(j, s0_scr)
            update(j, s0_scr, True)
            return c

        lax.fori_loop(0, n_full, full_body, 0)
        lax.fori_loop(n_full, n_proc, masked_body, 0)

    lam = (jnp.exp(jnp.sum(lq1_ref[...] * lk1_ref[...], axis=-1, keepdims=True))
           - jnp.exp(jnp.sum(lq2_ref[...] * lk2_ref[...], axis=-1, keepdims=True)) + lambda_init)
    ot = acc_scr[:, :tq] / l_scr[:, :tq] - lam * (acc_scr[:, tq:] / l_scr[:, tq:])
    o_ref[...] = (_rms(ot.T, g_ref[...]) * (1.0 - lambda_init)).astype(BF16)


def _attention(q, k, v, lp, tq, tk, past, lambda_init):
    bn, seq, _ = q.shape
    lk = k.shape[1]
    vec = lambda a: a.astype(F32).reshape(1, -1)
    small = lambda n: pl.BlockSpec((1, n), lambda b, h, i: (0, 0))
    paired = past == 0 and tq == 2 * tk and lk == seq
    s_scr = pltpu.VMEM((tk, 2 * tq), F32)
    return pl.pallas_call(
        functools.partial(_attn_kernel, tq=tq, tk=tk, past=past, paired=paired, lambda_init=lambda_init),
        grid=(bn, N_HEADS, seq // tq),
        in_specs=[
            pl.BlockSpec((None, tq, V_DIM), lambda b, h, i: (b, i, h)),
            pl.BlockSpec((None, lk, V_DIM), lambda b, h, i: (b, 0, h)),
            pl.BlockSpec((None, lk, V_DIM), lambda b, h, i: (b, 0, h)),
            small(HEAD_DIM), small(HEAD_DIM), small(HEAD_DIM), small(HEAD_DIM), small(V_DIM),
        ],
        out_specs=pl.BlockSpec((None, tq, V_DIM), lambda b, h, i: (b, i, h)),
        out_shape=jax.ShapeDtypeStruct((bn, seq, V_COLS), BF16),
        scratch_shapes=[pltpu.VMEM((V_DIM, 2 * tq), BF16), s_scr, s_scr, pltpu.VMEM((1, 2 * tq), F32),
                        pltpu.VMEM((1, 2 * tq), F32), pltpu.VMEM((V_DIM, 2 * tq), F32)],
        compiler_params=_cparams("parallel", "parallel", "arbitrary"),
        name="diff_attn",
    )(q, k, v, vec(lp['lq1']), vec(lp['lk1']), vec(lp['lq2']), vec(lp['lk2']), vec(lp['g_subln']))


def _merge_kernel(x_ref, o_ref, ga_ref, sb_ref, wa_ref, wo_ref, g_ref, x1_ref, h2t_ref):
    branch_b = _dot(o_ref[...], wa_ref[...])
    merged = ga_ref[...].astype(F32) + sb_ref[...].astype(F32) * branch_b
    x1 = x_ref[...] + _dot(merged.astype(BF16), wo_ref[...])
    x1_ref[...] = x1
    h2t_ref[...] = _rms(x1, g_ref[...]).T.astype(BF16)


def _merge(x2d, o, gated_a, sig_b, wa_bf, wo_bf, g_ffn, tm):
    T = x2d.shape[0]
    row = lambda n: pl.BlockSpec((tm, n), lambda i: (i, 0))
    const = lambda s: pl.BlockSpec(s, lambda i: (0, 0))
    return pl.pallas_call(
        _merge_kernel,
        grid=(T // tm,),
        in_specs=[row(D_MODEL), row(V_COLS), row(D_MODEL), row(D_MODEL),
                  const((V_COLS, D_MODEL)), const((D_MODEL, D_MODEL)), const((1, D_MODEL))],
        out_specs=[row(D_MODEL), pl.BlockSpec((D_MODEL, tm), lambda i: (0, i))],
        out_shape=[jax.ShapeDtypeStruct((T, D_MODEL), F32), jax.ShapeDtypeStruct((D_MODEL, T), BF16)],
        compiler_params=_cparams("parallel"),
        name="merge_out",
    )(x2d, o, gated_a, sig_b, wa_bf, wo_bf, g_ffn)


ROUTE_LC = 256
TAKEN = -3.0e38


def _split_bf16(x):
    hi = x.astype(BF16)
    return hi, (x - hi.astype(F32)).astype(BF16)


def _top_rows(s, k):
    out = []
    for _ in range(k):
        m = jnp.max(s, axis=0, keepdims=True)
        out.append(m)
        s = jnp.where(s == m, TAKEN, s)
    return out


def _stack_rows(rows):
    n, w = len(rows), rows[0].shape[1]
    idx = lax.broadcasted_iota(jnp.int32, (n, w), 0)
    out = jnp.zeros((n, w), F32)
    for r, v in enumerate(rows):
        out = jnp.where(idx == r, v, out)
    return out


def _route_kernel(h2t_ref, wq_ref, keys_ref, thr_ref, s2_ref, e1_ref, e2_ref, s1_scr):
    tt = h2t_ref.shape[1]
    n_top = PEER_TOPK + 1
    qt = _dot(wq_ref[...], h2t_ref[...])
    halves = []
    for c in range(2):
        q_hi, q_lo = _split_bf16(qt[c * PEER_HALF:(c + 1) * PEER_HALF, :])
        k_hi, k_lo = _split_bf16(keys_ref[0, c])
        halves.append(_dot(k_hi, q_hi) + _dot(k_hi, q_lo) + _dot(k_lo, q_hi))
    s1_scr[...] = halves[0]
    s2_ref[0] = halves[1]

    width = min(tt, ROUTE_LC)
    for lc in range(tt // width):
        sl = slice(lc * width, (lc + 1) * width)
        s1 = s1_scr[:, sl]
        s2 = s2_ref[0, :, sl]
        a = _top_rows(s1, n_top)
        b = _top_rows(s2, n_top)
        taken = jnp.full_like(b[0], TAKEN)
        b_all = _stack_rows(b + [taken] * (3 * SUBLANES - n_top))
        cand = jnp.concatenate([a[0] + b_all] + [a[i] + b_all[:SUBLANES] for i in range(1, n_top)], axis=0)
        top = _top_rows(cand, n_top)
        z = jnp.zeros_like(top[0])
        for r in range(PEER_TOPK):
            z = z + jnp.exp(top[r] - top[0])
        tau = 0.5 * (top[PEER_TOPK - 1] + top[PEER_TOPK])
        grouped = (N_KEYS // SUBLANES, SUBLANES, width)
        thr_ref[0, :, :, sl] = (tau - s1).reshape(grouped)
        e1_ref[0, :, :, sl] = (jnp.exp(s1 - a[0]) * (0.5 / z)).reshape(grouped)
        e2_ref[0, :, sl] = jnp.exp(s2 - b[0])


def _route(h2t, wq_t_bf, keys, tt):
    T = h2t.shape[1]
    arr = jax.ShapeDtypeStruct((PEER_HEADS, N_KEYS, T), F32)
    spec = pl.BlockSpec((1, N_KEYS, tt), lambda i, h: (h, 0, i))
    n_grp = N_KEYS // SUBLANES
    garr = jax.ShapeDtypeStruct((PEER_HEADS, n_grp, SUBLANES, T), F32)
    gspec = pl.BlockSpec((1, n_grp, SUBLANES, tt), lambda i, h: (h, 0, 0, i))
    return pl.pallas_call(
        _route_kernel,
        grid=(T // tt, PEER_HEADS),
        in_specs=[
            pl.BlockSpec((D_MODEL, tt), lambda i, h: (0, i)),
            pl.BlockSpec((PEER_QUERY, D_MODEL), lambda i, h: (h, 0)),
            pl.BlockSpec((1, 2, N_KEYS, PEER_HALF), lambda i, h: (h, 0, 0, 0)),
        ],
        out_specs=[gspec, spec, gspec, spec],
        out_shape=[garr, arr, garr, arr],
        scratch_shapes=[pltpu.VMEM((N_KEYS, tt), F32)],
        compiler_params=_cparams("parallel", "arbitrary"),
        name="peer_route",
    )(h2t, wq_t_bf, keys)


PEER_SUB = 512
PEER_EB = 2 * PEER_SUB
PEER_ROWS = PEER_SUB // N_KEYS
N_SUB = N_EXPERTS // PEER_SUB
PEER_JH = 32


def _peer_kernel(h2t_ref, thr_ref, s2_ref, e1_ref, e2_ref, u_ref, v_ref, o_ref,
                 act0, act1, coef0, coef1, acc_scr):
    g = pl.program_id(1)
    tt = h2t_ref.shape[1]

    @pl.when(g == 0)
    def _():
        act1[...] = jnp.zeros(act1.shape, F32)
        coef0[...] = jnp.zeros(coef0.shape, F32)
        coef1[...] = jnp.zeros(coef1.shape, F32)
        acc_scr[...] = jnp.zeros(acc_scr.shape, F32)

    def stage_a(half, act):
        act[...] = _dot(u_ref[half * PEER_SUB:(half + 1) * PEER_SUB, :], h2t_ref[...])

    def stage_b(b, act, coef):
        b = jnp.clip(b, 0, N_SUB - 1)
        grp = b // 2
        odd = (b % 2) == 1
        for lc in range(tt // LANES):
            ln = slice(lc * LANES, (lc + 1) * LANES)
            for jh in range(N_KEYS // PEER_JH):
                js = slice(jh * PEER_JH, (jh + 1) * PEER_JH)
                gates = [jnp.zeros((PEER_JH, LANES), F32) for _ in range(PEER_ROWS)]
                for h in range(PEER_HEADS):
                    s2 = s2_ref[h, js, ln]
                    e2 = e2_ref[h, js, ln]
                    thr_grp = thr_ref[h, grp, :, ln]
                    e1_grp = e1_ref[h, grp, :, ln]
                    thr_rows = jnp.where(odd, thr_grp[PEER_ROWS:], thr_grp[:PEER_ROWS])
                    e1_rows = jnp.where(odd, e1_grp[PEER_ROWS:], e1_grp[:PEER_ROWS])
                    for r in range(PEER_ROWS):
                        gates[r] = gates[r] + jnp.where(s2 >= thr_rows[r:r + 1], e1_rows[r:r + 1] * e2, 0.0)
                for r in range(PEER_ROWS):
                    rows = slice(r * N_KEYS + jh * PEER_JH, r * N_KEYS + (jh + 1) * PEER_JH)
                    x = act[rows, ln]
                    coef[rows, ln] = gates[r] * (x * (1.0 + lax.erf(x * (2.0 ** -0.5))))

    def stage_c():
        coef = jnp.concatenate([coef0[...].astype(BF16), coef1[...].astype(BF16)], axis=0)
        acc_scr[...] += lax.dot_general(coef, v_ref[...], (((0,), (0,)), ((), ())), preferred_element_type=F32)

    stage_a(0, act0)
    stage_b(2 * g - 1, act1, coef1)
    stage_a(1, act1)
    stage_c()
    stage_b(2 * g, act0, coef0)

    @pl.when(g == pl.num_programs(1) - 1)
    def _():
        o_ref[...] = acc_scr[...].astype(BF16)


def _peer(h2t, route, u_bf, v_bf, tt):
    T = h2t.shape[1]
    thr, s2, e1, e2 = route
    n_eb = N_EXPERTS // PEER_EB
    spec = pl.BlockSpec((PEER_HEADS, N_KEYS, tt), lambda i, g: (0, 0, i))
    gspec = pl.BlockSpec((PEER_HEADS, N_KEYS // SUBLANES, SUBLANES, tt), lambda i, g: (0, 0, 0, i))
    assert SUBLANES == 2 * PEER_ROWS
    return pl.pallas_call(
        _peer_kernel,
        grid=(T // tt, n_eb + 1),
        in_specs=[
            pl.BlockSpec((D_MODEL, tt), lambda i, g: (0, i)),
            gspec, spec, gspec, spec,
            pl.BlockSpec((PEER_EB, D_MODEL), lambda i, g: (jnp.minimum(g, n_eb - 1), 0)),
            pl.BlockSpec((PEER_EB, D_MODEL), lambda i, g: (jnp.maximum(g - 1, 0), 0)),
        ],
        out_specs=pl.BlockSpec((tt, D_MODEL), lambda i, g: (i, 0)),
        out_shape=jax.ShapeDtypeStruct((T, D_MODEL), BF16),
        scratch_shapes=[pltpu.VMEM((PEER_SUB, tt), F32), pltpu.VMEM((PEER_SUB, tt), F32),
                        pltpu.VMEM((PEER_SUB, tt), F32), pltpu.VMEM((PEER_SUB, tt), F32),
                        pltpu.VMEM((tt, D_MODEL), F32)],
        compiler_params=_cparams("parallel", "arbitrary"),
        name="peer_dense",
    )(h2t, thr, s2, e1, e2, u_bf, v_bf)


def _ple_kernel(x1_ref, po_ref, p_ref, wg_ref, wp_ref, gp_ref, gf_ref, y_ref):
    x2 = x1_ref[...] + po_ref[...].astype(F32)
    h3 = _rms(x2, gp_ref[...]).astype(BF16)
    gate = _sigmoid(_dot(h3, wg_ref[...]))
    proj = _dot(p_ref[...].astype(BF16), wp_ref[...])
    y_ref[...] = _rms(x2 + proj * gate, gf_ref[...])


def _ple(x1, peer_out, p2d, wg_bf, wp_bf, g_ple, g_final, tm):
    T = x1.shape[0]
    row = lambda n: pl.BlockSpec((tm, n), lambda i: (i, 0))
    const = lambda s: pl.BlockSpec(s, lambda i: (0, 0))
    return pl.pallas_call(
        _ple_kernel,
        grid=(T // tm,),
        in_specs=[row(D_MODEL), row(D_MODEL), row(PLE_DIM),
                  const((D_MODEL, D_MODEL)), const((PLE_DIM, D_MODEL)), const((1, D_MODEL)), const((1, D_MODEL))],
        out_specs=row(D_MODEL),
        out_shape=jax.ShapeDtypeStruct((T, D_MODEL), F32),
        compiler_params=_cparams("parallel"),
        name="ple_final",
    )(x1, peer_out, p2d, wg_bf, wp_bf, g_ple, g_final)


def _tile(n, pref):
    t = min(n, pref)
    assert n % t == 0, (n, t)
    return t


def _layer(x, p, h0_re, h0_im, k_past, v_past, lp, lambda_init):
    bn, seq, _ = x.shape
    past = k_past.shape[1]
    T = bn * seq
    x2d = x.reshape(T, D_MODEL)
    row = lambda a: a.astype(F32).reshape(1, -1)

    tm_in = _tile(T, 1024)
    cos_t, sin_t = _rope_tables(seq, past, max(seq, tm_in))
    u, q, k_f32, k_bf, v_f32, v_bf, sig_a, sig_b = _inproj(
        x2d, row(lp['g_mix']), lp['w_in'].astype(BF16), cos_t, sin_t, tm_in)

    seg_len = _tile(seq // SUBLANES, 32)
    h0 = jnp.concatenate([h0_re.reshape(bn, 1, N_STATE), h0_im.reshape(bn, 1, N_STATE)], axis=-1).astype(F32)
    y_ssm, h_fin = _s5(u.reshape(bn, seq, D_SSM), h0, _s5_tables(lp, seg_len), seg_len)
    gated_a = _glu(y_ssm.reshape(T, D_SSM), lp['w_glu'].astype(BF16), sig_a, _tile(T, 1024))

    k_all = jnp.concatenate([k_past.reshape(bn, past, Q_COLS).astype(BF16), k_bf.reshape(bn, seq, Q_COLS)], axis=1)
    v_all = jnp.concatenate([v_past.reshape(bn, past, V_COLS).astype(BF16), v_bf.reshape(bn, seq, V_COLS)], axis=1)
    seq_q = max(seq, LANES // 2)
    q3 = jnp.pad(q.reshape(bn, seq, Q_COLS), ((0, 0), (0, seq_q - seq), (0, 0)))
    tq = _tile(seq_q, 512)
    tk = tq // 2 if past == 0 else past + seq
    o = _attention(q3, k_all, v_all, lp, tq, tk, past, lambda_init)[:, :seq]

    x1, h2t = _merge(x2d, o.reshape(T, V_COLS), gated_a, sig_b, lp['w_attn_out'].astype(BF16),
                    lp['w_out'].astype(BF16), row(lp['g_ffn']), _tile(T, 256))

    tt = _tile(T, 512)
    route = _route(h2t, lp['peer_w_q'].T.astype(BF16), lp['peer_keys'].astype(F32), tt)
    peer_out = _peer(h2t, route, lp['peer_u'].astype(BF16), lp['peer_v'].astype(BF16), tt)

    y = _ple(x1, peer_out, p.reshape(T, PLE_DIM), lp['w_ple_gate'].astype(BF16), lp['w_ple_proj'].astype(BF16),
             row(lp['g_ple']), row(lp['g_final']), _tile(T, 256))

    new_k = k_f32.reshape(bn, seq, N_HEADS, 2 * HEAD_DIM)
    new_v = v_f32.reshape(bn, seq, N_HEADS, V_DIM)
    hr = h_fin[:, 0, :N_STATE].reshape(bn, N_SSM_GROUPS, SSM_STATE)
    hi = h_fin[:, 0, N_STATE:].reshape(bn, N_SSM_GROUPS, SSM_STATE)
    return y.reshape(bn, seq, D_MODEL), new_k, new_v, hr, hi


def _trunk(x, p, h0_re, h0_im, k_past, v_past, lp):
    assert DEPTH == 1
    lambda_init = 0.8 - 0.6 * math.exp(-0.3 * 0)
    y, k_new, v_new, hr, hi = _layer(x, p[0], h0_re[0], h0_im[0], k_past[0], v_past[0], lp, lambda_init)
    return y, k_new[None], v_new[None], hr[None], hi[None]


def kernel(x_prompt, x_sample, p_prompt, p_sample, cache_k, cache_v, state_ssm_re, state_ssm_im,
           g_mix_norm, w_in, ssm_lambda_re, ssm_lambda_im, ssm_log_step, ssm_b_re, ssm_b_im,
           ssm_c_re, ssm_c_im, ssm_d, w_glu, diff_lambda_q1, diff_lambda_k1, diff_lambda_q2,
           diff_lambda_k2, g_subln, w_attn_out, w_out, g_ffn_norm, peer_w_q, peer_keys, peer_u,
           peer_v, g_ple_norm, w_ple_gate, w_ple_proj, g_final):
    lp = dict(g_mix=g_mix_norm[0], w_in=w_in[0], lam_re=ssm_lambda_re[0], lam_im=ssm_lambda_im[0],
              log_step=ssm_log_step[0], b_re=ssm_b_re[0], b_im=ssm_b_im[0], c_re=ssm_c_re[0],
              c_im=ssm_c_im[0], d=ssm_d[0], w_glu=w_glu[0], lq1=diff_lambda_q1[0],
              lk1=diff_lambda_k1[0], lq2=diff_lambda_q2[0], lk2=diff_lambda_k2[0],
              g_subln=g_subln[0], w_attn_out=w_attn_out[0], w_out=w_out[0], g_ffn=g_ffn_norm[0],
              peer_w_q=peer_w_q[0], peer_keys=peer_keys[0], peer_u=peer_u[0], peer_v=peer_v[0],
              g_ple=g_ple_norm[0], w_ple_gate=w_ple_gate[0], w_ple_proj=w_ple_proj[0], g_final=g_final)
    bn = x_prompt.shape[0]
    zeros_state = jnp.zeros((DEPTH, bn, N_SSM_GROUPS, SSM_STATE), F32)
    k_none = jnp.zeros((DEPTH, bn, 0, N_HEADS, 2 * HEAD_DIM), x_prompt.dtype)
    v_none = jnp.zeros((DEPTH, bn, 0, N_HEADS, V_DIM), x_prompt.dtype)
    y_p, k_p, v_p, r_p, i_p = _trunk(x_prompt, p_prompt, zeros_state, zeros_state, k_none, v_none, lp)
    y_s, k_s, v_s, r_s, i_s = _trunk(x_sample, p_sample, state_ssm_re, state_ssm_im, cache_k, cache_v, lp)
    return (y_p, y_s, k_p, v_p, r_p, i_p, k_s, v_s, r_s, i_s)
```

```python
import functools
import math

import jax
import jax.numpy as jnp
from jax import lax
from jax.experimental import pallas as pl
from jax.experimental.pallas import tpu as pltpu

F32 = jnp.float32
BF16 = jnp.bfloat16

D_MODEL = 2048
DEPTH = 1
CHUNK = 64
CHUNK_SHIFT = CHUNK.bit_length() - 1
assert 1 << CHUNK_SHIFT == CHUNK
PLE_DIM = 256
RMS_EPS = 1e-6
NEG_INF = -1e30
D_SSM = 1024
SSM_GROUP = 16
N_SSM_GROUPS = D_SSM // SSM_GROUP
SSM_STATE = 64
N_STATE = N_SSM_GROUPS * SSM_STATE
N_HEADS = 8
HEAD_DIM = 64
V_DIM = 2 * HEAD_DIM
ROPE_THETA = 10000.0
Q_COLS = N_HEADS * 2 * HEAD_DIM
V_COLS = N_HEADS * V_DIM
IN_COLS = D_SSM + 2 * Q_COLS + V_COLS + 2 * D_MODEL
PEER_HEADS = 8
N_KEYS = 128
N_EXPERTS = N_KEYS * N_KEYS
PEER_QUERY = 256
PEER_HALF = PEER_QUERY // 2
PEER_TOPK = 16

LANES = 128
SUBLANES = 8
VMEM_LIMIT = 56 * 1024 * 1024


def _cparams(*sem):
    return pltpu.CompilerParams(dimension_semantics=sem, vmem_limit_bytes=VMEM_LIMIT)


def _rms(x, g):
    return x * lax.rsqrt(jnp.mean(x * x, axis=-1, keepdims=True) + RMS_EPS) * g


def _gelu(x):
    return 0.5 * x * (1.0 + lax.erf(x * (2.0 ** -0.5)))


def _sigmoid(x):
    return 1.0 / (1.0 + jnp.exp(-x))


def _dot(a, b):
    return jnp.dot(a, b, preferred_element_type=F32)


def _dot_nt(a, b):
    return lax.dot_general(a, b, (((1,), (1,)), ((), ())), preferred_element_type=F32)


IN_TN = 512
Q_SCALE = HEAD_DIM ** -0.5 * math.log2(math.e)
SEG_U = (0, D_SSM // IN_TN)
SEG_Q = (SEG_U[0] + SEG_U[1], Q_COLS // IN_TN)
SEG_K = (SEG_Q[0] + SEG_Q[1], Q_COLS // IN_TN)
SEG_V = (SEG_K[0] + SEG_K[1], V_COLS // IN_TN)
SEG_GA = (SEG_V[0] + SEG_V[1], D_MODEL // IN_TN)
SEG_GB = (SEG_GA[0] + SEG_GA[1], D_MODEL // IN_TN)


def _inproj_kernel(x_ref, g_ref, w_ref, cos_ref, sin_ref,
                   u_ref, q_ref, kf_ref, kb_ref, vf_ref, vb_ref, ga_ref, gb_ref, h_scr):
    j = pl.program_id(1)

    @pl.when(j == 0)
    def _():
        h_scr[...] = _rms(x_ref[...], g_ref[...]).astype(BF16)

    def z():
        return _dot(h_scr[...], w_ref[...])

    def rope(t):
        n = t.shape[-1]
        lane = lax.broadcasted_iota(jnp.int32, t.shape, 1)
        first = jnp.bitwise_and(lane, HEAD_DIM - 1) < (HEAD_DIM // 2)
        partner = jnp.where(first, pltpu.roll(t, n - HEAD_DIM // 2, 1), pltpu.roll(t, HEAD_DIM // 2, 1))
        return t * cos_ref[...] + partner * sin_ref[...]

    def in_seg(seg):
        return jnp.logical_and(j >= seg[0], j < seg[0] + seg[1])

    @pl.when(in_seg(SEG_U))
    def _():
        u_ref[...] = z().astype(BF16)

    @pl.when(in_seg(SEG_Q))
    def _():
        q_ref[...] = (rope(z()) * Q_SCALE).astype(BF16)

    @pl.when(in_seg(SEG_K))
    def _():
        k = rope(z())
        kf_ref[...] = k
        kb_ref[...] = k.astype(BF16)

    @pl.when(in_seg(SEG_V))
    def _():
        v = z()
        vf_ref[...] = v
        vb_ref[...] = v.astype(BF16)

    @pl.when(in_seg(SEG_GA))
    def _():
        ga_ref[...] = _sigmoid(z()).astype(BF16)

    @pl.when(in_seg(SEG_GB))
    def _():
        gb_ref[...] = _sigmoid(z()).astype(BF16)


def _inproj(x2d, g, w_bf, cos_t, sin_t, tm):
    T = x2d.shape[0]
    nj = IN_COLS // IN_TN
    n_tab = cos_t.shape[0] // tm

    def seg_map(seg):
        return lambda i, j: (i, jnp.clip(j - seg[0], 0, seg[1] - 1))

    def out(seg, dtype):
        return (jax.ShapeDtypeStruct((T, seg[1] * IN_TN), dtype), pl.BlockSpec((tm, IN_TN), seg_map(seg)))

    outs = [out(SEG_U, BF16), out(SEG_Q, BF16), out(SEG_K, F32), out(SEG_K, BF16),
            out(SEG_V, F32), out(SEG_V, BF16), out(SEG_GA, BF16), out(SEG_GB, BF16)]
    return pl.pallas_call(
        _inproj_kernel,
        grid=(T // tm, nj),
        in_specs=[
            pl.BlockSpec((tm, D_MODEL), lambda i, j: (i, 0)),
            pl.BlockSpec((1, D_MODEL), lambda i, j: (0, 0)),
            pl.BlockSpec((D_MODEL, IN_TN), lambda i, j: (0, j)),
            pl.BlockSpec((tm, IN_TN), lambda i, j: (i % n_tab, 0)),
            pl.BlockSpec((tm, IN_TN), lambda i, j: (i % n_tab, 0)),
        ],
        out_specs=[o[1] for o in outs],
        out_shape=[o[0] for o in outs],
        scratch_shapes=[pltpu.VMEM((tm, D_MODEL), BF16)],
        compiler_params=_cparams("parallel", "arbitrary"),
        name="inproj",
    )(x2d, g, w_bf, cos_t, sin_t)


def _rope_tables(seq, past, rows):
    half = HEAD_DIM // 2
    inv = ROPE_THETA ** (-jnp.arange(half, dtype=F32) * 2.0 / HEAD_DIM)
    pos = (past + (jnp.arange(rows, dtype=jnp.int32) % seq)).astype(F32)
    ang = pos[:, None] * inv[None, :]
    cos, sin = jnp.cos(ang), jnp.sin(ang)
    reps = IN_TN // HEAD_DIM
    cos_t = jnp.tile(jnp.concatenate([cos, cos], axis=-1), (1, reps))
    sin_t = jnp.tile(jnp.concatenate([-sin, sin], axis=-1), (1, reps))
    return cos_t, sin_t


S5_COLS = 512
S5_UBLK = LANES
S5_N_UBLK = D_SSM // S5_UBLK
S5_XBLK = (S5_UBLK // SSM_GROUP) * SSM_STATE
S5_YBLK = 256
S5_N_YBLK = D_SSM // S5_YBLK
S5_HBLK = (S5_YBLK // SSM_GROUP) * SSM_STATE


def _s5_kernel(u_ref, h0_ref, perm_ref, permt_ref, wx_ref, apow_ref, cr_ref, ci_ref, d_ref,
               y_ref, hfin_ref, x_scr, c_scr, carry_scr, *, seg_len):
    t = pl.program_id(1)
    rows = SUBLANES * seg_len
    n = N_STATE

    @pl.when(t == 0)
    def _():
        carry_scr[...] = h0_ref[...]

    up = _dot(perm_ref[...], u_ref[...])
    upb = up.astype(BF16)
    for a in range(S5_N_UBLK):
        xa = _dot(upb[:, a * S5_UBLK:(a + 1) * S5_UBLK], wx_ref[a])
        x_scr[:, a * S5_XBLK:(a + 1) * S5_XBLK] = xa[:, :S5_XBLK]
        x_scr[:, n + a * S5_XBLK:n + (a + 1) * S5_XBLK] = xa[:, S5_XBLK:]

    for c in range(n // S5_COLS):
        lo = c * S5_COLS
        ar = jnp.broadcast_to(apow_ref[0:1, lo:lo + S5_COLS], (SUBLANES, S5_COLS))
        ai = jnp.broadcast_to(apow_ref[0:1, n + lo:n + lo + S5_COLS], (SUBLANES, S5_COLS))

        def scan_step(k, carry, lo=lo, ar=ar, ai=ai):
            hr, hi = carry
            r0 = pl.multiple_of(k * SUBLANES, SUBLANES)
            xr = x_scr[pl.ds(r0, SUBLANES), lo:lo + S5_COLS]
            xi = x_scr[pl.ds(r0, SUBLANES), n + lo:n + lo + S5_COLS]
            nr = ar * hr - ai * hi + xr
            ni = ar * hi + ai * hr + xi
            x_scr[pl.ds(r0, SUBLANES), lo:lo + S5_COLS] = nr
            x_scr[pl.ds(r0, SUBLANES), n + lo:n + lo + S5_COLS] = ni
            return nr, ni

        zero = jnp.zeros((SUBLANES, S5_COLS), F32)
        lax.fori_loop(0, seg_len, scan_step, (zero, zero))

    alr = apow_ref[seg_len - 1:seg_len, :n]
    ali = apow_ref[seg_len - 1:seg_len, n:]
    cr = carry_scr[:, :n]
    ci = carry_scr[:, n:]
    for s in range(SUBLANES):
        c_scr[s:s + 1, :n] = cr
        c_scr[s:s + 1, n:] = ci
        lr = x_scr[rows - SUBLANES + s:rows - SUBLANES + s + 1, :n]
        li = x_scr[rows - SUBLANES + s:rows - SUBLANES + s + 1, n:]
        cr, ci = alr * cr - ali * ci + lr, alr * ci + ali * cr + li
    carry_scr[:, :n] = cr
    carry_scr[:, n:] = ci

    for c in range(n // S5_COLS):
        lo = c * S5_COLS
        sr = c_scr[:, lo:lo + S5_COLS]
        si = c_scr[:, n + lo:n + lo + S5_COLS]

        def fix_step(k, _, lo=lo, sr=sr, si=si):
            r0 = pl.multiple_of(k * SUBLANES, SUBLANES)
            pr = apow_ref[pl.ds(k, 1), lo:lo + S5_COLS]
            pi = apow_ref[pl.ds(k, 1), n + lo:n + lo + S5_COLS]
            x_scr[pl.ds(r0, SUBLANES), lo:lo + S5_COLS] += pr * sr - pi * si
            x_scr[pl.ds(r0, SUBLANES), n + lo:n + lo + S5_COLS] += pr * si + pi * sr
            return 0

        lax.fori_loop(0, seg_len, fix_step, 0)

    @pl.when(t == pl.num_programs(1) - 1)
    def _():
        hfin_ref[...] = carry_scr[...]

    for j in range(S5_N_YBLK):
        hr = x_scr[:, j * S5_HBLK:(j + 1) * S5_HBLK].astype(BF16)
        hi = x_scr[:, n + j * S5_HBLK:n + (j + 1) * S5_HBLK].astype(BF16)
        yj = (_dot(hr, cr_ref[j]) + _dot(hi, ci_ref[j])
              + d_ref[:, j * S5_YBLK:(j + 1) * S5_YBLK] * up[:, j * S5_YBLK:(j + 1) * S5_YBLK])
        yj = _gelu(yj).astype(BF16)
        y_ref[:, j * S5_YBLK:(j + 1) * S5_YBLK] = _dot(permt_ref[...], yj).astype(BF16)


def _s5_tables(lp, seg_len):
    g, p, c = N_SSM_GROUPS, SSM_STATE, SSM_GROUP
    dt = jnp.exp(lp['log_step'].astype(F32))[:, None]
    lr, li = lp['lam_re'].astype(F32), lp['lam_im'].astype(F32)
    mag = jnp.exp(lr * dt)
    ar, ai = mag * jnp.cos(li * dt), mag * jnp.sin(li * dt)
    den = lr * lr + li * li
    fr = ((ar - 1.0) * lr + ai * li) / den
    fi = (ai * lr - (ar - 1.0) * li) / den
    br, bi = lp['b_re'].astype(F32), lp['b_im'].astype(F32)
    bbr = fr[..., None] * br - fi[..., None] * bi
    bbi = fr[..., None] * bi + fi[..., None] * br

    gpb = S5_UBLK // c
    eye = jnp.eye(gpb, dtype=F32)

    def in_blocks(bb):
        bb = bb.reshape(S5_N_UBLK, gpb, p, c)
        return jnp.einsum('xy,axpc->axcyp', eye, bb).reshape(S5_N_UBLK, S5_UBLK, S5_XBLK)

    wx = jnp.concatenate([in_blocks(bbr), in_blocks(bbi)], axis=-1).astype(BF16)

    pr, pi = ar.reshape(1, g * p), ai.reshape(1, g * p)
    while pr.shape[0] < seg_len:
        tr, ti = pr[-1:], pi[-1:]
        pr, pi = (jnp.concatenate([pr, pr * tr - pi * ti], axis=0),
                  jnp.concatenate([pi, pr * ti + pi * tr], axis=0))
    apow = jnp.concatenate([pr[:seg_len], pi[:seg_len]], axis=-1)

    gpy = S5_YBLK // c
    eye_y = jnp.eye(gpy, dtype=F32)

    def out_blocks(cm):
        cm = cm.astype(F32).reshape(S5_N_YBLK, gpy, c, p)
        return jnp.einsum('xy,jxcp->jxpyc', eye_y, cm).reshape(S5_N_YBLK, S5_HBLK, S5_YBLK)

    cr = out_blocks(lp['c_re']).astype(BF16)
    ci = (-out_blocks(lp['c_im'])).astype(BF16)
    d = lp['d'].astype(F32).reshape(1, D_SSM)
    return wx, apow, cr, ci, d


def _s5(u, h0, tables, seg_len):
    bn, seq, _ = u.shape
    rows = SUBLANES * seg_len
    wx, apow, cr, ci, d = tables
    r = jnp.arange(rows)
    perm = (r[None, :] == ((r % SUBLANES) * seg_len + r // SUBLANES)[:, None]).astype(BF16)
    const2 = lambda b, t: (0, 0)
    const3 = lambda b, t: (0, 0, 0)
    return pl.pallas_call(
        functools.partial(_s5_kernel, seg_len=seg_len),
        grid=(bn, seq // rows),
        in_specs=[
            pl.BlockSpec((None, rows, D_SSM), lambda b, t: (b, t, 0)),
            pl.BlockSpec((None, 1, 2 * N_STATE), lambda b, t: (b, 0, 0)),
            pl.BlockSpec((rows, rows), const2),
            pl.BlockSpec((rows, rows), const2),
            pl.BlockSpec(wx.shape, const3),
            pl.BlockSpec(apow.shape, const2),
            pl.BlockSpec(cr.shape, const3),
            pl.BlockSpec(ci.shape, const3),
            pl.BlockSpec((1, D_SSM), const2),
        ],
        out_specs=[
            pl.BlockSpec((None, rows, D_SSM), lambda b, t: (b, t, 0)),
            pl.BlockSpec((None, 1, 2 * N_STATE), lambda b, t: (b, 0, 0)),
        ],
        out_shape=[jax.ShapeDtypeStruct((bn, seq, D_SSM), BF16),
                   jax.ShapeDtypeStruct((bn, 1, 2 * N_STATE), F32)],
        scratch_shapes=[pltpu.VMEM((rows, 2 * N_STATE), F32),
                        pltpu.VMEM((SUBLANES, 2 * N_STATE), F32),
                        pltpu.VMEM((1, 2 * N_STATE), F32)],
        compiler_params=_cparams("parallel", "arbitrary"),
        name="s5",
    )(u, h0, perm, perm.T, wx, apow, cr, ci, d)


GLU_TN = 512


def _glu_kernel(y_ref, wa_ref, wb_ref, ga_ref, o_ref):
    y = y_ref[...]
    a = _dot(y, wa_ref[...])
    b = _dot(y, wb_ref[...])
    o_ref[...] = (ga_ref[...].astype(F32) * a * _sigmoid(b)).astype(BF16)


def _glu(y, w_glu_bf, sig_a, tm):
    T = y.shape[0]
    nj = D_MODEL // GLU_TN
    return pl.pallas_call(
        _glu_kernel,
        grid=(T // tm, nj),
        in_specs=[
            pl.BlockSpec((tm, D_SSM), lambda i, j: (i, 0)),
            pl.BlockSpec((D_SSM, GLU_TN), lambda i, j: (0, j)),
            pl.BlockSpec((D_SSM, GLU_TN), lambda i, j: (0, j + nj)),
            pl.BlockSpec((tm, GLU_TN), lambda i, j: (i, j)),
        ],
        out_specs=pl.BlockSpec((tm, GLU_TN), lambda i, j: (i, j)),
        out_shape=jax.ShapeDtypeStruct((T, D_MODEL), BF16),
        compiler_params=_cparams("parallel", "arbitrary"),
        name="glu",
    )(y, w_glu_bf, w_glu_bf, sig_a)


def _attn_kernel(q_ref, k_ref, v_ref, lq1_ref, lk1_ref, lq2_ref, lk2_ref, g_ref, o_ref,
                 qt_scr, s0_scr, s1_scr, m_scr, l_scr, acc_scr, *, tq, tk, past, paired, lambda_init):
    i = pl.program_id(2)
    nk = k_ref.shape[0] // tk
    q = q_ref[...].astype(F32)
    lane = lax.broadcasted_iota(jnp.int32, q.shape, 1)
    qs = jnp.concatenate([jnp.where(lane < HEAD_DIM, q, 0.0), jnp.where(lane >= HEAD_DIM, q, 0.0)], axis=0)
    qt_scr[...] = qs.T.astype(BF16)

    m_scr[...] = jnp.full(m_scr.shape, NEG_INF, F32)
    l_scr[...] = jnp.zeros(l_scr.shape, F32)
    acc_scr[...] = jnp.zeros(acc_scr.shape, F32)

    q_lo = past + i * tq
    min_qc = q_lo // CHUNK
    max_qc = (q_lo + tq - 1) // CHUNK
    n_proc = jnp.minimum(nk, (max_qc * CHUNK + CHUNK - 1) // tk + 1)
    n_full = jnp.clip((min_qc * CHUNK + CHUNK) // tk, 0, n_proc)

    def scores(j, s_ref):
        k0 = pl.multiple_of(j * tk, tk)
        s_ref[...] = _dot(k_ref[pl.ds(k0, tk), :], qt_scr[...])

    def update(j, s_ref, masked):
        k0 = pl.multiple_of(j * tk, tk)
        s = s_ref[...]
        if masked:
            kpos = k0 + lax.broadcasted_iota(jnp.int32, s.shape, 0)
            qpos = q_lo + jnp.bitwise_and(lax.broadcasted_iota(jnp.int32, s.shape, 1), tq - 1)
            s = jnp.where(jnp.right_shift(kpos, CHUNK_SHIFT) <= jnp.right_shift(qpos, CHUNK_SHIFT), s, NEG_INF)
        m_old = m_scr[...]
        m_new = jnp.maximum(m_old, jnp.max(s, axis=0, keepdims=True))
        alpha = jnp.exp2(m_old - m_new)
        p = jnp.exp2(s - m_new)
        l_scr[...] = alpha * l_scr[...] + jnp.sum(p, axis=0, keepdims=True)
        pv = lax.dot_general(v_ref[pl.ds(k0, tk), :], p.astype(BF16), (((0,), (0,)), ((), ())),
                             preferred_element_type=F32)
        acc_scr[...] = alpha * acc_scr[...] + pv
        m_scr[...] = m_new

    if paired:
        scores(0, s0_scr)

        def pair(p, c):
            scores(2 * p + 1, s1_scr)
            update(2 * p, s0_scr, False)
            scores(2 * p + 2, s0_scr)
            update(2 * p + 1, s1_scr, False)
            return c

        lax.fori_loop(0, i, pair, 0)
        scores(2 * i + 1, s1_scr)
        update(2 * i, s0_scr, True)
        update(2 * i + 1, s1_scr, True)
    else:
        def full_body(j, c):
            scores(j, s0_scr)
            update(j, s0_scr, False)
            return c

        def masked_body(j, c):
            scores(j, s0_scr)
            update(j, s0_scr, True)
            return c

        lax.fori_loop(0, n_full, full_body, 0)
        lax.fori_loop(n_full, n_proc, masked_body, 0)

    lam = (jnp.exp(jnp.sum(lq1_ref[...] * lk1_ref[...], axis=-1, keepdims=True))
           - jnp.exp(jnp.sum(lq2_ref[...] * lk2_ref[...], axis=-1, keepdims=True)) + lambda_init)
    ot = acc_scr[:, :tq] / l_scr[:, :tq] - lam * (acc_scr[:, tq:] / l_scr[:, tq:])
    o_ref[...] = (_rms(ot.T, g_ref[...]) * (1.0 - lambda_init)).astype(BF16)


def _attention(q, k, v, lp, tq, tk, past, lambda_init):
    bn, seq, _ = q.shape
    lk = k.shape[1]
    vec = lambda a: a.astype(F32).reshape(1, -1)
    small = lambda n: pl.BlockSpec((1, n), lambda b, h, i: (0, 0))
    paired = past == 0 and tq == 2 * tk and lk == seq
    s_scr = pltpu.VMEM((tk, 2 * tq), F32)
    return pl.pallas_call(
        functools.partial(_attn_kernel, tq=tq, tk=tk, past=past, paired=paired, lambda_init=lambda_init),
        grid=(bn, N_HEADS, seq // tq),
        in_specs=[
            pl.BlockSpec((None, tq, V_DIM), lambda b, h, i: (b, i, h)),
            pl.BlockSpec((None, lk, V_DIM), lambda b, h, i: (b, 0, h)),
            pl.BlockSpec((None, lk, V_DIM), lambda b, h, i: (b, 0, h)),
            small(HEAD_DIM), small(HEAD_DIM), small(HEAD_DIM), small(HEAD_DIM), small(V_DIM),
        ],
        out_specs=pl.BlockSpec((None, tq, V_DIM), lambda b, h, i: (b, i, h)),
        out_shape=jax.ShapeDtypeStruct((bn, seq, V_COLS), BF16),
        scratch_shapes=[pltpu.VMEM((V_DIM, 2 * tq), BF16), s_scr, s_scr, pltpu.VMEM((1, 2 * tq), F32),
                        pltpu.VMEM((1, 2 * tq), F32), pltpu.VMEM((V_DIM, 2 * tq), F32)],
        compiler_params=_cparams("parallel", "parallel", "arbitrary"),
        name="diff_attn",
    )(q, k, v, vec(lp['lq1']), vec(lp['lk1']), vec(lp['lq2']), vec(lp['lk2']), vec(lp['g_subln']))


def _merge_kernel(x_ref, o_ref, ga_ref, sb_ref, wa_ref, wo_ref, g_ref, x1_ref, h2t_ref):
    branch_b = _dot(o_ref[...], wa_ref[...])
    merged = ga_ref[...].astype(F32) + sb_ref[...].astype(F32) * branch_b
    x1 = x_ref[...] + _dot(merged.astype(BF16), wo_ref[...])
    x1_ref[...] = x1
    h2t_ref[...] = _rms(x1, g_ref[...]).T.astype(BF16)


def _merge(x2d, o, gated_a, sig_b, wa_bf, wo_bf, g_ffn, tm):
    T = x2d.shape[0]
    row = lambda n: pl.BlockSpec((tm, n), lambda i: (i, 0))
    const = lambda s: pl.BlockSpec(s, lambda i: (0, 0))
    return pl.pallas_call(
        _merge_kernel,
        grid=(T // tm,),
        in_specs=[row(D_MODEL), row(V_COLS), row(D_MODEL), row(D_MODEL),
                  const((V_COLS, D_MODEL)), const((D_MODEL, D_MODEL)), const((1, D_MODEL))],
        out_specs=[row(D_MODEL), pl.BlockSpec((D_MODEL, tm), lambda i: (0, i))],
        out_shape=[jax.ShapeDtypeStruct((T, D_MODEL), F32), jax.ShapeDtypeStruct((D_MODEL, T), BF16)],
        compiler_params=_cparams("parallel"),
        name="merge_out",
    )(x2d, o, gated_a, sig_b, wa_bf, wo_bf, g_ffn)


ROUTE_LC = 256
TAKEN = -3.0e38


def _split_bf16(x):
    hi = x.astype(BF16)
    return hi, (x - hi.astype(F32)).astype(BF16)


def _oddeven_merge(lo, hi, r):
    step = r * 2
    if step < hi - lo:
        yield from _oddeven_merge(lo, hi, step)
        yield from _oddeven_merge(lo + r, hi, step)
        yield from [(i, i + r) for i in range(lo + r, hi - r, step)]
    else:
        yield (lo, lo + r)


def _oddeven_merge_sort(lo, hi):
    if hi - lo >= 1:
        mid = lo + (hi - lo) // 2
        yield from _oddeven_merge_sort(lo, mid)
        yield from _oddeven_merge_sort(mid + 1, hi)
        yield from _oddeven_merge(lo, hi, 1)


def _top_rows_sorted(s, k):
    n = s.shape[0] // SUBLANES
    rows = [s[SUBLANES * g:SUBLANES * (g + 1), :] for g in range(n)]
    n_pow2 = 1 << (n - 1).bit_length()
    for i, j in _oddeven_merge_sort(0, n_pow2 - 1):
        if j < n:
            rows[i], rows[j] = jnp.maximum(rows[i], rows[j]), jnp.minimum(rows[i], rows[j])
    taken = jnp.full_like(rows[0], TAKEN)
    out = []
    for r in range(k):
        m = jnp.max(rows[0], axis=0, keepdims=True)
        out.append(m)
        hit = rows[0] == m
        depth = min(n, k - r - 1)
        for d in range(depth):
            rows[d] = jnp.where(hit, rows[d + 1] if d + 1 < n else taken, rows[d])
    return out


def _stack_rows(rows):
    n, w = len(rows), rows[0].shape[1]
    idx = lax.broadcasted_iota(jnp.int32, (n, w), 0)
    out = jnp.zeros((n, w), F32)
    for r, v in enumerate(rows):
        out = jnp.where(idx == r, v, out)
    return out


def _route_kernel(h2t_ref, wq_ref, keys_ref, thr_ref, s2_ref, e1_ref, e2_ref, s1_scr):
    tt = h2t_ref.shape[1]
    n_top = PEER_TOPK + 1
    qt = _dot(wq_ref[...], h2t_ref[...])
    halves = []
    for c in range(2):
        q_hi, q_lo = _split_bf16(qt[c * PEER_HALF:(c + 1) * PEER_HALF, :])
        k_hi, k_lo = _split_bf16(keys_ref[0, c])
        halves.append(_dot(k_hi, q_hi) + _dot(k_hi, q_lo) + _dot(k_lo, q_hi))
    s1_scr[...] = halves[0]
    s2_ref[0] = halves[1]

    width = min(tt, ROUTE_LC)
    for lc in range(tt // width):
        sl = slice(lc * width, (lc + 1) * width)
        s1 = s1_scr[:, sl]
        s2 = s2_ref[0, :, sl]
        a = _top_rows_sorted(s1, n_top)
        b = _top_rows_sorted(s2, n_top)
        taken = jnp.full_like(b[0], TAKEN)
        b_all = _stack_rows(b + [taken] * (3 * SUBLANES - n_top))
        tail = [a[i] + b[j] for i in range(4, n_top) for j in range(n_top // (i + 1))]
        tail += [taken] * (-len(tail) % SUBLANES)
        cand = jnp.concatenate([a[0] + b_all] + [a[i] + b_all[:SUBLANES] for i in range(1, 4)]
                               + [_stack_rows(tail)], axis=0)
        top = _top_rows_sorted(cand, n_top)
        z = jnp.zeros_like(top[0])
        for r in range(PEER_TOPK):
            z = z + jnp.exp(top[r] - top[0])
        tau = 0.5 * (top[PEER_TOPK - 1] + top[PEER_TOPK])
        grouped = (N_KEYS // SUBLANES, SUBLANES, width)
        thr_ref[0, :, :, sl] = (tau - s1).reshape(grouped)
        e1_ref[0, :, :, sl] = (jnp.exp(s1 - a[0]) * (0.5 / z)).reshape(grouped)
        e2_ref[0, :, sl] = jnp.exp(s2 - b[0])


def _route(h2t, wq_t_bf, keys, tt):
    T = h2t.shape[1]
    arr = jax.ShapeDtypeStruct((PEER_HEADS, N_KEYS, T), F32)
    spec = pl.BlockSpec((1, N_KEYS, tt), lambda i, h: (h, 0, i))
    n_grp = N_KEYS // SUBLANES
    garr = jax.ShapeDtypeStruct((PEER_HEADS, n_grp, SUBLANES, T), F32)
    gspec = pl.BlockSpec((1, n_grp, SUBLANES, tt), lambda i, h: (h, 0, 0, i))
    return pl.pallas_call(
        _route_kernel,
        grid=(T // tt, PEER_HEADS),
        in_specs=[
            pl.BlockSpec((D_MODEL, tt), lambda i, h: (0, i)),
            pl.BlockSpec((PEER_QUERY, D_MODEL), lambda i, h: (h, 0)),
            pl.BlockSpec((1, 2, N_KEYS, PEER_HALF), lambda i, h: (h, 0, 0, 0)),
        ],
        out_specs=[gspec, spec, gspec, spec],
        out_shape=[garr, arr, garr, arr],
        scratch_shapes=[pltpu.VMEM((N_KEYS, tt), F32)],
        compiler_params=_cparams("parallel", "arbitrary"),
        name="peer_route",
    )(h2t, wq_t_bf, keys)


PEER_SUB = 512
PEER_EB = 2 * PEER_SUB
PEER_ROWS = PEER_SUB // N_KEYS
N_SUB = N_EXPERTS // PEER_SUB
PEER_JH = 64


def _peer_kernel(h2t_ref, thr_ref, s2_ref, e1_ref, e2_ref, u_ref, v_ref, o_ref,
                 act0, act1, coef0, coef1, acc_scr):
    g = pl.program_id(1)
    tt = h2t_ref.shape[1]

    @pl.when(g == 0)
    def _():
        act1[...] = jnp.zeros(act1.shape, F32)
        coef0[...] = jnp.zeros(coef0.shape, F32)
        coef1[...] = jnp.zeros(coef1.shape, F32)
        acc_scr[...] = jnp.zeros(acc_scr.shape, F32)

    def stage_a(half, act):
        act[...] = _dot(u_ref[half * PEER_SUB:(half + 1) * PEER_SUB, :], h2t_ref[...])

    def stage_b(b, act, coef):
        b = jnp.clip(b, 0, N_SUB - 1)
        grp = b // 2
        odd = (b % 2) == 1
        for lc in range(tt // LANES):
            ln = slice(lc * LANES, (lc + 1) * LANES)
            for jh in range(N_KEYS // PEER_JH):
                js = slice(jh * PEER_JH, (jh + 1) * PEER_JH)
                gates = [jnp.zeros((PEER_JH, LANES), F32) for _ in range(PEER_ROWS)]
                for h in range(PEER_HEADS):
                    s2 = s2_ref[h, js, ln]
                    e2 = e2_ref[h, js, ln]
                    thr_grp = thr_ref[h, grp, :, ln]
                    e1_grp = e1_ref[h, grp, :, ln]
                    thr_rows = jnp.where(odd, thr_grp[PEER_ROWS:], thr_grp[:PEER_ROWS])
                    e1_rows = jnp.where(odd, e1_grp[PEER_ROWS:], e1_grp[:PEER_ROWS])
                    for r in range(PEER_ROWS):
                        gates[r] = gates[r] + jnp.where(s2 >= thr_rows[r:r + 1], e1_rows[r:r + 1] * e2, 0.0)
                for r in range(PEER_ROWS):
                    rows = slice(r * N_KEYS + jh * PEER_JH, r * N_KEYS + (jh + 1) * PEER_JH)
                    x = act[rows, ln]
                    coef[rows, ln] = gates[r] * (x * (1.0 + lax.erf(x * (2.0 ** -0.5))))

    def stage_c():
        coef = jnp.concatenate([coef0[...].astype(BF16), coef1[...].astype(BF16)], axis=0)
        acc_scr[...] += lax.dot_general(coef, v_ref[...], (((0,), (0,)), ((), ())), preferred_element_type=F32)

    stage_a(0, act0)
    stage_b(2 * g - 1, act1, coef1)
    stage_a(1, act1)
    stage_c()
    stage_b(2 * g, act0, coef0)

    @pl.when(g == pl.num_programs(1) - 1)
    def _():
        o_ref[...] = acc_scr[...].astype(BF16)


def _peer(h2t, route, u_bf, v_bf, tt):
    T = h2t.shape[1]
    thr, s2, e1, e2 = route
    n_eb = N_EXPERTS // PEER_EB
    spec = pl.BlockSpec((PEER_HEADS, N_KEYS, tt), lambda i, g: (0, 0, i))
    gspec = pl.BlockSpec((PEER_HEADS, N_KEYS // SUBLANES, SUBLANES, tt), lambda i, g: (0, 0, 0, i))
    assert SUBLANES == 2 * PEER_ROWS
    return pl.pallas_call(
        _peer_kernel,
        grid=(T // tt, n_eb + 1),
        in_specs=[
            pl.BlockSpec((D_MODEL, tt), lambda i, g: (0, i)),
            gspec, spec, gspec, spec,
            pl.BlockSpec((PEER_EB, D_MODEL), lambda i, g: (jnp.minimum(g, n_eb - 1), 0)),
            pl.BlockSpec((PEER_EB, D_MODEL), lambda i, g: (jnp.maximum(g - 1, 0), 0)),
        ],
        out_specs=pl.BlockSpec((tt, D_MODEL), lambda i, g: (i, 0)),
        out_shape=jax.ShapeDtypeStruct((T, D_MODEL), BF16),
        scratch_shapes=[pltpu.VMEM((PEER_SUB, tt), F32), pltpu.VMEM((PEER_SUB, tt), F32),
                        pltpu.VMEM((PEER_SUB, tt), F32), pltpu.VMEM((PEER_SUB, tt), F32),
                        pltpu.VMEM((tt, D_MODEL), F32)],
        compiler_params=_cparams("parallel", "arbitrary"),
        name="peer_dense",
    )(h2t, thr, s2, e1, e2, u_bf, v_bf)


def _ple_kernel(x1_ref, po_ref, p_ref, wg_ref, wp_ref, gp_ref, gf_ref, y_ref):
    x2 = x1_ref[...] + po_ref[...].astype(F32)
    h3 = _rms(x2, gp_ref[...]).astype(BF16)
    gate = _sigmoid(_dot(h3, wg_ref[...]))
    proj = _dot(p_ref[...].astype(BF16), wp_ref[...])
    y_ref[...] = _rms(x2 + proj * gate, gf_ref[...])


def _ple(x1, peer_out, p2d, wg_bf, wp_bf, g_ple, g_final, tm):
    T = x1.shape[0]
    row = lambda n: pl.BlockSpec((tm, n), lambda i: (i, 0))
    const = lambda s: pl.BlockSpec(s, lambda i: (0, 0))
    return pl.pallas_call(
        _ple_kernel,
        grid=(T // tm,),
        in_specs=[row(D_MODEL), row(D_MODEL), row(PLE_DIM),
                  const((D_MODEL, D_MODEL)), const((PLE_DIM, D_MODEL)), const((1, D_MODEL)), const((1, D_MODEL))],
        out_specs=row(D_MODEL),
        out_shape=jax.ShapeDtypeStruct((T, D_MODEL), F32),
        compiler_params=_cparams("parallel"),
        name="ple_final",
    )(x1, peer_out, p2d, wg_bf, wp_bf, g_ple, g_final)


def _tile(n, pref):
    t = min(n, pref)
    assert n % t == 0, (n, t)
    return t


def _layer(x, p, h0_re, h0_im, k_past, v_past, lp, lambda_init):
    bn, seq, _ = x.shape
    past = k_past.shape[1]
    T = bn * seq
    x2d = x.reshape(T, D_MODEL)
    row = lambda a: a.astype(F32).reshape(1, -1)

    tm_in = _tile(T, 1024)
    cos_t, sin_t = _rope_tables(seq, past, max(seq, tm_in))
    u, q, k_f32, k_bf, v_f32, v_bf, sig_a, sig_b = _inproj(
        x2d, row(lp['g_mix']), lp['w_in'].astype(BF16), cos_t, sin_t, tm_in)

    seg_len = _tile(seq // SUBLANES, 32)
    h0 = jnp.concatenate([h0_re.reshape(bn, 1, N_STATE), h0_im.reshape(bn, 1, N_STATE)], axis=-1).astype(F32)
    y_ssm, h_fin = _s5(u.reshape(bn, seq, D_SSM), h0, _s5_tables(lp, seg_len), seg_len)
    gated_a = _glu(y_ssm.reshape(T, D_SSM), lp['w_glu'].astype(BF16), sig_a, _tile(T, 1024))

    k_all = jnp.concatenate([k_past.reshape(bn, past, Q_COLS).astype(BF16), k_bf.reshape(bn, seq, Q_COLS)], axis=1)
    v_all = jnp.concatenate([v_past.reshape(bn, past, V_COLS).astype(BF16), v_bf.reshape(bn, seq, V_COLS)], axis=1)
    seq_q = max(seq, LANES // 2)
    q3 = jnp.pad(q.reshape(bn, seq, Q_COLS), ((0, 0), (0, seq_q - seq), (0, 0)))
    tq = _tile(seq_q, 512)
    tk = tq // 2 if past == 0 else past + seq
    o = _attention(q3, k_all, v_all, lp, tq, tk, past, lambda_init)[:, :seq]

    x1, h2t = _merge(x2d, o.reshape(T, V_COLS), gated_a, sig_b, lp['w_attn_out'].astype(BF16),
                    lp['w_out'].astype(BF16), row(lp['g_ffn']), _tile(T, 256))

    tt = _tile(T, 512)
    route = _route(h2t, lp['peer_w_q'].T.astype(BF16), lp['peer_keys'].astype(F32), tt)
    peer_out = _peer(h2t, route, lp['peer_u'].astype(BF16), lp['peer_v'].astype(BF16), tt)

    y = _ple(x1, peer_out, p.reshape(T, PLE_DIM), lp['w_ple_gate'].astype(BF16), lp['w_ple_proj'].astype(BF16),
             row(lp['g_ple']), row(lp['g_final']), _tile(T, 256))

    new_k = k_f32.reshape(bn, seq, N_HEADS, 2 * HEAD_DIM)
    new_v = v_f32.reshape(bn, seq, N_HEADS, V_DIM)
    hr = h_fin[:, 0, :N_STATE].reshape(bn, N_SSM_GROUPS, SSM_STATE)
    hi = h_fin[:, 0, N_STATE:].reshape(bn, N_SSM_GROUPS, SSM_STATE)
    return y.reshape(bn, seq, D_MODEL), new_k, new_v, hr, hi


def _trunk(x, p, h0_re, h0_im, k_past, v_past, lp):
    assert DEPTH == 1
    lambda_init = 0.8 - 0.6 * math.exp(-0.3 * 0)
    y, k_new, v_new, hr, hi = _layer(x, p[0], h0_re[0], h0_im[0], k_past[0], v_past[0], lp, lambda_init)
    return y, k_new[None], v_new[None], hr[None], hi[None]


def kernel(x_prompt, x_sample, p_prompt, p_sample, cache_k, cache_v, state_ssm_re, state_ssm_im,
           g_mix_norm, w_in, ssm_lambda_re, ssm_lambda_im, ssm_log_step, ssm_b_re, ssm_b_im,
           ssm_c_re, ssm_c_im, ssm_d, w_glu, diff_lambda_q1, diff_lambda_k1, diff_lambda_q2,
           diff_lambda_k2, g_subln, w_attn_out, w_out, g_ffn_norm, peer_w_q, peer_keys, peer_u,
           peer_v, g_ple_norm, w_ple_gate, w_ple_proj, g_final):
    lp = dict(g_mix=g_mix_norm[0], w_in=w_in[0], lam_re=ssm_lambda_re[0], lam_im=ssm_lambda_im[0],
              log_step=ssm_log_step[0], b_re=ssm_b_re[0], b_im=ssm_b_im[0], c_re=ssm_c_re[0],
              c_im=ssm_c_im[0], d=ssm_d[0], w_glu=w_glu[0], lq1=diff_lambda_q1[0],
              lk1=diff_lambda_k1[0], lq2=diff_lambda_q2[0], lk2=diff_lambda_k2[0],
              g_subln=g_subln[0], w_attn_out=w_attn_out[0], w_out=w_out[0], g_ffn=g_ffn_norm[0],
              peer_w_q=peer_w_q[0], peer_keys=peer_keys[0], peer_u=peer_u[0], peer_v=peer_v[0],
              g_ple=g_ple_norm[0], w_ple_gate=w_ple_gate[0], w_ple_proj=w_ple_proj[0], g_final=g_final)
    bn = x_prompt.shape[0]
    zeros_state = jnp.zeros((DEPTH, bn, N_SSM_GROUPS, SSM_STATE), F32)
    k_none = jnp.zeros((DEPTH, bn, 0, N_HEADS, 2 * HEAD_DIM), x_prompt.dtype)
    v_none = jnp.zeros((DEPTH, bn, 0, N_HEADS, V_DIM), x_prompt.dtype)
    y_p, k_p, v_p, r_p, i_p = _trunk(x_prompt, p_prompt, zeros_state, zeros_state, k_none, v_none, lp)
    y_s, k_s, v_s, r_s, i_s = _trunk(x_sample, p_sample, state_ssm_re, state_ssm_im, cache_k, cache_v, lp)
    return (y_p, y_s, k_p, v_p, r_p, i_p, k_s, v_s, r_s, i_s)
```

```python
import functools
import math

import jax
import jax.numpy as jnp
from jax import lax
from jax.experimental import pallas as pl
from jax.experimental.pallas import tpu as pltpu

F32 = jnp.float32
BF16 = jnp.bfloat16

D_MODEL = 2048
DEPTH = 1
CHUNK = 64
CHUNK_SHIFT = CHUNK.bit_length() - 1
assert 1 << CHUNK_SHIFT == CHUNK
PLE_DIM = 256
RMS_EPS = 1e-6
NEG_INF = -1e30
D_SSM = 1024
SSM_GROUP = 16
N_SSM_GROUPS = D_SSM // SSM_GROUP
SSM_STATE = 64
N_STATE = N_SSM_GROUPS * SSM_STATE
N_HEADS = 8
HEAD_DIM = 64
V_DIM = 2 * HEAD_DIM
ROPE_THETA = 10000.0
Q_COLS = N_HEADS * 2 * HEAD_DIM
V_COLS = N_HEADS * V_DIM
IN_COLS = D_SSM + 2 * Q_COLS + V_COLS + 2 * D_MODEL
PEER_HEADS = 8
N_KEYS = 128
N_EXPERTS = N_KEYS * N_KEYS
PEER_QUERY = 256
PEER_HALF = PEER_QUERY // 2
PEER_TOPK = 16

LANES = 128
SUBLANES = 8
VMEM_LIMIT = 56 * 1024 * 1024


def _cparams(*sem):
    return pltpu.CompilerParams(dimension_semantics=sem, vmem_limit_bytes=VMEM_LIMIT)


def _rms(x, g):
    return x * lax.rsqrt(jnp.mean(x * x, axis=-1, keepdims=True) + RMS_EPS) * g


def _gelu(x):
    return 0.5 * x * (1.0 + lax.erf(x * (2.0 ** -0.5)))


def _sigmoid(x):
    return 1.0 / (1.0 + jnp.exp(-x))


def _dot(a, b):
    return jnp.dot(a, b, preferred_element_type=F32)


IN_TN = 512
Q_SCALE = HEAD_DIM ** -0.5 * math.log2(math.e)
SEG_U = (0, D_SSM // IN_TN)
SEG_Q = (SEG_U[0] + SEG_U[1], Q_COLS // IN_TN)
SEG_K = (SEG_Q[0] + SEG_Q[1], Q_COLS // IN_TN)
SEG_V = (SEG_K[0] + SEG_K[1], V_COLS // IN_TN)
SEG_GA = (SEG_V[0] + SEG_V[1], D_MODEL // IN_TN)
SEG_GB = (SEG_GA[0] + SEG_GA[1], D_MODEL // IN_TN)


def _inproj_kernel(x_ref, g_ref, w_ref, cos_ref, sin_ref,
                   u_ref, q_ref, kf_ref, kb_ref, vf_ref, vb_ref, ga_ref, gb_ref, h_scr):
    j = pl.program_id(1)

    @pl.when(j == 0)
    def _():
        h_scr[...] = _rms(x_ref[...], g_ref[...]).astype(BF16)

    def z():
        return _dot(h_scr[...], w_ref[...])

    def rope(t):
        n = t.shape[-1]
        lane = lax.broadcasted_iota(jnp.int32, t.shape, 1)
        first = jnp.bitwise_and(lane, HEAD_DIM - 1) < (HEAD_DIM // 2)
        partner = jnp.where(first, pltpu.roll(t, n - HEAD_DIM // 2, 1), pltpu.roll(t, HEAD_DIM // 2, 1))
        return t * cos_ref[...] + partner * sin_ref[...]

    def in_seg(seg):
        return jnp.logical_and(j >= seg[0], j < seg[0] + seg[1])

    @pl.when(in_seg(SEG_U))
    def _():
        u_ref[...] = z().astype(BF16)

    @pl.when(in_seg(SEG_Q))
    def _():
        q_ref[...] = (rope(z()) * Q_SCALE).astype(BF16)

    @pl.when(in_seg(SEG_K))
    def _():
        k = rope(z())
        kf_ref[...] = k
        kb_ref[...] = k.astype(BF16)

    @pl.when(in_seg(SEG_V))
    def _():
        v = z()
        vf_ref[...] = v
        vb_ref[...] = v.astype(BF16)

    @pl.when(in_seg(SEG_GA))
    def _():
        ga_ref[...] = _sigmoid(z()).astype(BF16)

    @pl.when(in_seg(SEG_GB))
    def _():
        gb_ref[...] = _sigmoid(z()).astype(BF16)


def _inproj(x2d, g, w_bf, cos_t, sin_t, tm):
    T = x2d.shape[0]
    nj = IN_COLS // IN_TN
    n_tab = cos_t.shape[0] // tm

    def seg_map(seg):
        return lambda i, j: (i, jnp.clip(j - seg[0], 0, seg[1] - 1))

    def out(seg, dtype):
        return (jax.ShapeDtypeStruct((T, seg[1] * IN_TN), dtype), pl.BlockSpec((tm, IN_TN), seg_map(seg)))

    outs = [out(SEG_U, BF16), out(SEG_Q, BF16), out(SEG_K, F32), out(SEG_K, BF16),
            out(SEG_V, F32), out(SEG_V, BF16), out(SEG_GA, BF16), out(SEG_GB, BF16)]
    return pl.pallas_call(
        _inproj_kernel,
        grid=(T // tm, nj),
        in_specs=[
            pl.BlockSpec((tm, D_MODEL), lambda i, j: (i, 0)),
            pl.BlockSpec((1, D_MODEL), lambda i, j: (0, 0)),
            pl.BlockSpec((D_MODEL, IN_TN), lambda i, j: (0, j)),
            pl.BlockSpec((tm, IN_TN), lambda i, j: (i % n_tab, 0)),
            pl.BlockSpec((tm, IN_TN), lambda i, j: (i % n_tab, 0)),
        ],
        out_specs=[o[1] for o in outs],
        out_shape=[o[0] for o in outs],
        scratch_shapes=[pltpu.VMEM((tm, D_MODEL), BF16)],
        compiler_params=_cparams("parallel", "arbitrary"),
        name="inproj",
    )(x2d, g, w_bf, cos_t, sin_t)


def _rope_tables(seq, past, rows):
    half = HEAD_DIM // 2
    inv = ROPE_THETA ** (-jnp.arange(half, dtype=F32) * 2.0 / HEAD_DIM)
    pos = (past + (jnp.arange(rows, dtype=jnp.int32) % seq)).astype(F32)
    ang = pos[:, None] * inv[None, :]
    cos, sin = jnp.cos(ang), jnp.sin(ang)
    reps = IN_TN // HEAD_DIM
    cos_t = jnp.tile(jnp.concatenate([cos, cos], axis=-1), (1, reps))
    sin_t = jnp.tile(jnp.concatenate([-sin, sin], axis=-1), (1, reps))
    return cos_t, sin_t


S5_COLS = 512
S5_UBLK = LANES
S5_N_UBLK = D_SSM // S5_UBLK
S5_XBLK = (S5_UBLK // SSM_GROUP) * SSM_STATE
S5_YBLK = 256
S5_N_YBLK = D_SSM // S5_YBLK
S5_HBLK = (S5_YBLK // SSM_GROUP) * SSM_STATE


def _s5_kernel(u_ref, h0_ref, perm_ref, permt_ref, wx_ref, apow_ref, cr_ref, ci_ref, d_ref,
               y_ref, hfin_ref, x_scr, c_scr, carry_scr, *, seg_len):
    t = pl.program_id(1)
    rows = SUBLANES * seg_len
    n = N_STATE

    @pl.when(t == 0)
    def _():
        carry_scr[...] = h0_ref[...]

    up = _dot(perm_ref[...], u_ref[...])
    upb = up.astype(BF16)
    for a in range(S5_N_UBLK):
        xa = _dot(upb[:, a * S5_UBLK:(a + 1) * S5_UBLK], wx_ref[a])
        x_scr[:, a * S5_XBLK:(a + 1) * S5_XBLK] = xa[:, :S5_XBLK]
        x_scr[:, n + a * S5_XBLK:n + (a + 1) * S5_XBLK] = xa[:, S5_XBLK:]

    for c in range(n // S5_COLS):
        lo = c * S5_COLS
        ar = jnp.broadcast_to(apow_ref[0:1, lo:lo + S5_COLS], (SUBLANES, S5_COLS))
        ai = jnp.broadcast_to(apow_ref[0:1, n + lo:n + lo + S5_COLS], (SUBLANES, S5_COLS))

        def scan_step(k, carry, lo=lo, ar=ar, ai=ai):
            hr, hi = carry
            r0 = pl.multiple_of(k * SUBLANES, SUBLANES)
            xr = x_scr[pl.ds(r0, SUBLANES), lo:lo + S5_COLS]
            xi = x_scr[pl.ds(r0, SUBLANES), n + lo:n + lo + S5_COLS]
            nr = ar * hr - ai * hi + xr
            ni = ar * hi + ai * hr + xi
            x_scr[pl.ds(r0, SUBLANES), lo:lo + S5_COLS] = nr
            x_scr[pl.ds(r0, SUBLANES), n + lo:n + lo + S5_COLS] = ni
            return nr, ni

        zero = jnp.zeros((SUBLANES, S5_COLS), F32)
        lax.fori_loop(0, seg_len, scan_step, (zero, zero))

    alr = apow_ref[seg_len - 1:seg_len, :n]
    ali = apow_ref[seg_len - 1:seg_len, n:]
    cr = carry_scr[:, :n]
    ci = carry_scr[:, n:]
    for s in range(SUBLANES):
        c_scr[s:s + 1, :n] = cr
        c_scr[s:s + 1, n:] = ci
        lr = x_scr[rows - SUBLANES + s:rows - SUBLANES + s + 1, :n]
        li = x_scr[rows - SUBLANES + s:rows - SUBLANES + s + 1, n:]
        cr, ci = alr * cr - ali * ci + lr, alr * ci + ali * cr + li
    carry_scr[:, :n] = cr
    carry_scr[:, n:] = ci

    for c in range(n // S5_COLS):
        lo = c * S5_COLS
        sr = c_scr[:, lo:lo + S5_COLS]
        si = c_scr[:, n + lo:n + lo + S5_COLS]

        def fix_step(k, _, lo=lo, sr=sr, si=si):
            r0 = pl.multiple_of(k * SUBLANES, SUBLANES)
            pr = apow_ref[pl.ds(k, 1), lo:lo + S5_COLS]
            pi = apow_ref[pl.ds(k, 1), n + lo:n + lo + S5_COLS]
            x_scr[pl.ds(r0, SUBLANES), lo:lo + S5_COLS] += pr * sr - pi * si
            x_scr[pl.ds(r0, SUBLANES), n + lo:n + lo + S5_COLS] += pr * si + pi * sr
            return 0

        lax.fori_loop(0, seg_len, fix_step, 0)

    @pl.when(t == pl.num_programs(1) - 1)
    def _():
        hfin_ref[...] = carry_scr[...]

    for j in range(S5_N_YBLK):
        hr = x_scr[:, j * S5_HBLK:(j + 1) * S5_HBLK].astype(BF16)
        hi = x_scr[:, n + j * S5_HBLK:n + (j + 1) * S5_HBLK].astype(BF16)
        yj = (_dot(hr, cr_ref[j]) + _dot(hi, ci_ref[j])
              + d_ref[:, j * S5_YBLK:(j + 1) * S5_YBLK] * up[:, j * S5_YBLK:(j + 1) * S5_YBLK])
        yj = _gelu(yj).astype(BF16)
        y_ref[:, j * S5_YBLK:(j + 1) * S5_YBLK] = _dot(permt_ref[...], yj).astype(BF16)


def _s5_tables(lp, seg_len):
    g, p, c = N_SSM_GROUPS, SSM_STATE, SSM_GROUP
    dt = jnp.exp(lp['log_step'].astype(F32))[:, None]
    lr, li = lp['lam_re'].astype(F32), lp['lam_im'].astype(F32)
    mag = jnp.exp(lr * dt)
    ar, ai = mag * jnp.cos(li * dt), mag * jnp.sin(li * dt)
    den = lr * lr + li * li
    fr = ((ar - 1.0) * lr + ai * li) / den
    fi = (ai * lr - (ar - 1.0) * li) / den
    br, bi = lp['b_re'].astype(F32), lp['b_im'].astype(F32)
    bbr = fr[..., None] * br - fi[..., None] * bi
    bbi = fr[..., None] * bi + fi[..., None] * br

    gpb = S5_UBLK // c
    eye = jnp.eye(gpb, dtype=F32)

    def in_blocks(bb):
        bb = bb.reshape(S5_N_UBLK, gpb, p, c)
        return jnp.einsum('xy,axpc->axcyp', eye, bb).reshape(S5_N_UBLK, S5_UBLK, S5_XBLK)

    wx = jnp.concatenate([in_blocks(bbr), in_blocks(bbi)], axis=-1).astype(BF16)

    pr, pi = ar.reshape(1, g * p), ai.reshape(1, g * p)
    while pr.shape[0] < seg_len:
        tr, ti = pr[-1:], pi[-1:]
        pr, pi = (jnp.concatenate([pr, pr * tr - pi * ti], axis=0),
                  jnp.concatenate([pi, pr * ti + pi * tr], axis=0))
    apow = jnp.concatenate([pr[:seg_len], pi[:seg_len]], axis=-1)

    gpy = S5_YBLK // c
    eye_y = jnp.eye(gpy, dtype=F32)

    def out_blocks(cm):
        cm = cm.astype(F32).reshape(S5_N_YBLK, gpy, c, p)
        return jnp.einsum('xy,jxcp->jxpyc', eye_y, cm).reshape(S5_N_YBLK, S5_HBLK, S5_YBLK)

    cr = out_blocks(lp['c_re']).astype(BF16)
    ci = (-out_blocks(lp['c_im'])).astype(BF16)
    d = lp['d'].astype(F32).reshape(1, D_SSM)
    return wx, apow, cr, ci, d


def _s5(u, h0, tables, seg_len):
    bn, seq, _ = u.shape
    rows = SUBLANES * seg_len
    wx, apow, cr, ci, d = tables
    r = jnp.arange(rows)
    perm = (r[None, :] == ((r % SUBLANES) * seg_len + r // SUBLANES)[:, None]).astype(BF16)
    const2 = lambda b, t: (0, 0)
    const3 = lambda b, t: (0, 0, 0)
    return pl.pallas_call(
        functools.partial(_s5_kernel, seg_len=seg_len),
        grid=(bn, seq // rows),
        in_specs=[
            pl.BlockSpec((None, rows, D_SSM), lambda b, t: (b, t, 0)),
            pl.BlockSpec((None, 1, 2 * N_STATE), lambda b, t: (b, 0, 0)),
            pl.BlockSpec((rows, rows), const2),
            pl.BlockSpec((rows, rows), const2),
            pl.BlockSpec(wx.shape, const3),
            pl.BlockSpec(apow.shape, const2),
            pl.BlockSpec(cr.shape, const3),
            pl.BlockSpec(ci.shape, const3),
            pl.BlockSpec((1, D_SSM), const2),
        ],
        out_specs=[
            pl.BlockSpec((None, rows, D_SSM), lambda b, t: (b, t, 0)),
            pl.BlockSpec((None, 1, 2 * N_STATE), lambda b, t: (b, 0, 0)),
        ],
        out_shape=[jax.ShapeDtypeStruct((bn, seq, D_SSM), BF16),
                   jax.ShapeDtypeStruct((bn, 1, 2 * N_STATE), F32)],
        scratch_shapes=[pltpu.VMEM((rows, 2 * N_STATE), F32),
                        pltpu.VMEM((SUBLANES, 2 * N_STATE), F32),
                        pltpu.VMEM((1, 2 * N_STATE), F32)],
        compiler_params=_cparams("parallel", "arbitrary"),
        name="s5",
    )(u, h0, perm, perm.T, wx, apow, cr, ci, d)


GLU_TN = 512


def _glu_kernel(y_ref, wa_ref, wb_ref, ga_ref, o_ref):
    y = y_ref[...]
    a = _dot(y, wa_ref[...])
    b = _dot(y, wb_ref[...])
    o_ref[...] = (ga_ref[...].astype(F32) * a * _sigmoid(b)).astype(BF16)


def _glu(y, w_glu_bf, sig_a, tm):
    T = y.shape[0]
    nj = D_MODEL // GLU_TN
    return pl.pallas_call(
        _glu_kernel,
        grid=(T // tm, nj),
        in_specs=[
            pl.BlockSpec((tm, D_SSM), lambda i, j: (i, 0)),
            pl.BlockSpec((D_SSM, GLU_TN), lambda i, j: (0, j)),
            pl.BlockSpec((D_SSM, GLU_TN), lambda i, j: (0, j + nj)),
            pl.BlockSpec((tm, GLU_TN), lambda i, j: (i, j)),
        ],
        out_specs=pl.BlockSpec((tm, GLU_TN), lambda i, j: (i, j)),
        out_shape=jax.ShapeDtypeStruct((T, D_MODEL), BF16),
        compiler_params=_cparams("parallel", "arbitrary"),
        name="glu",
    )(y, w_glu_bf, w_glu_bf, sig_a)


def _attn_kernel(q_ref, k_ref, v_ref, lq1_ref, lk1_ref, lq2_ref, lk2_ref, g_ref, o_ref,
                 qt_scr, s0_scr, s1_scr, m_scr, l_scr, acc_scr, *, tq, tk, past, paired, lambda_init):
    i = pl.program_id(2)
    nk = k_ref.shape[0] // tk
    q = q_ref[...].astype(F32)
    lane = lax.broadcasted_iota(jnp.int32, q.shape, 1)
    qs = jnp.concatenate([jnp.where(lane < HEAD_DIM, q, 0.0), jnp.where(lane >= HEAD_DIM, q, 0.0)], axis=0)
    qt_scr[...] = qs.T.astype(BF16)

    m_scr[...] = jnp.full(m_scr.shape, NEG_INF, F32)
    l_scr[...] = jnp.zeros(l_scr.shape, F32)
    acc_scr[...] = jnp.zeros(acc_scr.shape, F32)

    q_lo = past + i * tq
    min_qc = q_lo // CHUNK
    max_qc = (q_lo + tq - 1) // CHUNK
    n_proc = jnp.minimum(nk, (max_qc * CHUNK + CHUNK - 1) // tk + 1)
    n_full = jnp.clip((min_qc * CHUNK + CHUNK) // tk, 0, n_proc)

    def scores(j, s_ref):
        k0 = pl.multiple_of(j * tk, tk)
        s_ref[...] = _dot(k_ref[pl.ds(k0, tk), :], qt_scr[...])

    def update(j, s_ref, cols):
        k0 = pl.multiple_of(j * tk, tk)
        vb = v_ref[pl.ds(k0, tk), :]
        for lo, hi, masked in cols:
            s = s_ref[:, lo:hi]
            if masked:
                kpos = k0 + lax.broadcasted_iota(jnp.int32, s.shape, 0)
                qpos = q_lo + jnp.bitwise_and(lo + lax.broadcasted_iota(jnp.int32, s.shape, 1), tq - 1)
                s = jnp.where(jnp.right_shift(kpos, CHUNK_SHIFT) <= jnp.right_shift(qpos, CHUNK_SHIFT), s, NEG_INF)
            m_old = m_scr[:, lo:hi]
            m_new = jnp.maximum(m_old, jnp.max(s, axis=0, keepdims=True))
            alpha = jnp.exp2(m_old - m_new)
            p = jnp.exp2(s - m_new)
            l_scr[:, lo:hi] = alpha * l_scr[:, lo:hi] + jnp.sum(p, axis=0, keepdims=True)
            pv = lax.dot_general(vb, p.astype(BF16), (((0,), (0,)), ((), ())), preferred_element_type=F32)
            acc_scr[:, lo:hi] = alpha * acc_scr[:, lo:hi] + pv
            m_scr[:, lo:hi] = m_new

    everything = ((0, 2 * tq, False),)
    everything_masked = ((0, 2 * tq, True),)

    if paired:
        scores(0, s0_scr)

        def pair(p, c):
            scores(2 * p + 1, s1_scr)
            update(2 * p, s0_scr, everything)
            scores(2 * p + 2, s0_scr)
            update(2 * p + 1, s1_scr, everything)
            return c

        lax.fori_loop(0, i, pair, 0)
        scores(2 * i + 1, s1_scr)
        update(2 * i, s0_scr, tuple(r for c0 in (0, tq) for r in ((c0, c0 + tk, True), (c0 + tk, c0 + tq, False))))
        update(2 * i + 1, s1_scr, tuple((c0 + tk, c0 + tq, True) for c0 in (0, tq)))
    else:
        def full_body(j, c):
            scores(j, s0_scr)
            update(j, s0_scr, everything)
            return c

        def masked_body(j, c):
            scores(j, s0_scr)
            update(j, s0_scr, everything_masked)
            return c

        lax.fori_loop(0, n_full, full_body, 0)
        lax.fori_loop(n_full, n_proc, masked_body, 0)

    lam = (jnp.exp(jnp.sum(lq1_ref[...] * lk1_ref[...], axis=-1, keepdims=True))
           - jnp.exp(jnp.sum(lq2_ref[...] * lk2_ref[...], axis=-1, keepdims=True)) + lambda_init)
    ot = acc_scr[:, :tq] / l_scr[:, :tq] - lam * (acc_scr[:, tq:] / l_scr[:, tq:])
    o_ref[...] = (_rms(ot.T, g_ref[...]) * (1.0 - lambda_init)).astype(BF16)


def _attention(q, k, v, lp, tq, tk, past, lambda_init):
    bn, seq, _ = q.shape
    lk = k.shape[1]
    vec = lambda a: a.astype(F32).reshape(1, -1)
    small = lambda n: pl.BlockSpec((1, n), lambda b, h, i: (0, 0))
    paired = past == 0 and tq == 2 * tk and lk == seq
    s_scr = pltpu.VMEM((tk, 2 * tq), F32)
    return pl.pallas_call(
        functools.partial(_attn_kernel, tq=tq, tk=tk, past=past, paired=paired, lambda_init=lambda_init),
        grid=(bn, N_HEADS, seq // tq),
        in_specs=[
            pl.BlockSpec((None, tq, V_DIM), lambda b, h, i: (b, i, h)),
            pl.BlockSpec((None, lk, V_DIM), lambda b, h, i: (b, 0, h)),
            pl.BlockSpec((None, lk, V_DIM), lambda b, h, i: (b, 0, h)),
            small(HEAD_DIM), small(HEAD_DIM), small(HEAD_DIM), small(HEAD_DIM), small(V_DIM),
        ],
        out_specs=pl.BlockSpec((None, tq, V_DIM), lambda b, h, i: (b, i, h)),
        out_shape=jax.ShapeDtypeStruct((bn, seq, V_COLS), BF16),
        scratch_shapes=[pltpu.VMEM((V_DIM, 2 * tq), BF16), s_scr, s_scr, pltpu.VMEM((1, 2 * tq), F32),
                        pltpu.VMEM((1, 2 * tq), F32), pltpu.VMEM((V_DIM, 2 * tq), F32)],
        compiler_params=_cparams("parallel", "parallel", "arbitrary"),
        name="diff_attn",
    )(q, k, v, vec(lp['lq1']), vec(lp['lk1']), vec(lp['lq2']), vec(lp['lk2']), vec(lp['g_subln']))


def _merge_kernel(x_ref, o_ref, ga_ref, sb_ref, wa_ref, wo_ref, g_ref, x1_ref, h2t_ref):
    branch_b = _dot(o_ref[...], wa_ref[...])
    merged = ga_ref[...].astype(F32) + sb_ref[...].astype(F32) * branch_b
    x1 = x_ref[...] + _dot(merged.astype(BF16), wo_ref[...])
    x1_ref[...] = x1
    h2t_ref[...] = _rms(x1, g_ref[...]).T.astype(BF16)


def _merge(x2d, o, gated_a, sig_b, wa_bf, wo_bf, g_ffn, tm):
    T = x2d.shape[0]
    row = lambda n: pl.BlockSpec((tm, n), lambda i: (i, 0))
    const = lambda s: pl.BlockSpec(s, lambda i: (0, 0))
    return pl.pallas_call(
        _merge_kernel,
        grid=(T // tm,),
        in_specs=[row(D_MODEL), row(V_COLS), row(D_MODEL), row(D_MODEL),
                  const((V_COLS, D_MODEL)), const((D_MODEL, D_MODEL)), const((1, D_MODEL))],
        out_specs=[row(D_MODEL), pl.BlockSpec((D_MODEL, tm), lambda i: (0, i))],
        out_shape=[jax.ShapeDtypeStruct((T, D_MODEL), F32), jax.ShapeDtypeStruct((D_MODEL, T), BF16)],
        compiler_params=_cparams("parallel"),
        name="merge_out",
    )(x2d, o, gated_a, sig_b, wa_bf, wo_bf, g_ffn)


ROUTE_LC = 256
TAKEN = -3.0e38


def _split_bf16(x):
    hi = x.astype(BF16)
    return hi, (x - hi.astype(F32)).astype(BF16)


def _oddeven_merge(lo, hi, r):
    step = r * 2
    if step < hi - lo:
        yield from _oddeven_merge(lo, hi, step)
        yield from _oddeven_merge(lo + r, hi, step)
        yield from [(i, i + r) for i in range(lo + r, hi - r, step)]
    else:
        yield (lo, lo + r)


def _oddeven_merge_sort(lo, hi):
    if hi - lo >= 1:
        mid = lo + (hi - lo) // 2
        yield from _oddeven_merge_sort(lo, mid)
        yield from _oddeven_merge_sort(mid + 1, hi)
        yield from _oddeven_merge(lo, hi, 1)


def _top_rows_sorted(s, k):
    n = s.shape[0] // SUBLANES
    rows = [s[SUBLANES * g:SUBLANES * (g + 1), :] for g in range(n)]
    n_pow2 = 1 << (n - 1).bit_length()
    for i, j in _oddeven_merge_sort(0, n_pow2 - 1):
        if j < n:
            rows[i], rows[j] = jnp.maximum(rows[i], rows[j]), jnp.minimum(rows[i], rows[j])
    taken = jnp.full_like(rows[0], TAKEN)
    out = []
    for r in range(k):
        m = jnp.max(rows[0], axis=0, keepdims=True)
        out.append(m)
        hit = rows[0] == m
        depth = min(n, k - r - 1)
        for d in range(depth):
            rows[d] = jnp.where(hit, rows[d + 1] if d + 1 < n else taken, rows[d])
    return out


def _stack_rows(rows):
    n, w = len(rows), rows[0].shape[1]
    idx = lax.broadcasted_iota(jnp.int32, (n, w), 0)
    out = jnp.zeros((n, w), F32)
    for r, v in enumerate(rows):
        out = jnp.where(idx == r, v, out)
    return out


def _route_kernel(h2t_ref, wq_ref, keys_ref, thr_ref, s2_ref, e1_ref, e2_ref, s1_scr):
    tt = h2t_ref.shape[1]
    n_top = PEER_TOPK + 1
    qt = _dot(wq_ref[...], h2t_ref[...])
    halves = []
    for c in range(2):
        q_hi, q_lo = _split_bf16(qt[c * PEER_HALF:(c + 1) * PEER_HALF, :])
        k_hi, k_lo = _split_bf16(keys_ref[0, c])
        halves.append(_dot(k_hi, q_hi) + _dot(k_hi, q_lo) + _dot(k_lo, q_hi))
    s1_scr[...] = halves[0]
    s2_ref[0] = halves[1]

    width = min(tt, ROUTE_LC)
    for lc in range(tt // width):
        sl = slice(lc * width, (lc + 1) * width)
        s1 = s1_scr[:, sl]
        s2 = s2_ref[0, :, sl]
        a = _top_rows_sorted(s1, n_top)
        b = _top_rows_sorted(s2, n_top)
        taken = jnp.full_like(b[0], TAKEN)
        b_all = _stack_rows(b + [taken] * (3 * SUBLANES - n_top))
        tail = [a[i] + b[j] for i in range(4, n_top) for j in range(n_top // (i + 1))]
        tail += [taken] * (-len(tail) % SUBLANES)
        cand = jnp.concatenate([a[0] + b_all] + [a[i] + b_all[:SUBLANES] for i in range(1, 4)]
                               + [_stack_rows(tail)], axis=0)
        top = _top_rows_sorted(cand, n_top)
        z = jnp.zeros_like(top[0])
        for r in range(PEER_TOPK):
            z = z + jnp.exp(top[r] - top[0])
        tau = 0.5 * (top[PEER_TOPK - 1] + top[PEER_TOPK])
        grouped = (N_KEYS // SUBLANES, SUBLANES, width)
        thr_ref[0, :, :, sl] = (tau - s1).reshape(grouped)
        e1_ref[0, :, :, sl] = (jnp.exp(s1 - a[0]) * (0.5 / z)).reshape(grouped)
        e2_ref[0, :, sl] = jnp.exp(s2 - b[0])


def _route(h2t, wq_t_bf, keys, tt):
    T = h2t.shape[1]
    arr = jax.ShapeDtypeStruct((PEER_HEADS, N_KEYS, T), F32)
    spec = pl.BlockSpec((1, N_KEYS, tt), lambda i, h: (h, 0, i))
    n_grp = N_KEYS // SUBLANES
    garr = jax.ShapeDtypeStruct((PEER_HEADS, n_grp, SUBLANES, T), F32)
    gspec = pl.BlockSpec((1, n_grp, SUBLANES, tt), lambda i, h: (h, 0, 0, i))
    return pl.pallas_call(
        _route_kernel,
        grid=(T // tt, PEER_HEADS),
        in_specs=[
            pl.BlockSpec((D_MODEL, tt), lambda i, h: (0, i)),
            pl.BlockSpec((PEER_QUERY, D_MODEL), lambda i, h: (h, 0)),
            pl.BlockSpec((1, 2, N_KEYS, PEER_HALF), lambda i, h: (h, 0, 0, 0)),
        ],
        out_specs=[gspec, spec, gspec, spec],
        out_shape=[garr, arr, garr, arr],
        scratch_shapes=[pltpu.VMEM((N_KEYS, tt), F32)],
        compiler_params=_cparams("parallel", "arbitrary"),
        name="peer_route",
    )(h2t, wq_t_bf, keys)


PEER_SUB = 512
PEER_EB = 2 * PEER_SUB
PEER_ROWS = PEER_SUB // N_KEYS
N_SUB = N_EXPERTS // PEER_SUB
PEER_JH = 64


def _peer_kernel(h2t_ref, thr_ref, s2_ref, e1_ref, e2_ref, u_ref, v_ref, o_ref,
                 act0, act1, coef0, coef1, acc_scr):
    g = pl.program_id(1)
    tt = h2t_ref.shape[1]

    def stage_a(half, act):
        act[...] = _dot(u_ref[half * PEER_SUB:(half + 1) * PEER_SUB, :], h2t_ref[...])

    def stage_b(b, act, coef):
        b = jnp.clip(b, 0, N_SUB - 1)
        grp = b // 2
        odd = (b % 2) == 1
        for lc in range(tt // LANES):
            ln = slice(lc * LANES, (lc + 1) * LANES)
            for jh in range(N_KEYS // PEER_JH):
                js = slice(jh * PEER_JH, (jh + 1) * PEER_JH)
                gates = [jnp.zeros((PEER_JH, LANES), F32) for _ in range(PEER_ROWS)]
                for h in range(PEER_HEADS):
                    s2 = s2_ref[h, js, ln]
                    e2 = e2_ref[h, js, ln]
                    thr_grp = thr_ref[h, grp, :, ln]
                    e1_grp = e1_ref[h, grp, :, ln]
                    thr_rows = jnp.where(odd, thr_grp[PEER_ROWS:], thr_grp[:PEER_ROWS])
                    e1_rows = jnp.where(odd, e1_grp[PEER_ROWS:], e1_grp[:PEER_ROWS])
                    for r in range(PEER_ROWS):
                        gates[r] = gates[r] + jnp.where(s2 >= thr_rows[r:r + 1], e1_rows[r:r + 1] * e2, 0.0)
                for r in range(PEER_ROWS):
                    rows = slice(r * N_KEYS + jh * PEER_JH, r * N_KEYS + (jh + 1) * PEER_JH)
                    x = act[rows, ln]
                    coef[rows, ln] = gates[r] * (x * (1.0 + lax.erf(x * (2.0 ** -0.5))))

    def stage_c():
        coef = jnp.concatenate([coef0[...].astype(BF16), coef1[...].astype(BF16)], axis=0)
        acc_scr[...] += lax.dot_general(coef, v_ref[...], (((0,), (0,)), ((), ())), preferred_element_type=F32)

    last = pl.num_programs(1) - 1

    @pl.when(g == 0)
    def _():
        acc_scr[...] = jnp.zeros(acc_scr.shape, F32)
        stage_a(0, act0)
        stage_a(1, act1)
        stage_b(2 * g, act0, coef0)

    @pl.when(jnp.logical_and(g > 0, g < last))
    def _():
        stage_a(0, act0)
        stage_b(2 * g - 1, act1, coef1)
        stage_a(1, act1)
        stage_c()
        stage_b(2 * g, act0, coef0)

    @pl.when(g == last)
    def _():
        stage_b(2 * g - 1, act1, coef1)
        stage_c()
        o_ref[...] = acc_scr[...].astype(BF16)


def _peer(h2t, route, u_bf, v_bf, tt):
    T = h2t.shape[1]
    thr, s2, e1, e2 = route
    n_eb = N_EXPERTS // PEER_EB
    spec = pl.BlockSpec((PEER_HEADS, N_KEYS, tt), lambda i, g: (0, 0, i))
    gspec = pl.BlockSpec((PEER_HEADS, N_KEYS // SUBLANES, SUBLANES, tt), lambda i, g: (0, 0, 0, i))
    assert SUBLANES == 2 * PEER_ROWS
    return pl.pallas_call(
        _peer_kernel,
        grid=(T // tt, n_eb + 1),
        in_specs=[
            pl.BlockSpec((D_MODEL, tt), lambda i, g: (0, i)),
            gspec, spec, gspec, spec,
            pl.BlockSpec((PEER_EB, D_MODEL), lambda i, g: (jnp.minimum(g, n_eb - 1), 0)),
            pl.BlockSpec((PEER_EB, D_MODEL), lambda i, g: (jnp.maximum(g - 1, 0), 0)),
        ],
        out_specs=pl.BlockSpec((tt, D_MODEL), lambda i, g: (i, 0)),
        out_shape=jax.ShapeDtypeStruct((T, D_MODEL), BF16),
        scratch_shapes=[pltpu.VMEM((PEER_SUB, tt), F32), pltpu.VMEM((PEER_SUB, tt), F32),
                        pltpu.VMEM((PEER_SUB, tt), F32), pltpu.VMEM((PEER_SUB, tt), F32),
                        pltpu.VMEM((tt, D_MODEL), F32)],
        compiler_params=_cparams("parallel", "arbitrary"),
        name="peer_dense",
    )(h2t, thr, s2, e1, e2, u_bf, v_bf)


def _ple_kernel(x1_ref, po_ref, p_ref, wg_ref, wp_ref, gp_ref, gf_ref, y_ref):
    x2 = x1_ref[...] + po_ref[...].astype(F32)
    h3 = _rms(x2, gp_ref[...]).astype(BF16)
    gate = _sigmoid(_dot(h3, wg_ref[...]))
    proj = _dot(p_ref[...].astype(BF16), wp_ref[...])
    y_ref[...] = _rms(x2 + proj * gate, gf_ref[...])


def _ple(x1, peer_out, p2d, wg_bf, wp_bf, g_ple, g_final, tm):
    T = x1.shape[0]
    row = lambda n: pl.BlockSpec((tm, n), lambda i: (i, 0))
    const = lambda s: pl.BlockSpec(s, lambda i: (0, 0))
    return pl.pallas_call(
        _ple_kernel,
        grid=(T // tm,),
        in_specs=[row(D_MODEL), row(D_MODEL), row(PLE_DIM),
                  const((D_MODEL, D_MODEL)), const((PLE_DIM, D_MODEL)), const((1, D_MODEL)), const((1, D_MODEL))],
        out_specs=row(D_MODEL),
        out_shape=jax.ShapeDtypeStruct((T, D_MODEL), F32),
        compiler_params=_cparams("parallel"),
        name="ple_final",
    )(x1, peer_out, p2d, wg_bf, wp_bf, g_ple, g_final)


def _tile(n, pref):
    t = min(n, pref)
    assert n % t == 0, (n, t)
    return t


def _layer(x, p, h0_re, h0_im, k_past, v_past, lp, lambda_init):
    bn, seq, _ = x.shape
    past = k_past.shape[1]
    T = bn * seq
    x2d = x.reshape(T, D_MODEL)
    row = lambda a: a.astype(F32).reshape(1, -1)

    tm_in = _tile(T, 1024)
    cos_t, sin_t = _rope_tables(seq, past, max(seq, tm_in))
    u, q, k_f32, k_bf, v_f32, v_bf, sig_a, sig_b = _inproj(
        x2d, row(lp['g_mix']), lp['w_in'].astype(BF16), cos_t, sin_t, tm_in)

    seg_len = _tile(seq // SUBLANES, 32)
    h0 = jnp.concatenate([h0_re.reshape(bn, 1, N_STATE), h0_im.reshape(bn, 1, N_STATE)], axis=-1).astype(F32)
    y_ssm, h_fin = _s5(u.reshape(bn, seq, D_SSM), h0, _s5_tables(lp, seg_len), seg_len)
    gated_a = _glu(y_ssm.reshape(T, D_SSM), lp['w_glu'].astype(BF16), sig_a, _tile(T, 1024))

    k_all = jnp.concatenate([k_past.reshape(bn, past, Q_COLS).astype(BF16), k_bf.reshape(bn, seq, Q_COLS)], axis=1)
    v_all = jnp.concatenate([v_past.reshape(bn, past, V_COLS).astype(BF16), v_bf.reshape(bn, seq, V_COLS)], axis=1)
    seq_q = max(seq, LANES // 2)
    q3 = jnp.pad(q.reshape(bn, seq, Q_COLS), ((0, 0), (0, seq_q - seq), (0, 0)))
    tq = _tile(seq_q, 512)
    tk = tq // 2 if past == 0 else past + seq
    o = _attention(q3, k_all, v_all, lp, tq, tk, past, lambda_init)[:, :seq]

    x1, h2t = _merge(x2d, o.reshape(T, V_COLS), gated_a, sig_b, lp['w_attn_out'].astype(BF16),
                    lp['w_out'].astype(BF16), row(lp['g_ffn']), _tile(T, 256))

    tt = _tile(T, 512)
    route = _route(h2t, lp['peer_w_q'].T.astype(BF16), lp['peer_keys'].astype(F32), tt)
    peer_out = _peer(h2t, route, lp['peer_u'].astype(BF16), lp['peer_v'].astype(BF16), tt)

    y = _ple(x1, peer_out, p.reshape(T, PLE_DIM), lp['w_ple_gate'].astype(BF16), lp['w_ple_proj'].astype(BF16),
             row(lp['g_ple']), row(lp['g_final']), _tile(T, 256))

    new_k = k_f32.reshape(bn, seq, N_HEADS, 2 * HEAD_DIM)
    new_v = v_f32.reshape(bn, seq, N_HEADS, V_DIM)
    hr = h_fin[:, 0, :N_STATE].reshape(bn, N_SSM_GROUPS, SSM_STATE)
    hi = h_fin[:, 0, N_STATE:].reshape(bn, N_SSM_GROUPS, SSM_STATE)
    return y.reshape(bn, seq, D_MODEL), new_k, new_v, hr, hi


def _trunk(x, p, h0_re, h0_im, k_past, v_past, lp):
    assert DEPTH == 1
    lambda_init = 0.8 - 0.6 * math.exp(-0.3 * 0)
    y, k_new, v_new, hr, hi = _layer(x, p[0], h0_re[0], h0_im[0], k_past[0], v_past[0], lp, lambda_init)
    return y, k_new[None], v_new[None], hr[None], hi[None]


def kernel(x_prompt, x_sample, p_prompt, p_sample, cache_k, cache_v, state_ssm_re, state_ssm_im,
           g_mix_norm, w_in, ssm_lambda_re, ssm_lambda_im, ssm_log_step, ssm_b_re, ssm_b_im,
           ssm_c_re, ssm_c_im, ssm_d, w_glu, diff_lambda_q1, diff_lambda_k1, diff_lambda_q2,
           diff_lambda_k2, g_subln, w_attn_out, w_out, g_ffn_norm, peer_w_q, peer_keys, peer_u,
           peer_v, g_ple_norm, w_ple_gate, w_ple_proj, g_final):
    lp = dict(g_mix=g_mix_norm[0], w_in=w_in[0], lam_re=ssm_lambda_re[0], lam_im=ssm_lambda_im[0],
              log_step=ssm_log_step[0], b_re=ssm_b_re[0], b_im=ssm_b_im[0], c_re=ssm_c_re[0],
              c_im=ssm_c_im[0], d=ssm_d[0], w_glu=w_glu[0], lq1=diff_lambda_q1[0],
              lk1=diff_lambda_k1[0], lq2=diff_lambda_q2[0], lk2=diff_lambda_k2[0],
              g_subln=g_subln[0], w_attn_out=w_attn_out[0], w_out=w_out[0], g_ffn=g_ffn_norm[0],
              peer_w_q=peer_w_q[0], peer_keys=peer_keys[0], peer_u=peer_u[0], peer_v=peer_v[0],
              g_ple=g_ple_norm[0], w_ple_gate=w_ple_gate[0], w_ple_proj=w_ple_proj[0], g_final=g_final)
    bn = x_prompt.shape[0]
    zeros_state = jnp.zeros((DEPTH, bn, N_SSM_GROUPS, SSM_STATE), F32)
    k_none = jnp.zeros((DEPTH, bn, 0, N_HEADS, 2 * HEAD_DIM), x_prompt.dtype)
    v_none = jnp.zeros((DEPTH, bn, 0, N_HEADS, V_DIM), x_prompt.dtype)
    y_p, k_p, v_p, r_p, i_p = _trunk(x_prompt, p_prompt, zeros_state, zeros_state, k_none, v_none, lp)
    y_s, k_s, v_s, r_s, i_s = _trunk(x_sample, p_sample, state_ssm_re, state_ssm_im, cache_k, cache_v, lp)
    return (y_p, y_s, k_p, v_p, r_p, i_p, k_s, v_s, r_s, i_s)
```

```python
import functools
import math

import jax
import jax.numpy as jnp
from jax import lax
from jax.experimental import pallas as pl
from jax.experimental.pallas import tpu as pltpu

F32 = jnp.float32
BF16 = jnp.bfloat16

D_MODEL = 2048
DEPTH = 1
CHUNK = 64
CHUNK_SHIFT = CHUNK.bit_length() - 1
assert 1 << CHUNK_SHIFT == CHUNK
PLE_DIM = 256
RMS_EPS = 1e-6
NEG_INF = -1e30
D_SSM = 1024
SSM_GROUP = 16
N_SSM_GROUPS = D_SSM // SSM_GROUP
SSM_STATE = 64
N_STATE = N_SSM_GROUPS * SSM_STATE
N_HEADS = 8
HEAD_DIM = 64
V_DIM = 2 * HEAD_DIM
ROPE_THETA = 10000.0
Q_COLS = N_HEADS * 2 * HEAD_DIM
V_COLS = N_HEADS * V_DIM
IN_COLS = D_SSM + 2 * Q_COLS + V_COLS + 2 * D_MODEL
PEER_HEADS = 8
N_KEYS = 128
N_EXPERTS = N_KEYS * N_KEYS
PEER_QUERY = 256
PEER_HALF = PEER_QUERY // 2
PEER_TOPK = 16

LANES = 128
SUBLANES = 8
VMEM_LIMIT = 56 * 1024 * 1024


def _cparams(*sem):
    return pltpu.CompilerParams(dimension_semantics=sem, vmem_limit_bytes=VMEM_LIMIT)


def _rms(x, g):
    return x * lax.rsqrt(jnp.mean(x * x, axis=-1, keepdims=True) + RMS_EPS) * g


def _gelu(x):
    return 0.5 * x * (1.0 + lax.erf(x * (2.0 ** -0.5)))


def _sigmoid(x):
    return 1.0 / (1.0 + jnp.exp(-x))


def _dot(a, b):
    return jnp.dot(a, b, preferred_element_type=F32)


IN_TN = 512
Q_SCALE = HEAD_DIM ** -0.5 * math.log2(math.e)
IN_MIX_COLS = D_SSM + 2 * Q_COLS + V_COLS


def _column_tiles(h_scr, w_ref):
    return [functools.partial(lambda c: _dot(h_scr[...], w_ref[:, c * IN_TN:(c + 1) * IN_TN]), c)
            for c in range(w_ref.shape[1] // IN_TN)]


def _inproj_mix_kernel(x_ref, g_ref, w_ref, cos_ref, sin_ref, u_ref, q_ref, kf_ref, kb_ref, vf_ref, vb_ref, h_scr):
    h_scr[...] = _rms(x_ref[...], g_ref[...]).astype(BF16)
    z = _column_tiles(h_scr, w_ref)

    def rope(t):
        n = t.shape[-1]
        lane = lax.broadcasted_iota(jnp.int32, t.shape, 1)
        first = jnp.bitwise_and(lane, HEAD_DIM - 1) < (HEAD_DIM // 2)
        partner = jnp.where(first, pltpu.roll(t, n - HEAD_DIM // 2, 1), pltpu.roll(t, HEAD_DIM // 2, 1))
        reps = n // LANES
        return t * jnp.tile(cos_ref[...], (1, reps)) + partner * jnp.tile(sin_ref[...], (1, reps))

    per = Q_COLS // IN_TN
    for c in range(per):
        cols = slice(c * IN_TN, (c + 1) * IN_TN)
        u_ref[:, cols] = z[c]().astype(BF16)
        q_ref[:, cols] = (rope(z[per + c]()) * Q_SCALE).astype(BF16)
        k = rope(z[2 * per + c]())
        kf_ref[:, cols] = k
        kb_ref[:, cols] = k.astype(BF16)
        v = z[3 * per + c]()
        vf_ref[:, cols] = v
        vb_ref[:, cols] = v.astype(BF16)


def _inproj_gate_kernel(x_ref, g_ref, w_ref, ga_ref, gb_ref, h_scr):
    h_scr[...] = _rms(x_ref[...], g_ref[...]).astype(BF16)
    z = _column_tiles(h_scr, w_ref)
    per = D_MODEL // IN_TN
    for c in range(per):
        cols = slice(c * IN_TN, (c + 1) * IN_TN)
        ga_ref[:, cols] = _sigmoid(z[c]()).astype(BF16)
        gb_ref[:, cols] = _sigmoid(z[per + c]()).astype(BF16)


def _inproj(x2d, g, w_bf, cos_t, sin_t, tm):
    assert D_SSM == Q_COLS == V_COLS
    T = x2d.shape[0]
    n_tab = cos_t.shape[0] // tm
    row = lambda n: pl.BlockSpec((tm, n), lambda i: (i, 0))
    const = lambda s: pl.BlockSpec(s, lambda i: (0, 0))
    table = pl.BlockSpec((tm, LANES), lambda i: (i % n_tab, 0))
    arr = lambda n, dt: jax.ShapeDtypeStruct((T, n), dt)
    common = dict(grid=(T // tm,), scratch_shapes=[pltpu.VMEM((tm, D_MODEL), BF16)],
                  compiler_params=_cparams("parallel"))
    u, q, kf, kb, vf, vb = pl.pallas_call(
        _inproj_mix_kernel,
        in_specs=[row(D_MODEL), const((1, D_MODEL)), const((D_MODEL, IN_MIX_COLS)), table, table],
        out_specs=[row(Q_COLS)] * 6,
        out_shape=[arr(Q_COLS, BF16), arr(Q_COLS, BF16), arr(Q_COLS, F32), arr(Q_COLS, BF16),
                   arr(Q_COLS, F32), arr(Q_COLS, BF16)],
        name="inproj_mix", **common,
    )(x2d, g, w_bf[:, :IN_MIX_COLS], cos_t, sin_t)
    ga, gb = pl.pallas_call(
        _inproj_gate_kernel,
        in_specs=[row(D_MODEL), const((1, D_MODEL)), const((D_MODEL, 2 * D_MODEL))],
        out_specs=[row(D_MODEL)] * 2,
        out_shape=[arr(D_MODEL, BF16)] * 2,
        name="inproj_gate", **common,
    )(x2d, g, w_bf[:, IN_MIX_COLS:])
    return u, q, kf, kb, vf, vb, ga, gb


def _rope_tables(seq, past, rows):
    half = HEAD_DIM // 2
    inv = ROPE_THETA ** (-jnp.arange(half, dtype=F32) * 2.0 / HEAD_DIM)
    pos = (past + (jnp.arange(rows, dtype=jnp.int32) % seq)).astype(F32)
    ang = pos[:, None] * inv[None, :]
    cos, sin = jnp.cos(ang), jnp.sin(ang)
    reps = LANES // HEAD_DIM
    cos_t = jnp.tile(jnp.concatenate([cos, cos], axis=-1), (1, reps))
    sin_t = jnp.tile(jnp.concatenate([-sin, sin], axis=-1), (1, reps))
    return cos_t, sin_t


S5_COLS = 512
S5_UBLK = LANES
S5_N_UBLK = D_SSM // S5_UBLK
S5_XBLK = (S5_UBLK // SSM_GROUP) * SSM_STATE
S5_YBLK = 256
S5_N_YBLK = D_SSM // S5_YBLK
S5_HBLK = (S5_YBLK // SSM_GROUP) * SSM_STATE


def _s5_kernel(u_ref, h0_ref, perm_ref, permt_ref, wx_ref, apow_ref, cr_ref, ci_ref, d_ref,
               y_ref, hfin_ref, x_scr, c_scr, carry_scr, *, seg_len):
    t = pl.program_id(1)
    rows = SUBLANES * seg_len
    n = N_STATE

    @pl.when(t == 0)
    def _():
        carry_scr[...] = h0_ref[...]

    up = _dot(perm_ref[...], u_ref[...])
    upb = up.astype(BF16)
    for a in range(S5_N_UBLK):
        xa = _dot(upb[:, a * S5_UBLK:(a + 1) * S5_UBLK], wx_ref[a])
        x_scr[:, a * S5_XBLK:(a + 1) * S5_XBLK] = xa[:, :S5_XBLK]
        x_scr[:, n + a * S5_XBLK:n + (a + 1) * S5_XBLK] = xa[:, S5_XBLK:]

    for c in range(n // S5_COLS):
        lo = c * S5_COLS
        ar = jnp.broadcast_to(apow_ref[0:1, lo:lo + S5_COLS], (SUBLANES, S5_COLS))
        ai = jnp.broadcast_to(apow_ref[0:1, n + lo:n + lo + S5_COLS], (SUBLANES, S5_COLS))

        def scan_step(k, carry, lo=lo, ar=ar, ai=ai):
            hr, hi = carry
            r0 = pl.multiple_of(k * SUBLANES, SUBLANES)
            xr = x_scr[pl.ds(r0, SUBLANES), lo:lo + S5_COLS]
            xi = x_scr[pl.ds(r0, SUBLANES), n + lo:n + lo + S5_COLS]
            nr = ar * hr - ai * hi + xr
            ni = ar * hi + ai * hr + xi
            x_scr[pl.ds(r0, SUBLANES), lo:lo + S5_COLS] = nr
            x_scr[pl.ds(r0, SUBLANES), n + lo:n + lo + S5_COLS] = ni
            return nr, ni

        zero = jnp.zeros((SUBLANES, S5_COLS), F32)
        lax.fori_loop(0, seg_len, scan_step, (zero, zero))

    alr = apow_ref[seg_len - 1:seg_len, :n]
    ali = apow_ref[seg_len - 1:seg_len, n:]
    cr = carry_scr[:, :n]
    ci = carry_scr[:, n:]
    for s in range(SUBLANES):
        c_scr[s:s + 1, :n] = cr
        c_scr[s:s + 1, n:] = ci
        lr = x_scr[rows - SUBLANES + s:rows - SUBLANES + s + 1, :n]
        li = x_scr[rows - SUBLANES + s:rows - SUBLANES + s + 1, n:]
        cr, ci = alr * cr - ali * ci + lr, alr * ci + ali * cr + li
    carry_scr[:, :n] = cr
    carry_scr[:, n:] = ci

    for c in range(n // S5_COLS):
        lo = c * S5_COLS
        sr = c_scr[:, lo:lo + S5_COLS]
        si = c_scr[:, n + lo:n + lo + S5_COLS]

        def fix_step(k, _, lo=lo, sr=sr, si=si):
            r0 = pl.multiple_of(k * SUBLANES, SUBLANES)
            pr = apow_ref[pl.ds(k, 1), lo:lo + S5_COLS]
            pi = apow_ref[pl.ds(k, 1), n + lo:n + lo + S5_COLS]
            x_scr[pl.ds(r0, SUBLANES), lo:lo + S5_COLS] += pr * sr - pi * si
            x_scr[pl.ds(r0, SUBLANES), n + lo:n + lo + S5_COLS] += pr * si + pi * sr
            return 0

        lax.fori_loop(0, seg_len, fix_step, 0)

    @pl.when(t == pl.num_programs(1) - 1)
    def _():
        hfin_ref[...] = carry_scr[...]

    for j in range(S5_N_YBLK):
        hr = x_scr[:, j * S5_HBLK:(j + 1) * S5_HBLK].astype(BF16)
        hi = x_scr[:, n + j * S5_HBLK:n + (j + 1) * S5_HBLK].astype(BF16)
        yj = (_dot(hr, cr_ref[j]) + _dot(hi, ci_ref[j])
              + d_ref[:, j * S5_YBLK:(j + 1) * S5_YBLK] * up[:, j * S5_YBLK:(j + 1) * S5_YBLK])
        yj = _gelu(yj).astype(BF16)
        y_ref[:, j * S5_YBLK:(j + 1) * S5_YBLK] = _dot(permt_ref[...], yj).astype(BF16)


def _s5_tables(lp, seg_len):
    g, p, c = N_SSM_GROUPS, SSM_STATE, SSM_GROUP
    dt = jnp.exp(lp['log_step'].astype(F32))[:, None]
    lr, li = lp['lam_re'].astype(F32), lp['lam_im'].astype(F32)
    mag = jnp.exp(lr * dt)
    ar, ai = mag * jnp.cos(li * dt), mag * jnp.sin(li * dt)
    den = lr * lr + li * li
    fr = ((ar - 1.0) * lr + ai * li) / den
    fi = (ai * lr - (ar - 1.0) * li) / den
    br, bi = lp['b_re'].astype(F32), lp['b_im'].astype(F32)
    bbr = fr[..., None] * br - fi[..., None] * bi
    bbi = fr[..., None] * bi + fi[..., None] * br

    gpb = S5_UBLK // c
    eye = jnp.eye(gpb, dtype=F32)

    def in_blocks(bb):
        bb = bb.reshape(S5_N_UBLK, gpb, p, c)
        return jnp.einsum('xy,axpc->axcyp', eye, bb).reshape(S5_N_UBLK, S5_UBLK, S5_XBLK)

    wx = jnp.concatenate([in_blocks(bbr), in_blocks(bbi)], axis=-1).astype(BF16)

    pr, pi = ar.reshape(1, g * p), ai.reshape(1, g * p)
    while pr.shape[0] < seg_len:
        tr, ti = pr[-1:], pi[-1:]
        pr, pi = (jnp.concatenate([pr, pr * tr - pi * ti], axis=0),
                  jnp.concatenate([pi, pr * ti + pi * tr], axis=0))
    apow = jnp.concatenate([pr[:seg_len], pi[:seg_len]], axis=-1)

    gpy = S5_YBLK // c
    eye_y = jnp.eye(gpy, dtype=F32)

    def out_blocks(cm):
        cm = cm.astype(F32).reshape(S5_N_YBLK, gpy, c, p)
        return jnp.einsum('xy,jxcp->jxpyc', eye_y, cm).reshape(S5_N_YBLK, S5_HBLK, S5_YBLK)

    cr = out_blocks(lp['c_re']).astype(BF16)
    ci = (-out_blocks(lp['c_im'])).astype(BF16)
    d = lp['d'].astype(F32).reshape(1, D_SSM)
    return wx, apow, cr, ci, d


def _s5(u, h0, tables, seg_len):
    bn, seq, _ = u.shape
    rows = SUBLANES * seg_len
    wx, apow, cr, ci, d = tables
    r = jnp.arange(rows)
    perm = (r[None, :] == ((r % SUBLANES) * seg_len + r // SUBLANES)[:, None]).astype(BF16)
    const2 = lambda b, t: (0, 0)
    const3 = lambda b, t: (0, 0, 0)
    return pl.pallas_call(
        functools.partial(_s5_kernel, seg_len=seg_len),
        grid=(bn, seq // rows),
        in_specs=[
            pl.BlockSpec((None, rows, D_SSM), lambda b, t: (b, t, 0)),
            pl.BlockSpec((None, 1, 2 * N_STATE), lambda b, t: (b, 0, 0)),
            pl.BlockSpec((rows, rows), const2),
            pl.BlockSpec((rows, rows), const2),
            pl.BlockSpec(wx.shape, const3),
            pl.BlockSpec(apow.shape, const2),
            pl.BlockSpec(cr.shape, const3),
            pl.BlockSpec(ci.shape, const3),
            pl.BlockSpec((1, D_SSM), const2),
        ],
        out_specs=[
            pl.BlockSpec((None, rows, D_SSM), lambda b, t: (b, t, 0)),
            pl.BlockSpec((None, 1, 2 * N_STATE), lambda b, t: (b, 0, 0)),
        ],
        out_shape=[jax.ShapeDtypeStruct((bn, seq, D_SSM), BF16),
                   jax.ShapeDtypeStruct((bn, 1, 2 * N_STATE), F32)],
        scratch_shapes=[pltpu.VMEM((rows, 2 * N_STATE), F32),
                        pltpu.VMEM((SUBLANES, 2 * N_STATE), F32),
                        pltpu.VMEM((1, 2 * N_STATE), F32)],
        compiler_params=_cparams("parallel", "arbitrary"),
        name="s5",
    )(u, h0, perm, perm.T, wx, apow, cr, ci, d)


GLU_TN = 512


def _glu_kernel(y_ref, wa_ref, wb_ref, ga_ref, o_ref):
    y = y_ref[...]
    a = _dot(y, wa_ref[...])
    b = _dot(y, wb_ref[...])
    o_ref[...] = (ga_ref[...].astype(F32) * a * _sigmoid(b)).astype(BF16)


def _glu(y, w_glu_bf, sig_a, tm):
    T = y.shape[0]
    nj = D_MODEL // GLU_TN
    return pl.pallas_call(
        _glu_kernel,
        grid=(T // tm, nj),
        in_specs=[
            pl.BlockSpec((tm, D_SSM), lambda i, j: (i, 0)),
            pl.BlockSpec((D_SSM, GLU_TN), lambda i, j: (0, j)),
            pl.BlockSpec((D_SSM, GLU_TN), lambda i, j: (0, j + nj)),
            pl.BlockSpec((tm, GLU_TN), lambda i, j: (i, j)),
        ],
        out_specs=pl.BlockSpec((tm, GLU_TN), lambda i, j: (i, j)),
        out_shape=jax.ShapeDtypeStruct((T, D_MODEL), BF16),
        compiler_params=_cparams("parallel", "arbitrary"),
        name="glu",
    )(y, w_glu_bf, w_glu_bf, sig_a)


def _attn_kernel(q_ref, k_ref, v_ref, lq1_ref, lk1_ref, lq2_ref, lk2_ref, g_ref, o_ref,
                 qt_scr, s0_scr, s1_scr, m_scr, l_scr, acc_scr, *, tq, tk, past, paired, lambda_init):
    i = pl.program_id(2)
    nk = k_ref.shape[0] // tk
    q = q_ref[...].astype(F32)
    lane = lax.broadcasted_iota(jnp.int32, q.shape, 1)
    qs = jnp.concatenate([jnp.where(lane < HEAD_DIM, q, 0.0), jnp.where(lane >= HEAD_DIM, q, 0.0)], axis=0)
    qt_scr[...] = qs.T.astype(BF16)

    m_scr[...] = jnp.full(m_scr.shape, NEG_INF, F32)
    l_scr[...] = jnp.zeros(l_scr.shape, F32)
    acc_scr[...] = jnp.zeros(acc_scr.shape, F32)

    q_lo = past + i * tq
    min_qc = q_lo // CHUNK
    max_qc = (q_lo + tq - 1) // CHUNK
    n_proc = jnp.minimum(nk, (max_qc * CHUNK + CHUNK - 1) // tk + 1)
    n_full = jnp.clip((min_qc * CHUNK + CHUNK) // tk, 0, n_proc)

    def scores(j, s_ref):
        k0 = pl.multiple_of(j * tk, tk)
        s_ref[...] = _dot(k_ref[pl.ds(k0, tk), :], qt_scr[...])

    def update(j, s_ref, cols):
        k0 = pl.multiple_of(j * tk, tk)
        vb = v_ref[pl.ds(k0, tk), :]
        for lo, hi, masked in cols:
            s = s_ref[:, lo:hi]
            if masked:
                kpos = k0 + lax.broadcasted_iota(jnp.int32, s.shape, 0)
                qpos = q_lo + jnp.bitwise_and(lo + lax.broadcasted_iota(jnp.int32, s.shape, 1), tq - 1)
                s = jnp.where(jnp.right_shift(kpos, CHUNK_SHIFT) <= jnp.right_shift(qpos, CHUNK_SHIFT), s, NEG_INF)
            m_old = m_scr[:, lo:hi]
            m_new = jnp.maximum(m_old, jnp.max(s, axis=0, keepdims=True))
            alpha = jnp.exp2(m_old - m_new)
            p = jnp.exp2(s - m_new)
            l_scr[:, lo:hi] = alpha * l_scr[:, lo:hi] + jnp.sum(p, axis=0, keepdims=True)
            pv = lax.dot_general(vb, p.astype(BF16), (((0,), (0,)), ((), ())), preferred_element_type=F32)
            acc_scr[:, lo:hi] = alpha * acc_scr[:, lo:hi] + pv
            m_scr[:, lo:hi] = m_new

    everything = ((0, 2 * tq, False),)
    everything_masked = ((0, 2 * tq, True),)

    if paired:
        scores(0, s0_scr)

        def pair(p, c):
            scores(2 * p + 1, s1_scr)
            update(2 * p, s0_scr, everything)
            scores(2 * p + 2, s0_scr)
            update(2 * p + 1, s1_scr, everything)
            return c

        lax.fori_loop(0, i, pair, 0)
        scores(2 * i + 1, s1_scr)
        update(2 * i, s0_scr, tuple(r for c0 in (0, tq) for r in ((c0, c0 + tk, True), (c0 + tk, c0 + tq, False))))
        update(2 * i + 1, s1_scr, tuple((c0 + tk, c0 + tq, True) for c0 in (0, tq)))
    else:
        def full_body(j, c):
            scores(j, s0_scr)
            update(j, s0_scr, everything)
            return c

        def masked_body(j, c):
            scores(j, s0_scr)
            update(j, s0_scr, everything_masked)
            return c

        lax.fori_loop(0, n_full, full_body, 0)
        lax.fori_loop(n_full, n_proc, masked_body, 0)

    lam = (jnp.exp(jnp.sum(lq1_ref[...] * lk1_ref[...], axis=-1, keepdims=True))
           - jnp.exp(jnp.sum(lq2_ref[...] * lk2_ref[...], axis=-1, keepdims=True)) + lambda_init)
    ot = acc_scr[:, :tq] / l_scr[:, :tq] - lam * (acc_scr[:, tq:] / l_scr[:, tq:])
    o_ref[...] = (_rms(ot.T, g_ref[...]) * (1.0 - lambda_init)).astype(BF16)


def _attention(q, k, v, lp, tq, tk, past, lambda_init):
    bn, seq, _ = q.shape
    lk = k.shape[1]
    vec = lambda a: a.astype(F32).reshape(1, -1)
    small = lambda n: pl.BlockSpec((1, n), lambda b, h, i: (0, 0))
    paired = past == 0 and tq == 2 * tk and lk == seq
    s_scr = pltpu.VMEM((tk, 2 * tq), F32)
    return pl.pallas_call(
        functools.partial(_attn_kernel, tq=tq, tk=tk, past=past, paired=paired, lambda_init=lambda_init),
        grid=(bn, N_HEADS, seq // tq),
        in_specs=[
            pl.BlockSpec((None, tq, V_DIM), lambda b, h, i: (b, i, h)),
            pl.BlockSpec((None, lk, V_DIM), lambda b, h, i: (b, 0, h)),
            pl.BlockSpec((None, lk, V_DIM), lambda b, h, i: (b, 0, h)),
            small(HEAD_DIM), small(HEAD_DIM), small(HEAD_DIM), small(HEAD_DIM), small(V_DIM),
        ],
        out_specs=pl.BlockSpec((None, tq, V_DIM), lambda b, h, i: (b, i, h)),
        out_shape=jax.ShapeDtypeStruct((bn, seq, V_COLS), BF16),
        scratch_shapes=[pltpu.VMEM((V_DIM, 2 * tq), BF16), s_scr, s_scr, pltpu.VMEM((1, 2 * tq), F32),
                        pltpu.VMEM((1, 2 * tq), F32), pltpu.VMEM((V_DIM, 2 * tq), F32)],
        compiler_params=_cparams("parallel", "parallel", "arbitrary"),
        name="diff_attn",
    )(q, k, v, vec(lp['lq1']), vec(lp['lk1']), vec(lp['lq2']), vec(lp['lk2']), vec(lp['g_subln']))


def _merge_kernel(x_ref, o_ref, ga_ref, sb_ref, wa_ref, wo_ref, g_ref, x1_ref, h2t_ref):
    branch_b = _dot(o_ref[...], wa_ref[...])
    merged = ga_ref[...].astype(F32) + sb_ref[...].astype(F32) * branch_b
    x1 = x_ref[...] + _dot(merged.astype(BF16), wo_ref[...])
    x1_ref[...] = x1
    h2t_ref[...] = _rms(x1, g_ref[...]).T.astype(BF16)


def _merge(x2d, o, gated_a, sig_b, wa_bf, wo_bf, g_ffn, tm):
    T = x2d.shape[0]
    row = lambda n: pl.BlockSpec((tm, n), lambda i: (i, 0))
    const = lambda s: pl.BlockSpec(s, lambda i: (0, 0))
    return pl.pallas_call(
        _merge_kernel,
        grid=(T // tm,),
        in_specs=[row(D_MODEL), row(V_COLS), row(D_MODEL), row(D_MODEL),
                  const((V_COLS, D_MODEL)), const((D_MODEL, D_MODEL)), const((1, D_MODEL))],
        out_specs=[row(D_MODEL), pl.BlockSpec((D_MODEL, tm), lambda i: (0, i))],
        out_shape=[jax.ShapeDtypeStruct((T, D_MODEL), F32), jax.ShapeDtypeStruct((D_MODEL, T), BF16)],
        compiler_params=_cparams("parallel"),
        name="merge_out",
    )(x2d, o, gated_a, sig_b, wa_bf, wo_bf, g_ffn)


ROUTE_LC = 256
TAKEN = -3.0e38


def _split_bf16(x):
    hi = x.astype(BF16)
    return hi, (x - hi.astype(F32)).astype(BF16)


def _oddeven_merge(lo, hi, r):
    step = r * 2
    if step < hi - lo:
        yield from _oddeven_merge(lo, hi, step)
        yield from _oddeven_merge(lo + r, hi, step)
        yield from [(i, i + r) for i in range(lo + r, hi - r, step)]
    else:
        yield (lo, lo + r)


def _oddeven_merge_sort(lo, hi):
    if hi - lo >= 1:
        mid = lo + (hi - lo) // 2
        yield from _oddeven_merge_sort(lo, mid)
        yield from _oddeven_merge_sort(mid + 1, hi)
        yield from _oddeven_merge(lo, hi, 1)


def _top_rows_sorted(s, k):
    n = s.shape[0] // SUBLANES
    rows = [s[SUBLANES * g:SUBLANES * (g + 1), :] for g in range(n)]
    n_pow2 = 1 << (n - 1).bit_length()
    for i, j in _oddeven_merge_sort(0, n_pow2 - 1):
        if j < n:
            rows[i], rows[j] = jnp.maximum(rows[i], rows[j]), jnp.minimum(rows[i], rows[j])
    taken = jnp.full_like(rows[0], TAKEN)
    out = []
    for r in range(k):
        m = jnp.max(rows[0], axis=0, keepdims=True)
        out.append(m)
        hit = rows[0] == m
        depth = min(n, k - r - 1)
        for d in range(depth):
            rows[d] = jnp.where(hit, rows[d + 1] if d + 1 < n else taken, rows[d])
    return out


def _stack_rows(rows):
    n, w = len(rows), rows[0].shape[1]
    idx = lax.broadcasted_iota(jnp.int32, (n, w), 0)
    out = jnp.zeros((n, w), F32)
    for r, v in enumerate(rows):
        out = jnp.where(idx == r, v, out)
    return out


def _route_kernel(h2t_ref, wq_ref, keys_ref, thr_ref, s2_ref, e1_ref, e2_ref, s1_scr):
    tt = h2t_ref.shape[1]
    n_top = PEER_TOPK + 1
    qt = _dot(wq_ref[...], h2t_ref[...])
    halves = []
    for c in range(2):
        q_hi, q_lo = _split_bf16(qt[c * PEER_HALF:(c + 1) * PEER_HALF, :])
        k_hi, k_lo = _split_bf16(keys_ref[0, c])
        halves.append(_dot(k_hi, q_hi) + _dot(k_hi, q_lo) + _dot(k_lo, q_hi))
    s1_scr[...] = halves[0]
    s2_ref[0] = halves[1]

    width = min(tt, ROUTE_LC)
    for lc in range(tt // width):
        sl = slice(lc * width, (lc + 1) * width)
        s1 = s1_scr[:, sl]
        s2 = s2_ref[0, :, sl]
        a = _top_rows_sorted(s1, n_top)
        b = _top_rows_sorted(s2, n_top)
        taken = jnp.full_like(b[0], TAKEN)
        b_all = _stack_rows(b + [taken] * (3 * SUBLANES - n_top))
        tail = [a[i] + b[j] for i in range(4, n_top) for j in range(n_top // (i + 1))]
        tail += [taken] * (-len(tail) % SUBLANES)
        cand = jnp.concatenate([a[0] + b_all] + [a[i] + b_all[:SUBLANES] for i in range(1, 4)]
                               + [_stack_rows(tail)], axis=0)
        top = _top_rows_sorted(cand, n_top)
        z = jnp.zeros_like(top[0])
        for r in range(PEER_TOPK):
            z = z + jnp.exp(top[r] - top[0])
        tau = 0.5 * (top[PEER_TOPK - 1] + top[PEER_TOPK])
        grouped = (N_KEYS // SUBLANES, SUBLANES, width)
        thr_ref[0, :, :, sl] = (tau - s1).reshape(grouped)
        e1_ref[0, :, :, sl] = (jnp.exp(s1 - a[0]) * (0.5 / z)).reshape(grouped)
        e2_ref[0, :, sl] = jnp.exp(s2 - b[0])


def _route(h2t, wq_t_bf, keys, tt):
    T = h2t.shape[1]
    arr = jax.ShapeDtypeStruct((PEER_HEADS, N_KEYS, T), F32)
    spec = pl.BlockSpec((1, N_KEYS, tt), lambda i, h: (h, 0, i))
    n_grp = N_KEYS // SUBLANES
    garr = jax.ShapeDtypeStruct((PEER_HEADS, n_grp, SUBLANES, T), F32)
    gspec = pl.BlockSpec((1, n_grp, SUBLANES, tt), lambda i, h: (h, 0, 0, i))
    return pl.pallas_call(
        _route_kernel,
        grid=(T // tt, PEER_HEADS),
        in_specs=[
            pl.BlockSpec((D_MODEL, tt), lambda i, h: (0, i)),
            pl.BlockSpec((PEER_QUERY, D_MODEL), lambda i, h: (h, 0)),
            pl.BlockSpec((1, 2, N_KEYS, PEER_HALF), lambda i, h: (h, 0, 0, 0)),
        ],
        out_specs=[gspec, spec, gspec, spec],
        out_shape=[garr, arr, garr, arr],
        scratch_shapes=[pltpu.VMEM((N_KEYS, tt), F32)],
        compiler_params=_cparams("parallel", "arbitrary"),
        name="peer_route",
    )(h2t, wq_t_bf, keys)


PEER_SUB = 512
PEER_EB = 2 * PEER_SUB
PEER_ROWS = PEER_SUB // N_KEYS
N_SUB = N_EXPERTS // PEER_SUB
PEER_JH = 64


def _peer_kernel(h2t_ref, thr_ref, s2_ref, e1_ref, e2_ref, u_ref, v_ref, o_ref,
                 act0, act1, coef0, coef1, acc_scr):
    g = pl.program_id(1)
    tt = h2t_ref.shape[1]

    def stage_a(half, act):
        act[...] = _dot(u_ref[half * PEER_SUB:(half + 1) * PEER_SUB, :], h2t_ref[...])

    def stage_b(b, act, coef):
        b = jnp.clip(b, 0, N_SUB - 1)
        grp = b // 2
        odd = (b % 2) == 1
        for lc in range(tt // LANES):
            ln = slice(lc * LANES, (lc + 1) * LANES)
            for jh in range(N_KEYS // PEER_JH):
                js = slice(jh * PEER_JH, (jh + 1) * PEER_JH)
                gates = [jnp.zeros((PEER_JH, LANES), F32) for _ in range(PEER_ROWS)]
                for h in range(PEER_HEADS):
                    s2 = s2_ref[h, js, ln]
                    e2 = e2_ref[h, js, ln]
                    thr_grp = thr_ref[h, grp, :, ln]
                    e1_grp = e1_ref[h, grp, :, ln]
                    thr_rows = jnp.where(odd, thr_grp[PEER_ROWS:], thr_grp[:PEER_ROWS])
                    e1_rows = jnp.where(odd, e1_grp[PEER_ROWS:], e1_grp[:PEER_ROWS])
                    for r in range(PEER_ROWS):
                        gates[r] = gates[r] + jnp.where(s2 >= thr_rows[r:r + 1], e1_rows[r:r + 1] * e2, 0.0)
                for r in range(PEER_ROWS):
                    rows = slice(r * N_KEYS + jh * PEER_JH, r * N_KEYS + (jh + 1) * PEER_JH)
                    x = act[rows, ln]
                    coef[rows, ln] = gates[r] * (x * (1.0 + lax.erf(x * (2.0 ** -0.5))))

    def stage_c():
        coef = jnp.concatenate([coef0[...].astype(BF16), coef1[...].astype(BF16)], axis=0)
        acc_scr[...] += lax.dot_general(coef, v_ref[...], (((0,), (0,)), ((), ())), preferred_element_type=F32)

    last = pl.num_programs(1) - 1

    @pl.when(g == 0)
    def _():
        acc_scr[...] = jnp.zeros(acc_scr.shape, F32)
        stage_a(0, act0)
        stage_a(1, act1)
        stage_b(2 * g, act0, coef0)

    @pl.when(jnp.logical_and(g > 0, g < last))
    def _():
        stage_a(0, act0)
        stage_b(2 * g - 1, act1, coef1)
        stage_a(1, act1)
        stage_c()
        stage_b(2 * g, act0, coef0)

    @pl.when(g == last)
    def _():
        stage_b(2 * g - 1, act1, coef1)
        stage_c()
        o_ref[...] = acc_scr[...].astype(BF16)


def _peer(h2t, route, u_bf, v_bf, tt):
    T = h2t.shape[1]
    thr, s2, e1, e2 = route
    n_eb = N_EXPERTS // PEER_EB
    spec = pl.BlockSpec((PEER_HEADS, N_KEYS, tt), lambda i, g: (0, 0, i))
    gspec = pl.BlockSpec((PEER_HEADS, N_KEYS // SUBLANES, SUBLANES, tt), lambda i, g: (0, 0, 0, i))
    assert SUBLANES == 2 * PEER_ROWS
    return pl.pallas_call(
        _peer_kernel,
        grid=(T // tt, n_eb + 1),
        in_specs=[
            pl.BlockSpec((D_MODEL, tt), lambda i, g: (0, i)),
            gspec, spec, gspec, spec,
            pl.BlockSpec((PEER_EB, D_MODEL), lambda i, g: (jnp.minimum(g, n_eb - 1), 0)),
            pl.BlockSpec((PEER_EB, D_MODEL), lambda i, g: (jnp.maximum(g - 1, 0), 0)),
        ],
        out_specs=pl.BlockSpec((tt, D_MODEL), lambda i, g: (i, 0)),
        out_shape=jax.ShapeDtypeStruct((T, D_MODEL), BF16),
        scratch_shapes=[pltpu.VMEM((PEER_SUB, tt), F32), pltpu.VMEM((PEER_SUB, tt), F32),
                        pltpu.VMEM((PEER_SUB, tt), F32), pltpu.VMEM((PEER_SUB, tt), F32),
                        pltpu.VMEM((tt, D_MODEL), F32)],
        compiler_params=_cparams("parallel", "arbitrary"),
        name="peer_dense",
    )(h2t, thr, s2, e1, e2, u_bf, v_bf)


def _ple_kernel(x1_ref, po_ref, p_ref, wg_ref, wp_ref, gp_ref, gf_ref, y_ref):
    x2 = x1_ref[...] + po_ref[...].astype(F32)
    h3 = _rms(x2, gp_ref[...]).astype(BF16)
    gate = _sigmoid(_dot(h3, wg_ref[...]))
    proj = _dot(p_ref[...].astype(BF16), wp_ref[...])
    y_ref[...] = _rms(x2 + proj * gate, gf_ref[...])


def _ple(x1, peer_out, p2d, wg_bf, wp_bf, g_ple, g_final, tm):
    T = x1.shape[0]
    row = lambda n: pl.BlockSpec((tm, n), lambda i: (i, 0))
    const = lambda s: pl.BlockSpec(s, lambda i: (0, 0))
    return pl.pallas_call(
        _ple_kernel,
        grid=(T // tm,),
        in_specs=[row(D_MODEL), row(D_MODEL), row(PLE_DIM),
                  const((D_MODEL, D_MODEL)), const((PLE_DIM, D_MODEL)), const((1, D_MODEL)), const((1, D_MODEL))],
        out_specs=row(D_MODEL),
        out_shape=jax.ShapeDtypeStruct((T, D_MODEL), F32),
        compiler_params=_cparams("parallel"),
        name="ple_final",
    )(x1, peer_out, p2d, wg_bf, wp_bf, g_ple, g_final)


def _tile(n, pref):
    t = min(n, pref)
    assert n % t == 0, (n, t)
    return t


def _layer(x, p, h0_re, h0_im, k_past, v_past, lp, lambda_init):
    bn, seq, _ = x.shape
    past = k_past.shape[1]
    T = bn * seq
    x2d = x.reshape(T, D_MODEL)
    row = lambda a: a.astype(F32).reshape(1, -1)

    tm_in = _tile(T, 256)
    cos_t, sin_t = _rope_tables(seq, past, max(seq, tm_in))
    u, q, k_f32, k_bf, v_f32, v_bf, sig_a, sig_b = _inproj(
        x2d, row(lp['g_mix']), lp['w_in'].astype(BF16), cos_t, sin_t, tm_in)

    seg_len = _tile(seq // SUBLANES, 32)
    h0 = jnp.concatenate([h0_re.reshape(bn, 1, N_STATE), h0_im.reshape(bn, 1, N_STATE)], axis=-1).astype(F32)
    y_ssm, h_fin = _s5(u.reshape(bn, seq, D_SSM), h0, _s5_tables(lp, seg_len), seg_len)
    gated_a = _glu(y_ssm.reshape(T, D_SSM), lp['w_glu'].astype(BF16), sig_a, _tile(T, 1024))

    k_all = jnp.concatenate([k_past.reshape(bn, past, Q_COLS).astype(BF16), k_bf.reshape(bn, seq, Q_COLS)], axis=1)
    v_all = jnp.concatenate([v_past.reshape(bn, past, V_COLS).astype(BF16), v_bf.reshape(bn, seq, V_COLS)], axis=1)
    seq_q = max(seq, LANES // 2)
    q3 = jnp.pad(q.reshape(bn, seq, Q_COLS), ((0, 0), (0, seq_q - seq), (0, 0)))
    tq = _tile(seq_q, 512)
    tk = tq // 2 if past == 0 else past + seq
    o = _attention(q3, k_all, v_all, lp, tq, tk, past, lambda_init)[:, :seq]

    x1, h2t = _merge(x2d, o.reshape(T, V_COLS), gated_a, sig_b, lp['w_attn_out'].astype(BF16),
                    lp['w_out'].astype(BF16), row(lp['g_ffn']), _tile(T, 256))

    tt = _tile(T, 512)
    route = _route(h2t, lp['peer_w_q'].T.astype(BF16), lp['peer_keys'].astype(F32), tt)
    peer_out = _peer(h2t, route, lp['peer_u'].astype(BF16), lp['peer_v'].astype(BF16), tt)

    y = _ple(x1, peer_out, p.reshape(T, PLE_DIM), lp['w_ple_gate'].astype(BF16), lp['w_ple_proj'].astype(BF16),
             row(lp['g_ple']), row(lp['g_final']), _tile(T, 256))

    new_k = k_f32.reshape(bn, seq, N_HEADS, 2 * HEAD_DIM)
    new_v = v_f32.reshape(bn, seq, N_HEADS, V_DIM)
    hr = h_fin[:, 0, :N_STATE].reshape(bn, N_SSM_GROUPS, SSM_STATE)
    hi = h_fin[:, 0, N_STATE:].reshape(bn, N_SSM_GROUPS, SSM_STATE)
    return y.reshape(bn, seq, D_MODEL), new_k, new_v, hr, hi


def _trunk(x, p, h0_re, h0_im, k_past, v_past, lp):
    assert DEPTH == 1
    lambda_init = 0.8 - 0.6 * math.exp(-0.3 * 0)
    y, k_new, v_new, hr, hi = _layer(x, p[0], h0_re[0], h0_im[0], k_past[0], v_past[0], lp, lambda_init)
    return y, k_new[None], v_new[None], hr[None], hi[None]


def kernel(x_prompt, x_sample, p_prompt, p_sample, cache_k, cache_v, state_ssm_re, state_ssm_im,
           g_mix_norm, w_in, ssm_lambda_re, ssm_lambda_im, ssm_log_step, ssm_b_re, ssm_b_im,
           ssm_c_re, ssm_c_im, ssm_d, w_glu, diff_lambda_q1, diff_lambda_k1, diff_lambda_q2,
           diff_lambda_k2, g_subln, w_attn_out, w_out, g_ffn_norm, peer_w_q, peer_keys, peer_u,
           peer_v, g_ple_norm, w_ple_gate, w_ple_proj, g_final):
    lp = dict(g_mix=g_mix_norm[0], w_in=w_in[0], lam_re=ssm_lambda_re[0], lam_im=ssm_lambda_im[0],
              log_step=ssm_log_step[0], b_re=ssm_b_re[0], b_im=ssm_b_im[0], c_re=ssm_c_re[0],
              c_im=ssm_c_im[0], d=ssm_d[0], w_glu=w_glu[0], lq1=diff_lambda_q1[0],
              lk1=diff_lambda_k1[0], lq2=diff_lambda_q2[0], lk2=diff_lambda_k2[0],
              g_subln=g_subln[0], w_attn_out=w_attn_out[0], w_out=w_out[0], g_ffn=g_ffn_norm[0],
              peer_w_q=peer_w_q[0], peer_keys=peer_keys[0], peer_u=peer_u[0], peer_v=peer_v[0],
              g_ple=g_ple_norm[0], w_ple_gate=w_ple_gate[0], w_ple_proj=w_ple_proj[0], g_final=g_final)
    bn = x_prompt.shape[0]
    zeros_state = jnp.zeros((DEPTH, bn, N_SSM_GROUPS, SSM_STATE), F32)
    k_none = jnp.zeros((DEPTH, bn, 0, N_HEADS, 2 * HEAD_DIM), x_prompt.dtype)
    v_none = jnp.zeros((DEPTH, bn, 0, N_HEADS, V_DIM), x_prompt.dtype)
    y_p, k_p, v_p, r_p, i_p = _trunk(x_prompt, p_prompt, zeros_state, zeros_state, k_none, v_none, lp)
    y_s, k_s, v_s, r_s, i_s = _trunk(x_sample, p_sample, state_ssm_re, state_ssm_im, cache_k, cache_v, lp)
    return (y_p, y_s, k_p, v_p, r_p, i_p, k_s, v_s, r_s, i_s)
```

```python
import functools
import math

import jax
import jax.numpy as jnp
from jax import lax
from jax.experimental import pallas as pl
from jax.experimental.pallas import tpu as pltpu

F32 = jnp.float32
BF16 = jnp.bfloat16

D_MODEL = 2048
DEPTH = 1
CHUNK = 64
CHUNK_SHIFT = CHUNK.bit_length() - 1
assert 1 << CHUNK_SHIFT == CHUNK
PLE_DIM = 256
RMS_EPS = 1e-6
NEG_INF = -1e30
D_SSM = 1024
SSM_GROUP = 16
N_SSM_GROUPS = D_SSM // SSM_GROUP
SSM_STATE = 64
N_STATE = N_SSM_GROUPS * SSM_STATE
N_HEADS = 8
HEAD_DIM = 64
V_DIM = 2 * HEAD_DIM
ROPE_THETA = 10000.0
Q_COLS = N_HEADS * 2 * HEAD_DIM
V_COLS = N_HEADS * V_DIM
IN_COLS = D_SSM + 2 * Q_COLS + V_COLS + 2 * D_MODEL
PEER_HEADS = 8
N_KEYS = 128
N_EXPERTS = N_KEYS * N_KEYS
PEER_QUERY = 256
PEER_HALF = PEER_QUERY // 2
PEER_TOPK = 16

LANES = 128
SUBLANES = 8
VMEM_LIMIT = 56 * 1024 * 1024


def _cparams(*sem):
    return pltpu.CompilerParams(dimension_semantics=sem, vmem_limit_bytes=VMEM_LIMIT)


def _rms(x, g):
    return x * lax.rsqrt(jnp.mean(x * x, axis=-1, keepdims=True) + RMS_EPS) * g


def _gelu(x):
    return 0.5 * x * (1.0 + lax.erf(x * (2.0 ** -0.5)))


def _sigmoid(x):
    return 1.0 / (1.0 + jnp.exp(-x))


def _dot(a, b):
    return jnp.dot(a, b, preferred_element_type=F32)


IN_TN = 512
Q_SCALE = HEAD_DIM ** -0.5 * math.log2(math.e)
IN_MIX_COLS = D_SSM + 2 * Q_COLS + V_COLS


def _column_tiles(h_scr, w_ref):
    return [functools.partial(lambda c: _dot(h_scr[...], w_ref[:, c * IN_TN:(c + 1) * IN_TN]), c)
            for c in range(w_ref.shape[1] // IN_TN)]


def _inproj_mix_kernel(x_ref, g_ref, w_ref, cos_ref, sin_ref, u_ref, q_ref, kf_ref, kb_ref, vf_ref, vb_ref, h_scr):
    h_scr[...] = _rms(x_ref[...], g_ref[...]).astype(BF16)
    z = _column_tiles(h_scr, w_ref)

    def rope(t):
        n = t.shape[-1]
        lane = lax.broadcasted_iota(jnp.int32, t.shape, 1)
        first = jnp.bitwise_and(lane, HEAD_DIM - 1) < (HEAD_DIM // 2)
        partner = jnp.where(first, pltpu.roll(t, n - HEAD_DIM // 2, 1), pltpu.roll(t, HEAD_DIM // 2, 1))
        reps = n // LANES
        return t * jnp.tile(cos_ref[...], (1, reps)) + partner * jnp.tile(sin_ref[...], (1, reps))

    per = Q_COLS // IN_TN
    for c in range(per):
        cols = slice(c * IN_TN, (c + 1) * IN_TN)
        u_ref[:, cols] = z[c]().astype(BF16)
        q_ref[:, cols] = (rope(z[per + c]()) * Q_SCALE).astype(BF16)
        k = rope(z[2 * per + c]())
        kf_ref[:, cols] = k
        kb_ref[:, cols] = k.astype(BF16)
        v = z[3 * per + c]()
        vf_ref[:, cols] = v
        vb_ref[:, cols] = v.astype(BF16)


def _inproj_gate_kernel(x_ref, g_ref, w_ref, ga_ref, gb_ref, h_scr):
    h_scr[...] = _rms(x_ref[...], g_ref[...]).astype(BF16)
    z = _column_tiles(h_scr, w_ref)
    per = D_MODEL // IN_TN
    for c in range(per):
        cols = slice(c * IN_TN, (c + 1) * IN_TN)
        ga_ref[:, cols] = _sigmoid(z[c]()).astype(BF16)
        gb_ref[:, cols] = _sigmoid(z[per + c]()).astype(BF16)


def _inproj(x2d, g, w_bf, cos_t, sin_t, tm):
    assert D_SSM == Q_COLS == V_COLS
    T = x2d.shape[0]
    n_tab = cos_t.shape[0] // tm
    row = lambda n: pl.BlockSpec((tm, n), lambda i: (i, 0))
    const = lambda s: pl.BlockSpec(s, lambda i: (0, 0))
    table = pl.BlockSpec((tm, LANES), lambda i: (i % n_tab, 0))
    arr = lambda n, dt: jax.ShapeDtypeStruct((T, n), dt)
    common = dict(grid=(T // tm,), scratch_shapes=[pltpu.VMEM((tm, D_MODEL), BF16)],
                  compiler_params=_cparams("parallel"))
    u, q, kf, kb, vf, vb = pl.pallas_call(
        _inproj_mix_kernel,
        in_specs=[row(D_MODEL), const((1, D_MODEL)), const((D_MODEL, IN_MIX_COLS)), table, table],
        out_specs=[row(Q_COLS)] * 6,
        out_shape=[arr(Q_COLS, BF16), arr(Q_COLS, BF16), arr(Q_COLS, F32), arr(Q_COLS, BF16),
                   arr(Q_COLS, F32), arr(Q_COLS, BF16)],
        name="inproj_mix", **common,
    )(x2d, g, w_bf[:, :IN_MIX_COLS], cos_t, sin_t)
    ga, gb = pl.pallas_call(
        _inproj_gate_kernel,
        in_specs=[row(D_MODEL), const((1, D_MODEL)), const((D_MODEL, 2 * D_MODEL))],
        out_specs=[row(D_MODEL)] * 2,
        out_shape=[arr(D_MODEL, BF16)] * 2,
        name="inproj_gate", **common,
    )(x2d, g, w_bf[:, IN_MIX_COLS:])
    return u, q, kf, kb, vf, vb, ga, gb


def _rope_tables(seq, past, rows):
    half = HEAD_DIM // 2
    inv = ROPE_THETA ** (-jnp.arange(half, dtype=F32) * 2.0 / HEAD_DIM)
    pos = (past + (jnp.arange(rows, dtype=jnp.int32) % seq)).astype(F32)
    ang = pos[:, None] * inv[None, :]
    cos, sin = jnp.cos(ang), jnp.sin(ang)
    reps = LANES // HEAD_DIM
    cos_t = jnp.tile(jnp.concatenate([cos, cos], axis=-1), (1, reps))
    sin_t = jnp.tile(jnp.concatenate([-sin, sin], axis=-1), (1, reps))
    return cos_t, sin_t


S5_COLS = 512
S5_UBLK = LANES
S5_N_UBLK = D_SSM // S5_UBLK
S5_XBLK = (S5_UBLK // SSM_GROUP) * SSM_STATE
S5_YBLK = 256
S5_N_YBLK = D_SSM // S5_YBLK
S5_HBLK = (S5_YBLK // SSM_GROUP) * SSM_STATE


def _s5_kernel(u_ref, h0_ref, perm_ref, permt_ref, wx_ref, apow_ref, cr_ref, ci_ref, d_ref,
               y_ref, hfin_ref, x_scr, c_scr, carry_scr, *, seg_len):
    t = pl.program_id(1)
    rows = SUBLANES * seg_len
    n = N_STATE

    @pl.when(t == 0)
    def _():
        carry_scr[...] = h0_ref[...]

    up = _dot(perm_ref[...], u_ref[...])
    upb = up.astype(BF16)
    for a in range(S5_N_UBLK):
        xa = _dot(upb[:, a * S5_UBLK:(a + 1) * S5_UBLK], wx_ref[a])
        x_scr[:, a * S5_XBLK:(a + 1) * S5_XBLK] = xa[:, :S5_XBLK]
        x_scr[:, n + a * S5_XBLK:n + (a + 1) * S5_XBLK] = xa[:, S5_XBLK:]

    for c in range(n // S5_COLS):
        lo = c * S5_COLS
        ar = jnp.broadcast_to(apow_ref[0:1, lo:lo + S5_COLS], (SUBLANES, S5_COLS))
        ai = jnp.broadcast_to(apow_ref[0:1, n + lo:n + lo + S5_COLS], (SUBLANES, S5_COLS))

        def scan_step(k, carry, lo=lo, ar=ar, ai=ai):
            hr, hi = carry
            r0 = pl.multiple_of(k * SUBLANES, SUBLANES)
            xr = x_scr[pl.ds(r0, SUBLANES), lo:lo + S5_COLS]
            xi = x_scr[pl.ds(r0, SUBLANES), n + lo:n + lo + S5_COLS]
            nr = ar * hr - ai * hi + xr
            ni = ar * hi + ai * hr + xi
            x_scr[pl.ds(r0, SUBLANES), lo:lo + S5_COLS] = nr
            x_scr[pl.ds(r0, SUBLANES), n + lo:n + lo + S5_COLS] = ni
            return nr, ni

        zero = jnp.zeros((SUBLANES, S5_COLS), F32)
        lax.fori_loop(0, seg_len, scan_step, (zero, zero))

    alr = apow_ref[seg_len - 1:seg_len, :n]
    ali = apow_ref[seg_len - 1:seg_len, n:]
    cr = carry_scr[:, :n]
    ci = carry_scr[:, n:]
    for s in range(SUBLANES):
        c_scr[s:s + 1, :n] = cr
        c_scr[s:s + 1, n:] = ci
        lr = x_scr[rows - SUBLANES + s:rows - SUBLANES + s + 1, :n]
        li = x_scr[rows - SUBLANES + s:rows - SUBLANES + s + 1, n:]
        cr, ci = alr * cr - ali * ci + lr, alr * ci + ali * cr + li
    carry_scr[:, :n] = cr
    carry_scr[:, n:] = ci

    for c in range(n // S5_COLS):
        lo = c * S5_COLS
        sr = c_scr[:, lo:lo + S5_COLS]
        si = c_scr[:, n + lo:n + lo + S5_COLS]

        def fix_step(k, _, lo=lo, sr=sr, si=si):
            r0 = pl.multiple_of(k * SUBLANES, SUBLANES)
            pr = apow_ref[pl.ds(k, 1), lo:lo + S5_COLS]
            pi = apow_ref[pl.ds(k, 1), n + lo:n + lo + S5_COLS]
            x_scr[pl.ds(r0, SUBLANES), lo:lo + S5_COLS] += pr * sr - pi * si
            x_scr[pl.ds(r0, SUBLANES), n + lo:n + lo + S5_COLS] += pr * si + pi * sr
            return 0

        lax.fori_loop(0, seg_len, fix_step, 0)

    @pl.when(t == pl.num_programs(1) - 1)
    def _():
        hfin_ref[...] = carry_scr[...]

    for j in range(S5_N_YBLK):
        hr = x_scr[:, j * S5_HBLK:(j + 1) * S5_HBLK].astype(BF16)
        hi = x_scr[:, n + j * S5_HBLK:n + (j + 1) * S5_HBLK].astype(BF16)
        yj = (_dot(hr, cr_ref[j]) + _dot(hi, ci_ref[j])
              + d_ref[:, j * S5_YBLK:(j + 1) * S5_YBLK] * up[:, j * S5_YBLK:(j + 1) * S5_YBLK])
        yj = _gelu(yj).astype(BF16)
        y_ref[:, j * S5_YBLK:(j + 1) * S5_YBLK] = _dot(permt_ref[...], yj).astype(BF16)


def _s5_tables(lp, seg_len):
    g, p, c = N_SSM_GROUPS, SSM_STATE, SSM_GROUP
    dt = jnp.exp(lp['log_step'].astype(F32))[:, None]
    lr, li = lp['lam_re'].astype(F32), lp['lam_im'].astype(F32)
    mag = jnp.exp(lr * dt)
    ar, ai = mag * jnp.cos(li * dt), mag * jnp.sin(li * dt)
    den = lr * lr + li * li
    fr = ((ar - 1.0) * lr + ai * li) / den
    fi = (ai * lr - (ar - 1.0) * li) / den
    br, bi = lp['b_re'].astype(F32), lp['b_im'].astype(F32)
    bbr = fr[..., None] * br - fi[..., None] * bi
    bbi = fr[..., None] * bi + fi[..., None] * br

    gpb = S5_UBLK // c
    eye = jnp.eye(gpb, dtype=F32)

    def in_blocks(bb):
        bb = bb.reshape(S5_N_UBLK, gpb, p, c)
        return jnp.einsum('xy,axpc->axcyp', eye, bb).reshape(S5_N_UBLK, S5_UBLK, S5_XBLK)

    wx = jnp.concatenate([in_blocks(bbr), in_blocks(bbi)], axis=-1).astype(BF16)

    pr, pi = ar.reshape(1, g * p), ai.reshape(1, g * p)
    while pr.shape[0] < seg_len:
        tr, ti = pr[-1:], pi[-1:]
        pr, pi = (jnp.concatenate([pr, pr * tr - pi * ti], axis=0),
                  jnp.concatenate([pi, pr * ti + pi * tr], axis=0))
    apow = jnp.concatenate([pr[:seg_len], pi[:seg_len]], axis=-1)

    gpy = S5_YBLK // c
    eye_y = jnp.eye(gpy, dtype=F32)

    def out_blocks(cm):
        cm = cm.astype(F32).reshape(S5_N_YBLK, gpy, c, p)
        return jnp.einsum('xy,jxcp->jxpyc', eye_y, cm).reshape(S5_N_YBLK, S5_HBLK, S5_YBLK)

    cr = out_blocks(lp['c_re']).astype(BF16)
    ci = (-out_blocks(lp['c_im'])).astype(BF16)
    d = lp['d'].astype(F32).reshape(1, D_SSM)
    return wx, apow, cr, ci, d


def _s5(u, h0, tables, seg_len):
    bn, seq, _ = u.shape
    rows = SUBLANES * seg_len
    wx, apow, cr, ci, d = tables
    r = jnp.arange(rows)
    perm = (r[None, :] == ((r % SUBLANES) * seg_len + r // SUBLANES)[:, None]).astype(BF16)
    const2 = lambda b, t: (0, 0)
    const3 = lambda b, t: (0, 0, 0)
    return pl.pallas_call(
        functools.partial(_s5_kernel, seg_len=seg_len),
        grid=(bn, seq // rows),
        in_specs=[
            pl.BlockSpec((None, rows, D_SSM), lambda b, t: (b, t, 0)),
            pl.BlockSpec((None, 1, 2 * N_STATE), lambda b, t: (b, 0, 0)),
            pl.BlockSpec((rows, rows), const2),
            pl.BlockSpec((rows, rows), const2),
            pl.BlockSpec(wx.shape, const3),
            pl.BlockSpec(apow.shape, const2),
            pl.BlockSpec(cr.shape, const3),
            pl.BlockSpec(ci.shape, const3),
            pl.BlockSpec((1, D_SSM), const2),
        ],
        out_specs=[
            pl.BlockSpec((None, rows, D_SSM), lambda b, t: (b, t, 0)),
            pl.BlockSpec((None, 1, 2 * N_STATE), lambda b, t: (b, 0, 0)),
        ],
        out_shape=[jax.ShapeDtypeStruct((bn, seq, D_SSM), BF16),
                   jax.ShapeDtypeStruct((bn, 1, 2 * N_STATE), F32)],
        scratch_shapes=[pltpu.VMEM((rows, 2 * N_STATE), F32),
                        pltpu.VMEM((SUBLANES, 2 * N_STATE), F32),
                        pltpu.VMEM((1, 2 * N_STATE), F32)],
        compiler_params=_cparams("parallel", "arbitrary"),
        name="s5",
    )(u, h0, perm, perm.T, wx, apow, cr, ci, d)


GLU_TN = 512


def _glu_kernel(y_ref, w_ref, ga_ref, o_ref):
    y = y_ref[...]
    for c in range(D_MODEL // GLU_TN):
        cols = slice(c * GLU_TN, (c + 1) * GLU_TN)
        a = _dot(y, w_ref[:, cols])
        b = _dot(y, w_ref[:, D_MODEL + c * GLU_TN:D_MODEL + (c + 1) * GLU_TN])
        o_ref[:, cols] = (ga_ref[:, cols].astype(F32) * a * _sigmoid(b)).astype(BF16)


def _glu(y, w_glu_bf, sig_a, tm):
    T = y.shape[0]
    return pl.pallas_call(
        _glu_kernel,
        grid=(T // tm,),
        in_specs=[
            pl.BlockSpec((tm, D_SSM), lambda i: (i, 0)),
            pl.BlockSpec((D_SSM, 2 * D_MODEL), lambda i: (0, 0)),
            pl.BlockSpec((tm, D_MODEL), lambda i: (i, 0)),
        ],
        out_specs=pl.BlockSpec((tm, D_MODEL), lambda i: (i, 0)),
        out_shape=jax.ShapeDtypeStruct((T, D_MODEL), BF16),
        compiler_params=_cparams("parallel"),
        name="glu",
    )(y, w_glu_bf, sig_a)


def _attn_kernel(q_ref, k_ref, v_ref, lq1_ref, lk1_ref, lq2_ref, lk2_ref, g_ref, o_ref,
                 qt_scr, s0_scr, s1_scr, m_scr, l_scr, acc_scr, *, tq, tk, past, paired, lambda_init):
    i = pl.program_id(2)
    nk = k_ref.shape[0] // tk
    q = q_ref[...].astype(F32)
    lane = lax.broadcasted_iota(jnp.int32, q.shape, 1)
    qs = jnp.concatenate([jnp.where(lane < HEAD_DIM, q, 0.0), jnp.where(lane >= HEAD_DIM, q, 0.0)], axis=0)
    qt_scr[...] = qs.T.astype(BF16)

    m_scr[...] = jnp.full(m_scr.shape, NEG_INF, F32)
    l_scr[...] = jnp.zeros(l_scr.shape, F32)
    acc_scr[...] = jnp.zeros(acc_scr.shape, F32)

    q_lo = past + i * tq
    min_qc = q_lo // CHUNK
    max_qc = (q_lo + tq - 1) // CHUNK
    n_proc = jnp.minimum(nk, (max_qc * CHUNK + CHUNK - 1) // tk + 1)
    n_full = jnp.clip((min_qc * CHUNK + CHUNK) // tk, 0, n_proc)

    def scores(j, s_ref):
        k0 = pl.multiple_of(j * tk, tk)
        s_ref[...] = _dot(k_ref[pl.ds(k0, tk), :], qt_scr[...])

    def update(j, s_ref, cols):
        k0 = pl.multiple_of(j * tk, tk)
        vb = v_ref[pl.ds(k0, tk), :]
        for lo, hi, masked in cols:
            s = s_ref[:, lo:hi]
            if masked:
                kpos = k0 + lax.broadcasted_iota(jnp.int32, s.shape, 0)
                qpos = q_lo + jnp.bitwise_and(lo + lax.broadcasted_iota(jnp.int32, s.shape, 1), tq - 1)
                s = jnp.where(jnp.right_shift(kpos, CHUNK_SHIFT) <= jnp.right_shift(qpos, CHUNK_SHIFT), s, NEG_INF)
            m_old = m_scr[:, lo:hi]
            m_new = jnp.maximum(m_old, jnp.max(s, axis=0, keepdims=True))
            alpha = jnp.exp2(m_old - m_new)
            p = jnp.exp2(s - m_new)
            l_scr[:, lo:hi] = alpha * l_scr[:, lo:hi] + jnp.sum(p, axis=0, keepdims=True)
            pv = lax.dot_general(vb, p.astype(BF16), (((0,), (0,)), ((), ())), preferred_element_type=F32)
            acc_scr[:, lo:hi] = alpha * acc_scr[:, lo:hi] + pv
            m_scr[:, lo:hi] = m_new

    everything = ((0, 2 * tq, False),)
    everything_masked = ((0, 2 * tq, True),)

    if paired:
        scores(0, s0_scr)

        def pair(p, c):
            scores(2 * p + 1, s1_scr)
            update(2 * p, s0_scr, everything)
            scores(2 * p + 2, s0_scr)
            update(2 * p + 1, s1_scr, everything)
            return c

        lax.fori_loop(0, i, pair, 0)
        scores(2 * i + 1, s1_scr)
        update(2 * i, s0_scr, tuple(r for c0 in (0, tq) for r in ((c0, c0 + tk, True), (c0 + tk, c0 + tq, False))))
        update(2 * i + 1, s1_scr, tuple((c0 + tk, c0 + tq, True) for c0 in (0, tq)))
    else:
        def full_body(j, c):
            scores(j, s0_scr)
            update(j, s0_scr, everything)
            return c

        def masked_body(j, c):
            scores(j, s0_scr)
            update(j, s0_scr, everything_masked)
            return c

        lax.fori_loop(0, n_full, full_body, 0)
        lax.fori_loop(n_full, n_proc, masked_body, 0)

    lam = (jnp.exp(jnp.sum(lq1_ref[...] * lk1_ref[...], axis=-1, keepdims=True))
           - jnp.exp(jnp.sum(lq2_ref[...] * lk2_ref[...], axis=-1, keepdims=True)) + lambda_init)
    ot = acc_scr[:, :tq] / l_scr[:, :tq] - lam * (acc_scr[:, tq:] / l_scr[:, tq:])
    o_ref[...] = (_rms(ot.T, g_ref[...]) * (1.0 - lambda_init)).astype(BF16)


def _attention(q, k, v, lp, tq, tk, past, lambda_init):
    bn, seq, _ = q.shape
    lk = k.shape[1]
    vec = lambda a: a.astype(F32).reshape(1, -1)
    small = lambda n: pl.BlockSpec((1, n), lambda b, h, i: (0, 0))
    paired = past == 0 and tq == 2 * tk and lk == seq
    s_scr = pltpu.VMEM((tk, 2 * tq), F32)
    return pl.pallas_call(
        functools.partial(_attn_kernel, tq=tq, tk=tk, past=past, paired=paired, lambda_init=lambda_init),
        grid=(bn, N_HEADS, seq // tq),
        in_specs=[
            pl.BlockSpec((None, tq, V_DIM), lambda b, h, i: (b, i, h)),
            pl.BlockSpec((None, lk, V_DIM), lambda b, h, i: (b, 0, h)),
            pl.BlockSpec((None, lk, V_DIM), lambda b, h, i: (b, 0, h)),
            small(HEAD_DIM), small(HEAD_DIM), small(HEAD_DIM), small(HEAD_DIM), small(V_DIM),
        ],
        out_specs=pl.BlockSpec((None, tq, V_DIM), lambda b, h, i: (b, i, h)),
        out_shape=jax.ShapeDtypeStruct((bn, seq, V_COLS), BF16),
        scratch_shapes=[pltpu.VMEM((V_DIM, 2 * tq), BF16), s_scr, s_scr, pltpu.VMEM((1, 2 * tq), F32),
                        pltpu.VMEM((1, 2 * tq), F32), pltpu.VMEM((V_DIM, 2 * tq), F32)],
        compiler_params=_cparams("parallel", "parallel", "arbitrary"),
        name="diff_attn",
    )(q, k, v, vec(lp['lq1']), vec(lp['lk1']), vec(lp['lq2']), vec(lp['lk2']), vec(lp['g_subln']))


def _merge_kernel(x_ref, o_ref, ga_ref, sb_ref, wa_ref, wo_ref, g_ref, x1_ref, h2t_ref):
    branch_b = _dot(o_ref[...], wa_ref[...])
    merged = ga_ref[...].astype(F32) + sb_ref[...].astype(F32) * branch_b
    x1 = x_ref[...] + _dot(merged.astype(BF16), wo_ref[...])
    x1_ref[...] = x1
    h2t_ref[...] = _rms(x1, g_ref[...]).T.astype(BF16)


def _merge(x2d, o, gated_a, sig_b, wa_bf, wo_bf, g_ffn, tm):
    T = x2d.shape[0]
    row = lambda n: pl.BlockSpec((tm, n), lambda i: (i, 0))
    const = lambda s: pl.BlockSpec(s, lambda i: (0, 0))
    return pl.pallas_call(
        _merge_kernel,
        grid=(T // tm,),
        in_specs=[row(D_MODEL), row(V_COLS), row(D_MODEL), row(D_MODEL),
                  const((V_COLS, D_MODEL)), const((D_MODEL, D_MODEL)), const((1, D_MODEL))],
        out_specs=[row(D_MODEL), pl.BlockSpec((D_MODEL, tm), lambda i: (0, i))],
        out_shape=[jax.ShapeDtypeStruct((T, D_MODEL), F32), jax.ShapeDtypeStruct((D_MODEL, T), BF16)],
        compiler_params=_cparams("parallel"),
        name="merge_out",
    )(x2d, o, gated_a, sig_b, wa_bf, wo_bf, g_ffn)


ROUTE_LC = 256
TAKEN = -3.0e38


def _split_bf16(x):
    hi = x.astype(BF16)
    return hi, (x - hi.astype(F32)).astype(BF16)


def _oddeven_merge(lo, hi, r):
    step = r * 2
    if step < hi - lo:
        yield from _oddeven_merge(lo, hi, step)
        yield from _oddeven_merge(lo + r, hi, step)
        yield from [(i, i + r) for i in range(lo + r, hi - r, step)]
    else:
        yield (lo, lo + r)


def _oddeven_merge_sort(lo, hi):
    if hi - lo >= 1:
        mid = lo + (hi - lo) // 2
        yield from _oddeven_merge_sort(lo, mid)
        yield from _oddeven_merge_sort(mid + 1, hi)
        yield from _oddeven_merge(lo, hi, 1)


def _top_rows_sorted(s, k):
    n = s.shape[0] // SUBLANES
    rows = [s[SUBLANES * g:SUBLANES * (g + 1), :] for g in range(n)]
    n_pow2 = 1 << (n - 1).bit_length()
    for i, j in _oddeven_merge_sort(0, n_pow2 - 1):
        if j < n:
            rows[i], rows[j] = jnp.maximum(rows[i], rows[j]), jnp.minimum(rows[i], rows[j])
    taken = jnp.full_like(rows[0], TAKEN)
    out = []
    for r in range(k):
        m = jnp.max(rows[0], axis=0, keepdims=True)
        out.append(m)
        hit = rows[0] == m
        depth = min(n, k - r - 1)
        for d in range(depth):
            rows[d] = jnp.where(hit, rows[d + 1] if d + 1 < n else taken, rows[d])
    return out


def _stack_rows(rows):
    n, w = len(rows), rows[0].shape[1]
    idx = lax.broadcasted_iota(jnp.int32, (n, w), 0)
    out = jnp.zeros((n, w), F32)
    for r, v in enumerate(rows):
        out = jnp.where(idx == r, v, out)
    return out


def _route_kernel(h2t_ref, wq_ref, keys_ref, thr_ref, s2_ref, e1_ref, e2_ref, s1_scr):
    for h in range(PEER_HEADS):
        _route_head(h, h2t_ref, wq_ref, keys_ref, thr_ref, s2_ref, e1_ref, e2_ref, s1_scr)


def _route_head(h, h2t_ref, wq_ref, keys_ref, thr_ref, s2_ref, e1_ref, e2_ref, s1_scr):
    tt = h2t_ref.shape[1]
    n_top = PEER_TOPK + 1
    qt = _dot(wq_ref[h * PEER_QUERY:(h + 1) * PEER_QUERY, :], h2t_ref[...])
    halves = []
    for c in range(2):
        q_hi, q_lo = _split_bf16(qt[c * PEER_HALF:(c + 1) * PEER_HALF, :])
        k_hi, k_lo = _split_bf16(keys_ref[h, c])
        halves.append(_dot(k_hi, q_hi) + _dot(k_hi, q_lo) + _dot(k_lo, q_hi))
    s1_scr[h] = halves[0]
    s2_ref[h] = halves[1]

    width = min(tt, ROUTE_LC)
    for lc in range(tt // width):
        sl = slice(lc * width, (lc + 1) * width)
        s1 = s1_scr[h, :, sl]
        s2 = s2_ref[h, :, sl]
        a = _top_rows_sorted(s1, n_top)
        b = _top_rows_sorted(s2, n_top)
        taken = jnp.full_like(b[0], TAKEN)
        b_all = _stack_rows(b + [taken] * (3 * SUBLANES - n_top))
        tail = [a[i] + b[j] for i in range(4, n_top) for j in range(n_top // (i + 1))]
        tail += [taken] * (-len(tail) % SUBLANES)
        cand = jnp.concatenate([a[0] + b_all] + [a[i] + b_all[:SUBLANES] for i in range(1, 4)]
                               + [_stack_rows(tail)], axis=0)
        top = _top_rows_sorted(cand, n_top)
        z = jnp.zeros_like(top[0])
        for r in range(PEER_TOPK):
            z = z + jnp.exp(top[r] - top[0])
        tau = 0.5 * (top[PEER_TOPK - 1] + top[PEER_TOPK])
        grouped = (N_KEYS // SUBLANES, SUBLANES, width)
        thr_ref[h, :, :, sl] = (tau - s1).reshape(grouped)
        e1_ref[h, :, :, sl] = (jnp.exp(s1 - a[0]) * (0.5 / z)).reshape(grouped)
        e2_ref[h, :, sl] = jnp.exp(s2 - b[0])


def _route(h2t, wq_t_bf, keys, tt):
    T = h2t.shape[1]
    arr = jax.ShapeDtypeStruct((PEER_HEADS, N_KEYS, T), F32)
    spec = pl.BlockSpec((PEER_HEADS, N_KEYS, tt), lambda i: (0, 0, i))
    n_grp = N_KEYS // SUBLANES
    garr = jax.ShapeDtypeStruct((PEER_HEADS, n_grp, SUBLANES, T), F32)
    gspec = pl.BlockSpec((PEER_HEADS, n_grp, SUBLANES, tt), lambda i: (0, 0, 0, i))
    return pl.pallas_call(
        _route_kernel,
        grid=(T // tt,),
        in_specs=[
            pl.BlockSpec((D_MODEL, tt), lambda i: (0, i)),
            pl.BlockSpec((PEER_HEADS * PEER_QUERY, D_MODEL), lambda i: (0, 0)),
            pl.BlockSpec((PEER_HEADS, 2, N_KEYS, PEER_HALF), lambda i: (0, 0, 0, 0)),
        ],
        out_specs=[gspec, spec, gspec, spec],
        out_shape=[garr, arr, garr, arr],
        scratch_shapes=[pltpu.VMEM((PEER_HEADS, N_KEYS, tt), F32)],
        compiler_params=_cparams("parallel"),
        name="peer_route",
    )(h2t, wq_t_bf, keys)


PEER_SUB = 512
PEER_EB = 2 * PEER_SUB
PEER_ROWS = PEER_SUB // N_KEYS
N_SUB = N_EXPERTS // PEER_SUB
PEER_JH = 64


def _peer_kernel(h2t_ref, thr_ref, s2_ref, e1_ref, e2_ref, u_ref, v_ref, o_ref,
                 act0, act1, coef0, coef1, acc_scr):
    g = pl.program_id(1)
    tt = h2t_ref.shape[1]

    def stage_a(half, act):
        act[...] = _dot(u_ref[half * PEER_SUB:(half + 1) * PEER_SUB, :], h2t_ref[...])

    def stage_b(b, act, coef):
        b = jnp.clip(b, 0, N_SUB - 1)
        grp = b // 2
        odd = (b % 2) == 1
        for lc in range(tt // LANES):
            ln = slice(lc * LANES, (lc + 1) * LANES)
            for jh in range(N_KEYS // PEER_JH):
                js = slice(jh * PEER_JH, (jh + 1) * PEER_JH)
                gates = [jnp.zeros((PEER_JH, LANES), F32) for _ in range(PEER_ROWS)]
                for h in range(PEER_HEADS):
                    s2 = s2_ref[h, js, ln]
                    e2 = e2_ref[h, js, ln]
                    thr_grp = thr_ref[h, grp, :, ln]
                    e1_grp = e1_ref[h, grp, :, ln]
                    thr_rows = jnp.where(odd, thr_grp[PEER_ROWS:], thr_grp[:PEER_ROWS])
                    e1_rows = jnp.where(odd, e1_grp[PEER_ROWS:], e1_grp[:PEER_ROWS])
                    for r in range(PEER_ROWS):
                        gates[r] = gates[r] + jnp.where(s2 >= thr_rows[r:r + 1], e1_rows[r:r + 1] * e2, 0.0)
                for r in range(PEER_ROWS):
                    rows = slice(r * N_KEYS + jh * PEER_JH, r * N_KEYS + (jh + 1) * PEER_JH)
                    x = act[rows, ln]
                    coef[rows, ln] = gates[r] * (x * (1.0 + lax.erf(x * (2.0 ** -0.5))))

    def stage_c():
        coef = jnp.concatenate([coef0[...].astype(BF16), coef1[...].astype(BF16)], axis=0)
        acc_scr[...] += lax.dot_general(coef, v_ref[...], (((0,), (0,)), ((), ())), preferred_element_type=F32)

    last = pl.num_programs(1) - 1

    @pl.when(g == 0)
    def _():
        acc_scr[...] = jnp.zeros(acc_scr.shape, F32)
        stage_a(0, act0)
        stage_a(1, act1)
        stage_b(2 * g, act0, coef0)

    @pl.when(jnp.logical_and(g > 0, g < last))
    def _():
        stage_a(0, act0)
        stage_b(2 * g - 1, act1, coef1)
        stage_a(1, act1)
        stage_c()
        stage_b(2 * g, act0, coef0)

    @pl.when(g == last)
    def _():
        stage_b(2 * g - 1, act1, coef1)
        stage_c()
        o_ref[...] = acc_scr[...].astype(BF16)


def _peer(h2t, route, u_bf, v_bf, tt):
    T = h2t.shape[1]
    thr, s2, e1, e2 = route
    n_eb = N_EXPERTS // PEER_EB
    spec = pl.BlockSpec((PEER_HEADS, N_KEYS, tt), lambda i, g: (0, 0, i))
    gspec = pl.BlockSpec((PEER_HEADS, N_KEYS // SUBLANES, SUBLANES, tt), lambda i, g: (0, 0, 0, i))
    assert SUBLANES == 2 * PEER_ROWS
    return pl.pallas_call(
        _peer_kernel,
        grid=(T // tt, n_eb + 1),
        in_specs=[
            pl.BlockSpec((D_MODEL, tt), lambda i, g: (0, i)),
            gspec, spec, gspec, spec,
            pl.BlockSpec((PEER_EB, D_MODEL), lambda i, g: (jnp.minimum(g, n_eb - 1), 0)),
            pl.BlockSpec((PEER_EB, D_MODEL), lambda i, g: (jnp.maximum(g - 1, 0), 0)),
        ],
        out_specs=pl.BlockSpec((tt, D_MODEL), lambda i, g: (i, 0)),
        out_shape=jax.ShapeDtypeStruct((T, D_MODEL), BF16),
        scratch_shapes=[pltpu.VMEM((PEER_SUB, tt), F32), pltpu.VMEM((PEER_SUB, tt), F32),
                        pltpu.VMEM((PEER_SUB, tt), F32), pltpu.VMEM((PEER_SUB, tt), F32),
                        pltpu.VMEM((tt, D_MODEL), F32)],
        compiler_params=_cparams("parallel", "arbitrary"),
        name="peer_dense",
    )(h2t, thr, s2, e1, e2, u_bf, v_bf)


def _ple_kernel(x1_ref, po_ref, p_ref, wg_ref, wp_ref, gp_ref, gf_ref, y_ref):
    x2 = x1_ref[...] + po_ref[...].astype(F32)
    h3 = _rms(x2, gp_ref[...]).astype(BF16)
    gate = _sigmoid(_dot(h3, wg_ref[...]))
    proj = _dot(p_ref[...].astype(BF16), wp_ref[...])
    y_ref[...] = _rms(x2 + proj * gate, gf_ref[...])


def _ple(x1, peer_out, p2d, wg_bf, wp_bf, g_ple, g_final, tm):
    T = x1.shape[0]
    row = lambda n: pl.BlockSpec((tm, n), lambda i: (i, 0))
    const = lambda s: pl.BlockSpec(s, lambda i: (0, 0))
    return pl.pallas_call(
        _ple_kernel,
        grid=(T // tm,),
        in_specs=[row(D_MODEL), row(D_MODEL), row(PLE_DIM),
                  const((D_MODEL, D_MODEL)), const((PLE_DIM, D_MODEL)), const((1, D_MODEL)), const((1, D_MODEL))],
        out_specs=row(D_MODEL),
        out_shape=jax.ShapeDtypeStruct((T, D_MODEL), F32),
        compiler_params=_cparams("parallel"),
        name="ple_final",
    )(x1, peer_out, p2d, wg_bf, wp_bf, g_ple, g_final)


def _tile(n, pref):
    t = min(n, pref)
    assert n % t == 0, (n, t)
    return t


def _layer(x, p, h0_re, h0_im, k_past, v_past, lp, lambda_init):
    bn, seq, _ = x.shape
    past = k_past.shape[1]
    T = bn * seq
    x2d = x.reshape(T, D_MODEL)
    row = lambda a: a.astype(F32).reshape(1, -1)

    tm_in = _tile(T, 256)
    cos_t, sin_t = _rope_tables(seq, past, max(seq, tm_in))
    u, q, k_f32, k_bf, v_f32, v_bf, sig_a, sig_b = _inproj(
        x2d, row(lp['g_mix']), lp['w_in'].astype(BF16), cos_t, sin_t, tm_in)

    seg_len = _tile(seq // SUBLANES, 32)
    h0 = jnp.concatenate([h0_re.reshape(bn, 1, N_STATE), h0_im.reshape(bn, 1, N_STATE)], axis=-1).astype(F32)
    y_ssm, h_fin = _s5(u.reshape(bn, seq, D_SSM), h0, _s5_tables(lp, seg_len), seg_len)
    gated_a = _glu(y_ssm.reshape(T, D_SSM), lp['w_glu'].astype(BF16), sig_a, _tile(T, 1024))

    k_all = jnp.concatenate([k_past.reshape(bn, past, Q_COLS).astype(BF16), k_bf.reshape(bn, seq, Q_COLS)], axis=1)
    v_all = jnp.concatenate([v_past.reshape(bn, past, V_COLS).astype(BF16), v_bf.reshape(bn, seq, V_COLS)], axis=1)
    seq_q = max(seq, LANES // 2)
    q3 = jnp.pad(q.reshape(bn, seq, Q_COLS), ((0, 0), (0, seq_q - seq), (0, 0)))
    tq = _tile(seq_q, 512)
    tk = tq // 2 if past == 0 else past + seq
    o = _attention(q3, k_all, v_all, lp, tq, tk, past, lambda_init)[:, :seq]

    x1, h2t = _merge(x2d, o.reshape(T, V_COLS), gated_a, sig_b, lp['w_attn_out'].astype(BF16),
                    lp['w_out'].astype(BF16), row(lp['g_ffn']), _tile(T, 256))

    tt = _tile(T, 512)
    route = _route(h2t, lp['peer_w_q'].T.astype(BF16), lp['peer_keys'].astype(F32), tt)
    peer_out = _peer(h2t, route, lp['peer_u'].astype(BF16), lp['peer_v'].astype(BF16), tt)

    y = _ple(x1, peer_out, p.reshape(T, PLE_DIM), lp['w_ple_gate'].astype(BF16), lp['w_ple_proj'].astype(BF16),
             row(lp['g_ple']), row(lp['g_final']), _tile(T, 256))

    new_k = k_f32.reshape(bn, seq, N_HEADS, 2 * HEAD_DIM)
    new_v = v_f32.reshape(bn, seq, N_HEADS, V_DIM)
    hr = h_fin[:, 0, :N_STATE].reshape(bn, N_SSM_GROUPS, SSM_STATE)
    hi = h_fin[:, 0, N_STATE:].reshape(bn, N_SSM_GROUPS, SSM_STATE)
    return y.reshape(bn, seq, D_MODEL), new_k, new_v, hr, hi


def _trunk(x, p, h0_re, h0_im, k_past, v_past, lp):
    assert DEPTH == 1
    lambda_init = 0.8 - 0.6 * math.exp(-0.3 * 0)
    y, k_new, v_new, hr, hi = _layer(x, p[0], h0_re[0], h0_im[0], k_past[0], v_past[0], lp, lambda_init)
    return y, k_new[None], v_new[None], hr[None], hi[None]


def kernel(x_prompt, x_sample, p_prompt, p_sample, cache_k, cache_v, state_ssm_re, state_ssm_im,
           g_mix_norm, w_in, ssm_lambda_re, ssm_lambda_im, ssm_log_step, ssm_b_re, ssm_b_im,
           ssm_c_re, ssm_c_im, ssm_d, w_glu, diff_lambda_q1, diff_lambda_k1, diff_lambda_q2,
           diff_lambda_k2, g_subln, w_attn_out, w_out, g_ffn_norm, peer_w_q, peer_keys, peer_u,
           peer_v, g_ple_norm, w_ple_gate, w_ple_proj, g_final):
    lp = dict(g_mix=g_mix_norm[0], w_in=w_in[0], lam_re=ssm_lambda_re[0], lam_im=ssm_lambda_im[0],
              log_step=ssm_log_step[0], b_re=ssm_b_re[0], b_im=ssm_b_im[0], c_re=ssm_c_re[0],
              c_im=ssm_c_im[0], d=ssm_d[0], w_glu=w_glu[0], lq1=diff_lambda_q1[0],
              lk1=diff_lambda_k1[0], lq2=diff_lambda_q2[0], lk2=diff_lambda_k2[0],
              g_subln=g_subln[0], w_attn_out=w_attn_out[0], w_out=w_out[0], g_ffn=g_ffn_norm[0],
              peer_w_q=peer_w_q[0], peer_keys=peer_keys[0], peer_u=peer_u[0], peer_v=peer_v[0],
              g_ple=g_ple_norm[0], w_ple_gate=w_ple_gate[0], w_ple_proj=w_ple_proj[0], g_final=g_final)
    bn = x_prompt.shape[0]
    zeros_state = jnp.zeros((DEPTH, bn, N_SSM_GROUPS, SSM_STATE), F32)
    k_none = jnp.zeros((DEPTH, bn, 0, N_HEADS, 2 * HEAD_DIM), x_prompt.dtype)
    v_none = jnp.zeros((DEPTH, bn, 0, N_HEADS, V_DIM), x_prompt.dtype)
    y_p, k_p, v_p, r_p, i_p = _trunk(x_prompt, p_prompt, zeros_state, zeros_state, k_none, v_none, lp)
    y_s, k_s, v_s, r_s, i_s = _trunk(x_sample, p_sample, state_ssm_re, state_ssm_im, cache_k, cache_v, lp)
    return (y_p, y_s, k_p, v_p, r_p, i_p, k_s, v_s, r_s, i_s)
```

```python
import functools
import math

import jax
import jax.numpy as jnp
from jax import lax
from jax.experimental import pallas as pl
from jax.experimental.pallas import tpu as pltpu

F32 = jnp.float32
BF16 = jnp.bfloat16

D_MODEL = 2048
DEPTH = 1
CHUNK = 64
CHUNK_SHIFT = CHUNK.bit_length() - 1
assert 1 << CHUNK_SHIFT == CHUNK
PLE_DIM = 256
RMS_EPS = 1e-6
NEG_INF = -1e30
D_SSM = 1024
SSM_GROUP = 16
N_SSM_GROUPS = D_SSM // SSM_GROUP
SSM_STATE = 64
N_STATE = N_SSM_GROUPS * SSM_STATE
N_HEADS = 8
HEAD_DIM = 64
V_DIM = 2 * HEAD_DIM
ROPE_THETA = 10000.0
Q_COLS = N_HEADS * 2 * HEAD_DIM
V_COLS = N_HEADS * V_DIM
IN_COLS = D_SSM + 2 * Q_COLS + V_COLS + 2 * D_MODEL
PEER_HEADS = 8
N_KEYS = 128
N_EXPERTS = N_KEYS * N_KEYS
PEER_QUERY = 256
PEER_HALF = PEER_QUERY // 2
PEER_TOPK = 16

LANES = 128
SUBLANES = 8
VMEM_LIMIT = 56 * 1024 * 1024


def _cparams(*sem):
    return pltpu.CompilerParams(dimension_semantics=sem, vmem_limit_bytes=VMEM_LIMIT)


def _rms(x, g):
    return x * lax.rsqrt(jnp.mean(x * x, axis=-1, keepdims=True) + RMS_EPS) * g


def _gelu(x):
    return 0.5 * x * (1.0 + lax.erf(x * (2.0 ** -0.5)))


def _sigmoid(x):
    return 1.0 / (1.0 + jnp.exp(-x))


def _dot(a, b):
    return jnp.dot(a, b, preferred_element_type=F32)


IN_TN = 512
Q_SCALE = HEAD_DIM ** -0.5 * math.log2(math.e)
IN_MIX_COLS = D_SSM + 2 * Q_COLS + V_COLS


def _column_tiles(h_scr, w_ref):
    return [functools.partial(lambda c: _dot(h_scr[...], w_ref[:, c * IN_TN:(c + 1) * IN_TN]), c)
            for c in range(w_ref.shape[1] // IN_TN)]


def _inproj_mix_kernel(x_ref, g_ref, w_ref, cos_ref, sin_ref, u_ref, q_ref, kf_ref, kb_ref, vf_ref, vb_ref, h_scr):
    h_scr[...] = _rms(x_ref[...], g_ref[...]).astype(BF16)
    z = _column_tiles(h_scr, w_ref)

    def rope(t):
        n = t.shape[-1]
        lane = lax.broadcasted_iota(jnp.int32, t.shape, 1)
        first = jnp.bitwise_and(lane, HEAD_DIM - 1) < (HEAD_DIM // 2)
        partner = jnp.where(first, pltpu.roll(t, n - HEAD_DIM // 2, 1), pltpu.roll(t, HEAD_DIM // 2, 1))
        reps = n // LANES
        return t * jnp.tile(cos_ref[...], (1, reps)) + partner * jnp.tile(sin_ref[...], (1, reps))

    per = Q_COLS // IN_TN
    for c in range(per):
        cols = slice(c * IN_TN, (c + 1) * IN_TN)
        u_ref[:, cols] = z[c]().astype(BF16)
        q_ref[:, cols] = (rope(z[per + c]()) * Q_SCALE).astype(BF16)
        k = rope(z[2 * per + c]())
        kf_ref[:, cols] = k
        kb_ref[:, cols] = k.astype(BF16)
        v = z[3 * per + c]()
        vf_ref[:, cols] = v
        vb_ref[:, cols] = v.astype(BF16)


def _inproj_gate_kernel(x_ref, g_ref, w_ref, ga_ref, gb_ref, h_scr):
    h_scr[...] = _rms(x_ref[...], g_ref[...]).astype(BF16)
    z = _column_tiles(h_scr, w_ref)
    per = D_MODEL // IN_TN
    for c in range(per):
        cols = slice(c * IN_TN, (c + 1) * IN_TN)
        ga_ref[:, cols] = _sigmoid(z[c]()).astype(BF16)
        gb_ref[:, cols] = _sigmoid(z[per + c]()).astype(BF16)


def _inproj(x2d, g, w_bf, cos_t, sin_t, tm):
    assert D_SSM == Q_COLS == V_COLS
    T = x2d.shape[0]
    n_tab = cos_t.shape[0] // tm
    row = lambda n: pl.BlockSpec((tm, n), lambda i: (i, 0))
    const = lambda s: pl.BlockSpec(s, lambda i: (0, 0))
    table = pl.BlockSpec((tm, LANES), lambda i: (i % n_tab, 0))
    arr = lambda n, dt: jax.ShapeDtypeStruct((T, n), dt)
    common = dict(grid=(T // tm,), scratch_shapes=[pltpu.VMEM((tm, D_MODEL), BF16)],
                  compiler_params=_cparams("parallel"))
    u, q, kf, kb, vf, vb = pl.pallas_call(
        _inproj_mix_kernel,
        in_specs=[row(D_MODEL), const((1, D_MODEL)), const((D_MODEL, IN_MIX_COLS)), table, table],
        out_specs=[row(Q_COLS)] * 6,
        out_shape=[arr(Q_COLS, BF16), arr(Q_COLS, BF16), arr(Q_COLS, F32), arr(Q_COLS, BF16),
                   arr(Q_COLS, F32), arr(Q_COLS, BF16)],
        name="inproj_mix", **common,
    )(x2d, g, w_bf[:, :IN_MIX_COLS], cos_t, sin_t)
    ga, gb = pl.pallas_call(
        _inproj_gate_kernel,
        in_specs=[row(D_MODEL), const((1, D_MODEL)), const((D_MODEL, 2 * D_MODEL))],
        out_specs=[row(D_MODEL)] * 2,
        out_shape=[arr(D_MODEL, BF16)] * 2,
        name="inproj_gate", **common,
    )(x2d, g, w_bf[:, IN_MIX_COLS:])
    return u, q, kf, kb, vf, vb, ga, gb


def _rope_tables(seq, past, rows):
    half = HEAD_DIM // 2
    inv = ROPE_THETA ** (-jnp.arange(half, dtype=F32) * 2.0 / HEAD_DIM)
    pos = (past + (jnp.arange(rows, dtype=jnp.int32) % seq)).astype(F32)
    ang = pos[:, None] * inv[None, :]
    cos, sin = jnp.cos(ang), jnp.sin(ang)
    reps = LANES // HEAD_DIM
    cos_t = jnp.tile(jnp.concatenate([cos, cos], axis=-1), (1, reps))
    sin_t = jnp.tile(jnp.concatenate([-sin, sin], axis=-1), (1, reps))
    return cos_t, sin_t


S5_COLS = 512
S5_FIX_UNROLL = SUBLANES
S5_UBLK = LANES
S5_N_UBLK = D_SSM // S5_UBLK
S5_XBLK = (S5_UBLK // SSM_GROUP) * SSM_STATE
S5_YBLK = 256
S5_N_YBLK = D_SSM // S5_YBLK
S5_HBLK = (S5_YBLK // SSM_GROUP) * SSM_STATE


def _s5_kernel(u_ref, h0_ref, perm_ref, permt_ref, wx_ref, apow_ref, cr_ref, ci_ref, d_ref,
               y_ref, hfin_ref, x_scr, c_scr, carry_scr, *, seg_len):
    t = pl.program_id(1)
    rows = SUBLANES * seg_len
    n = N_STATE

    @pl.when(t == 0)
    def _():
        carry_scr[...] = h0_ref[...]

    up = _dot(perm_ref[...], u_ref[...])
    upb = up.astype(BF16)
    assert S5_COLS == S5_XBLK
    for a in range(S5_N_UBLK):
        xa = _dot(upb[:, a * S5_UBLK:(a + 1) * S5_UBLK], wx_ref[a])
        lo = a * S5_XBLK
        x_scr[:, lo:lo + S5_XBLK] = xa[:, :S5_XBLK]
        x_scr[:, n + lo:n + lo + S5_XBLK] = xa[:, S5_XBLK:]

        ar = jnp.broadcast_to(apow_ref[0:1, lo:lo + S5_COLS], (SUBLANES, S5_COLS))
        ai = jnp.broadcast_to(apow_ref[0:1, n + lo:n + lo + S5_COLS], (SUBLANES, S5_COLS))
        hr = hi = jnp.zeros((SUBLANES, S5_COLS), F32)
        for k in range(seg_len):
            step = slice(k * SUBLANES, (k + 1) * SUBLANES)
            xr = x_scr[step, lo:lo + S5_COLS]
            xi = x_scr[step, n + lo:n + lo + S5_COLS]
            hr, hi = ar * hr - ai * hi + xr, ar * hi + ai * hr + xi
            x_scr[step, lo:lo + S5_COLS] = hr
            x_scr[step, n + lo:n + lo + S5_COLS] = hi

    alr = apow_ref[seg_len - 1:seg_len, :n]
    ali = apow_ref[seg_len - 1:seg_len, n:]
    cr = carry_scr[:, :n]
    ci = carry_scr[:, n:]
    for s in range(SUBLANES):
        c_scr[s:s + 1, :n] = cr
        c_scr[s:s + 1, n:] = ci
        lr = x_scr[rows - SUBLANES + s:rows - SUBLANES + s + 1, :n]
        li = x_scr[rows - SUBLANES + s:rows - SUBLANES + s + 1, n:]
        cr, ci = alr * cr - ali * ci + lr, alr * ci + ali * cr + li
    carry_scr[:, :n] = cr
    carry_scr[:, n:] = ci

    for j in range(S5_N_YBLK):
        for c in range(j * (S5_HBLK // S5_COLS), (j + 1) * (S5_HBLK // S5_COLS)):
            lo = c * S5_COLS
            sr = c_scr[:, lo:lo + S5_COLS]
            si = c_scr[:, n + lo:n + lo + S5_COLS]
            unroll = math.gcd(seg_len, S5_FIX_UNROLL)

            def fix_steps(kk, carry, lo=lo, sr=sr, si=si, unroll=unroll):
                k0 = kk * unroll if isinstance(kk, int) else pl.multiple_of(kk * unroll, unroll)
                pr_all = apow_ref[pl.ds(k0, unroll), lo:lo + S5_COLS]
                pi_all = apow_ref[pl.ds(k0, unroll), n + lo:n + lo + S5_COLS]
                for d in range(unroll):
                    r0 = (k0 + d) * SUBLANES
                    step = pl.ds(r0 if isinstance(r0, int) else pl.multiple_of(r0, SUBLANES), SUBLANES)
                    pr, pi = pr_all[d:d + 1], pi_all[d:d + 1]
                    x_scr[step, lo:lo + S5_COLS] += pr * sr - pi * si
                    x_scr[step, n + lo:n + lo + S5_COLS] += pr * si + pi * sr
                return carry

            if seg_len == unroll:
                fix_steps(0, 0)
            else:
                lax.fori_loop(0, seg_len // unroll, fix_steps, 0)

        hr = x_scr[:, j * S5_HBLK:(j + 1) * S5_HBLK].astype(BF16)
        hi = x_scr[:, n + j * S5_HBLK:n + (j + 1) * S5_HBLK].astype(BF16)
        yj = (_dot(hr, cr_ref[j]) + _dot(hi, ci_ref[j])
              + d_ref[:, j * S5_YBLK:(j + 1) * S5_YBLK] * up[:, j * S5_YBLK:(j + 1) * S5_YBLK])
        yj = _gelu(yj).astype(BF16)
        y_ref[:, j * S5_YBLK:(j + 1) * S5_YBLK] = _dot(permt_ref[...], yj).astype(BF16)

    @pl.when(t == pl.num_programs(1) - 1)
    def _():
        hfin_ref[...] = carry_scr[...]


def _s5_tables(lp, seg_len):
    g, p, c = N_SSM_GROUPS, SSM_STATE, SSM_GROUP
    dt = jnp.exp(lp['log_step'].astype(F32))[:, None]
    lr, li = lp['lam_re'].astype(F32), lp['lam_im'].astype(F32)
    mag = jnp.exp(lr * dt)
    ar, ai = mag * jnp.cos(li * dt), mag * jnp.sin(li * dt)
    den = lr * lr + li * li
    fr = ((ar - 1.0) * lr + ai * li) / den
    fi = (ai * lr - (ar - 1.0) * li) / den
    br, bi = lp['b_re'].astype(F32), lp['b_im'].astype(F32)
    bbr = fr[..., None] * br - fi[..., None] * bi
    bbi = fr[..., None] * bi + fi[..., None] * br

    gpb = S5_UBLK // c
    eye = jnp.eye(gpb, dtype=F32)

    def in_blocks(bb):
        bb = bb.reshape(S5_N_UBLK, gpb, p, c)
        return jnp.einsum('xy,axpc->axcyp', eye, bb).reshape(S5_N_UBLK, S5_UBLK, S5_XBLK)

    wx = jnp.concatenate([in_blocks(bbr), in_blocks(bbi)], axis=-1).astype(BF16)

    pr, pi = ar.reshape(1, g * p), ai.reshape(1, g * p)
    while pr.shape[0] < seg_len:
        tr, ti = pr[-1:], pi[-1:]
        pr, pi = (jnp.concatenate([pr, pr * tr - pi * ti], axis=0),
                  jnp.concatenate([pi, pr * ti + pi * tr], axis=0))
    apow = jnp.concatenate([pr[:seg_len], pi[:seg_len]], axis=-1)

    gpy = S5_YBLK // c
    eye_y = jnp.eye(gpy, dtype=F32)

    def out_blocks(cm):
        cm = cm.astype(F32).reshape(S5_N_YBLK, gpy, c, p)
        return jnp.einsum('xy,jxcp->jxpyc', eye_y, cm).reshape(S5_N_YBLK, S5_HBLK, S5_YBLK)

    cr = out_blocks(lp['c_re']).astype(BF16)
    ci = (-out_blocks(lp['c_im'])).astype(BF16)
    d = lp['d'].astype(F32).reshape(1, D_SSM)
    return wx, apow, cr, ci, d


def _s5(u, h0, tables, seg_len):
    bn, seq, _ = u.shape
    rows = SUBLANES * seg_len
    wx, apow, cr, ci, d = tables
    r = jnp.arange(rows)
    perm = (r[None, :] == ((r % SUBLANES) * seg_len + r // SUBLANES)[:, None]).astype(BF16)
    const2 = lambda b, t: (0, 0)
    const3 = lambda b, t: (0, 0, 0)
    return pl.pallas_call(
        functools.partial(_s5_kernel, seg_len=seg_len),
        grid=(bn, seq // rows),
        in_specs=[
            pl.BlockSpec((None, rows, D_SSM), lambda b, t: (b, t, 0)),
            pl.BlockSpec((None, 1, 2 * N_STATE), lambda b, t: (b, 0, 0)),
            pl.BlockSpec((rows, rows), const2),
            pl.BlockSpec((rows, rows), const2),
            pl.BlockSpec(wx.shape, const3),
            pl.BlockSpec(apow.shape, const2),
            pl.BlockSpec(cr.shape, const3),
            pl.BlockSpec(ci.shape, const3),
            pl.BlockSpec((1, D_SSM), const2),
        ],
        out_specs=[
            pl.BlockSpec((None, rows, D_SSM), lambda b, t: (b, t, 0)),
            pl.BlockSpec((None, 1, 2 * N_STATE), lambda b, t: (b, 0, 0)),
        ],
        out_shape=[jax.ShapeDtypeStruct((bn, seq, D_SSM), BF16),
                   jax.ShapeDtypeStruct((bn, 1, 2 * N_STATE), F32)],
        scratch_shapes=[pltpu.VMEM((rows, 2 * N_STATE), F32),
                        pltpu.VMEM((SUBLANES, 2 * N_STATE), F32),
                        pltpu.VMEM((1, 2 * N_STATE), F32)],
        compiler_params=_cparams("parallel", "arbitrary"),
        name="s5",
    )(u, h0, perm, perm.T, wx, apow, cr, ci, d)


GLU_TN = 512


def _glu_kernel(y_ref, w_ref, ga_ref, o_ref):
    y = y_ref[...]
    for c in range(D_MODEL // GLU_TN):
        cols = slice(c * GLU_TN, (c + 1) * GLU_TN)
        a = _dot(y, w_ref[:, cols])
        b = _dot(y, w_ref[:, D_MODEL + c * GLU_TN:D_MODEL + (c + 1) * GLU_TN])
        o_ref[:, cols] = (ga_ref[:, cols].astype(F32) * a * _sigmoid(b)).astype(BF16)


def _glu(y, w_glu_bf, sig_a, tm):
    T = y.shape[0]
    return pl.pallas_call(
        _glu_kernel,
        grid=(T // tm,),
        in_specs=[
            pl.BlockSpec((tm, D_SSM), lambda i: (i, 0)),
            pl.BlockSpec((D_SSM, 2 * D_MODEL), lambda i: (0, 0)),
            pl.BlockSpec((tm, D_MODEL), lambda i: (i, 0)),
        ],
        out_specs=pl.BlockSpec((tm, D_MODEL), lambda i: (i, 0)),
        out_shape=jax.ShapeDtypeStruct((T, D_MODEL), BF16),
        compiler_params=_cparams("parallel"),
        name="glu",
    )(y, w_glu_bf, sig_a)


def _attn_kernel(q_ref, k_ref, v_ref, lq1_ref, lk1_ref, lq2_ref, lk2_ref, g_ref, o_ref,
                 qt_scr, s0_scr, s1_scr, m_scr, l_scr, acc_scr, *, tq, tk, past, paired, lambda_init):
    i = pl.program_id(2)
    nk = k_ref.shape[0] // tk
    q = q_ref[...].astype(F32)
    lane = lax.broadcasted_iota(jnp.int32, q.shape, 1)
    qs = jnp.concatenate([jnp.where(lane < HEAD_DIM, q, 0.0), jnp.where(lane >= HEAD_DIM, q, 0.0)], axis=0)
    qt_scr[...] = qs.T.astype(BF16)

    m_scr[...] = jnp.full(m_scr.shape, NEG_INF, F32)
    l_scr[...] = jnp.zeros(l_scr.shape, F32)
    acc_scr[...] = jnp.zeros(acc_scr.shape, F32)

    q_lo = past + i * tq
    min_qc = q_lo // CHUNK
    max_qc = (q_lo + tq - 1) // CHUNK
    n_proc = jnp.minimum(nk, (max_qc * CHUNK + CHUNK - 1) // tk + 1)
    n_full = jnp.clip((min_qc * CHUNK + CHUNK) // tk, 0, n_proc)

    def scores(j, s_ref):
        k0 = pl.multiple_of(j * tk, tk)
        s_ref[...] = _dot(k_ref[pl.ds(k0, tk), :], qt_scr[...])

    def update(j, s_ref, cols):
        k0 = pl.multiple_of(j * tk, tk)
        vb = v_ref[pl.ds(k0, tk), :]
        for lo, hi, masked in cols:
            s = s_ref[:, lo:hi]
            if masked:
                kpos = k0 + lax.broadcasted_iota(jnp.int32, s.shape, 0)
                qpos = q_lo + jnp.bitwise_and(lo + lax.broadcasted_iota(jnp.int32, s.shape, 1), tq - 1)
                s = jnp.where(jnp.right_shift(kpos, CHUNK_SHIFT) <= jnp.right_shift(qpos, CHUNK_SHIFT), s, NEG_INF)
            m_old = m_scr[:, lo:hi]
            m_new = jnp.maximum(m_old, jnp.max(s, axis=0, keepdims=True))
            alpha = jnp.exp2(m_old - m_new)
            p = jnp.exp2(s - m_new)
            l_scr[:, lo:hi] = alpha * l_scr[:, lo:hi] + jnp.sum(p, axis=0, keepdims=True)
            pv = lax.dot_general(vb, p.astype(BF16), (((0,), (0,)), ((), ())), preferred_element_type=F32)
            acc_scr[:, lo:hi] = alpha * acc_scr[:, lo:hi] + pv
            m_scr[:, lo:hi] = m_new

    everything = ((0, 2 * tq, False),)
    everything_masked = ((0, 2 * tq, True),)

    if paired:
        scores(0, s0_scr)

        def pair(p, c):
            scores(2 * p + 1, s1_scr)
            update(2 * p, s0_scr, everything)
            scores(2 * p + 2, s0_scr)
            update(2 * p + 1, s1_scr, everything)
            return c

        lax.fori_loop(0, i, pair, 0)
        scores(2 * i + 1, s1_scr)
        update(2 * i, s0_scr, tuple(r for c0 in (0, tq) for r in ((c0, c0 + tk, True), (c0 + tk, c0 + tq, False))))
        update(2 * i + 1, s1_scr, tuple((c0 + tk, c0 + tq, True) for c0 in (0, tq)))
    else:
        def full_body(j, c):
            scores(j, s0_scr)
            update(j, s0_scr, everything)
            return c

        def masked_body(j, c):
            scores(j, s0_scr)
            update(j, s0_scr, everything_masked)
            return c

        lax.fori_loop(0, n_full, full_body, 0)
        lax.fori_loop(n_full, n_proc, masked_body, 0)

    lam = (jnp.exp(jnp.sum(lq1_ref[...] * lk1_ref[...], axis=-1, keepdims=True))
           - jnp.exp(jnp.sum(lq2_ref[...] * lk2_ref[...], axis=-1, keepdims=True)) + lambda_init)
    ot = acc_scr[:, :tq] / l_scr[:, :tq] - lam * (acc_scr[:, tq:] / l_scr[:, tq:])
    o_ref[...] = (_rms(ot.T, g_ref[...]) * (1.0 - lambda_init)).astype(BF16)


def _attention(q, k, v, lp, tq, tk, past, lambda_init):
    bn, seq, _ = q.shape
    lk = k.shape[1]
    vec = lambda a: a.astype(F32).reshape(1, -1)
    small = lambda n: pl.BlockSpec((1, n), lambda b, h, i: (0, 0))
    paired = past == 0 and tq == 2 * tk and lk == seq
    s_scr = pltpu.VMEM((tk, 2 * tq), F32)
    return pl.pallas_call(
        functools.partial(_attn_kernel, tq=tq, tk=tk, past=past, paired=paired, lambda_init=lambda_init),
        grid=(bn, N_HEADS, seq // tq),
        in_specs=[
            pl.BlockSpec((None, tq, V_DIM), lambda b, h, i: (b, i, h)),
            pl.BlockSpec((None, lk, V_DIM), lambda b, h, i: (b, 0, h)),
            pl.BlockSpec((None, lk, V_DIM), lambda b, h, i: (b, 0, h)),
            small(HEAD_DIM), small(HEAD_DIM), small(HEAD_DIM), small(HEAD_DIM), small(V_DIM),
        ],
        out_specs=pl.BlockSpec((None, tq, V_DIM), lambda b, h, i: (b, i, h)),
        out_shape=jax.ShapeDtypeStruct((bn, seq, V_COLS), BF16),
        scratch_shapes=[pltpu.VMEM((V_DIM, 2 * tq), BF16), s_scr, s_scr, pltpu.VMEM((1, 2 * tq), F32),
                        pltpu.VMEM((1, 2 * tq), F32), pltpu.VMEM((V_DIM, 2 * tq), F32)],
        compiler_params=_cparams("parallel", "parallel", "arbitrary"),
        name="diff_attn",
    )(q, k, v, vec(lp['lq1']), vec(lp['lk1']), vec(lp['lq2']), vec(lp['lk2']), vec(lp['g_subln']))


def _merge_kernel(x_ref, o_ref, ga_ref, sb_ref, wa_ref, wo_ref, g_ref, x1_ref, h2t_ref):
    branch_b = _dot(o_ref[...], wa_ref[...])
    merged = ga_ref[...].astype(F32) + sb_ref[...].astype(F32) * branch_b
    x1 = x_ref[...] + _dot(merged.astype(BF16), wo_ref[...])
    x1_ref[...] = x1
    h2t_ref[...] = _rms(x1, g_ref[...]).T.astype(BF16)


def _merge(x2d, o, gated_a, sig_b, wa_bf, wo_bf, g_ffn, tm):
    T = x2d.shape[0]
    row = lambda n: pl.BlockSpec((tm, n), lambda i: (i, 0))
    const = lambda s: pl.BlockSpec(s, lambda i: (0, 0))
    return pl.pallas_call(
        _merge_kernel,
        grid=(T // tm,),
        in_specs=[row(D_MODEL), row(V_COLS), row(D_MODEL), row(D_MODEL),
                  const((V_COLS, D_MODEL)), const((D_MODEL, D_MODEL)), const((1, D_MODEL))],
        out_specs=[row(D_MODEL), pl.BlockSpec((D_MODEL, tm), lambda i: (0, i))],
        out_shape=[jax.ShapeDtypeStruct((T, D_MODEL), F32), jax.ShapeDtypeStruct((D_MODEL, T), BF16)],
        compiler_params=_cparams("parallel"),
        name="merge_out",
    )(x2d, o, gated_a, sig_b, wa_bf, wo_bf, g_ffn)


ROUTE_LC = 256
TAKEN = -3.0e38


def _split_bf16(x):
    hi = x.astype(BF16)
    return hi, (x - hi.astype(F32)).astype(BF16)


def _oddeven_merge(lo, hi, r):
    step = r * 2
    if step < hi - lo:
        yield from _oddeven_merge(lo, hi, step)
        yield from _oddeven_merge(lo + r, hi, step)
        yield from [(i, i + r) for i in range(lo + r, hi - r, step)]
    else:
        yield (lo, lo + r)


def _oddeven_merge_sort(lo, hi):
    if hi - lo >= 1:
        mid = lo + (hi - lo) // 2
        yield from _oddeven_merge_sort(lo, mid)
        yield from _oddeven_merge_sort(mid + 1, hi)
        yield from _oddeven_merge(lo, hi, 1)


def _top_rows_sorted(s, k):
    n = s.shape[0] // SUBLANES
    rows = [s[SUBLANES * g:SUBLANES * (g + 1), :] for g in range(n)]
    n_pow2 = 1 << (n - 1).bit_length()
    for i, j in _oddeven_merge_sort(0, n_pow2 - 1):
        if j < n:
            rows[i], rows[j] = jnp.maximum(rows[i], rows[j]), jnp.minimum(rows[i], rows[j])
    taken = jnp.full_like(rows[0], TAKEN)
    out = []
    for r in range(k):
        m = jnp.max(rows[0], axis=0, keepdims=True)
        out.append(m)
        hit = rows[0] == m
        depth = min(n, k - r - 1)
        for d in range(depth):
            rows[d] = jnp.where(hit, rows[d + 1] if d + 1 < n else taken, rows[d])
    return out


def _stack_rows(rows):
    n, w = len(rows), rows[0].shape[1]
    idx = lax.broadcasted_iota(jnp.int32, (n, w), 0)
    out = jnp.zeros((n, w), F32)
    for r, v in enumerate(rows):
        out = jnp.where(idx == r, v, out)
    return out


def _route_kernel(h2t_ref, wq_ref, keys_ref, thr_ref, s2_ref, e1_ref, e2_ref, s1_scr):
    for h in range(PEER_HEADS):
        _route_head(h, h2t_ref, wq_ref, keys_ref, thr_ref, s2_ref, e1_ref, e2_ref, s1_scr)


def _route_head(h, h2t_ref, wq_ref, keys_ref, thr_ref, s2_ref, e1_ref, e2_ref, s1_scr):
    tt = h2t_ref.shape[1]
    n_top = PEER_TOPK + 1
    qt = _dot(wq_ref[h * PEER_QUERY:(h + 1) * PEER_QUERY, :], h2t_ref[...])
    halves = []
    for c in range(2):
        q_hi, q_lo = _split_bf16(qt[c * PEER_HALF:(c + 1) * PEER_HALF, :])
        k_hi, k_lo = _split_bf16(keys_ref[h, c])
        halves.append(_dot(k_hi, q_hi) + _dot(k_hi, q_lo) + _dot(k_lo, q_hi))
    s1_scr[h] = halves[0]
    s2_ref[h] = halves[1]

    width = min(tt, ROUTE_LC)
    for lc in range(tt // width):
        sl = slice(lc * width, (lc + 1) * width)
        s1 = s1_scr[h, :, sl]
        s2 = s2_ref[h, :, sl]
        a = _top_rows_sorted(s1, n_top)
        b = _top_rows_sorted(s2, n_top)
        taken = jnp.full_like(b[0], TAKEN)
        b_all = _stack_rows(b + [taken] * (3 * SUBLANES - n_top))
        tail = [a[i] + b[j] for i in range(4, n_top) for j in range(n_top // (i + 1))]
        tail += [taken] * (-len(tail) % SUBLANES)
        cand = jnp.concatenate([a[0] + b_all] + [a[i] + b_all[:SUBLANES] for i in range(1, 4)]
                               + [_stack_rows(tail)], axis=0)
        top = _top_rows_sorted(cand, n_top)
        z = jnp.zeros_like(top[0])
        for r in range(PEER_TOPK):
            z = z + jnp.exp(top[r] - top[0])
        tau = 0.5 * (top[PEER_TOPK - 1] + top[PEER_TOPK])
        grouped = (N_KEYS // SUBLANES, SUBLANES, width)
        thr_ref[h, :, :, sl] = (tau - s1).reshape(grouped)
        e1_ref[h, :, :, sl] = (jnp.exp(s1 - a[0]) * (0.5 / z)).reshape(grouped)
        e2_ref[h, :, sl] = jnp.exp(s2 - b[0])


def _route(h2t, wq_t_bf, keys, tt):
    T = h2t.shape[1]
    arr = jax.ShapeDtypeStruct((PEER_HEADS, N_KEYS, T), F32)
    spec = pl.BlockSpec((PEER_HEADS, N_KEYS, tt), lambda i: (0, 0, i))
    n_grp = N_KEYS // SUBLANES
    garr = jax.ShapeDtypeStruct((PEER_HEADS, n_grp, SUBLANES, T), F32)
    gspec = pl.BlockSpec((PEER_HEADS, n_grp, SUBLANES, tt), lambda i: (0, 0, 0, i))
    return pl.pallas_call(
        _route_kernel,
        grid=(T // tt,),
        in_specs=[
            pl.BlockSpec((D_MODEL, tt), lambda i: (0, i)),
            pl.BlockSpec((PEER_HEADS * PEER_QUERY, D_MODEL), lambda i: (0, 0)),
            pl.BlockSpec((PEER_HEADS, 2, N_KEYS, PEER_HALF), lambda i: (0, 0, 0, 0)),
        ],
        out_specs=[gspec, spec, gspec, spec],
        out_shape=[garr, arr, garr, arr],
        scratch_shapes=[pltpu.VMEM((PEER_HEADS, N_KEYS, tt), F32)],
        compiler_params=_cparams("parallel"),
        name="peer_route",
    )(h2t, wq_t_bf, keys)


PEER_SUB = 512
PEER_EB = 2 * PEER_SUB
PEER_ROWS = PEER_SUB // N_KEYS
N_SUB = N_EXPERTS // PEER_SUB
PEER_JH = 64


def _peer_kernel(h2t_ref, thr_ref, s2_ref, e1_ref, e2_ref, u_ref, v_ref, o_ref,
                 act0, act1, coef0, coef1, acc_scr):
    g = pl.program_id(1)
    tt = h2t_ref.shape[1]

    def stage_a(half, act):
        act[...] = _dot(u_ref[half * PEER_SUB:(half + 1) * PEER_SUB, :], h2t_ref[...])

    def stage_b(b, act, coef):
        b = jnp.clip(b, 0, N_SUB - 1)
        grp = b // 2
        odd = (b % 2) == 1
        for lc in range(tt // LANES):
            ln = slice(lc * LANES, (lc + 1) * LANES)
            for jh in range(N_KEYS // PEER_JH):
                js = slice(jh * PEER_JH, (jh + 1) * PEER_JH)
                gates = [jnp.zeros((PEER_JH, LANES), F32) for _ in range(PEER_ROWS)]
                for h in range(PEER_HEADS):
                    s2 = s2_ref[h, js, ln]
                    e2 = e2_ref[h, js, ln]
                    thr_grp = thr_ref[h, grp, :, ln]
                    e1_grp = e1_ref[h, grp, :, ln]
                    thr_rows = jnp.where(odd, thr_grp[PEER_ROWS:], thr_grp[:PEER_ROWS])
                    e1_rows = jnp.where(odd, e1_grp[PEER_ROWS:], e1_grp[:PEER_ROWS])
                    for r in range(PEER_ROWS):
                        gates[r] = gates[r] + jnp.where(s2 >= thr_rows[r:r + 1], e1_rows[r:r + 1] * e2, 0.0)
                for r in range(PEER_ROWS):
                    rows = slice(r * N_KEYS + jh * PEER_JH, r * N_KEYS + (jh + 1) * PEER_JH)
                    x = act[rows, ln]
                    coef[rows, ln] = gates[r] * (x * (1.0 + lax.erf(x * (2.0 ** -0.5))))

    def stage_c():
        coef = jnp.concatenate([coef0[...].astype(BF16), coef1[...].astype(BF16)], axis=0)
        acc_scr[...] += lax.dot_general(coef, v_ref[...], (((0,), (0,)), ((), ())), preferred_element_type=F32)

    last = pl.num_programs(1) - 1

    @pl.when(g == 0)
    def _():
        acc_scr[...] = jnp.zeros(acc_scr.shape, F32)
        stage_a(0, act0)
        stage_a(1, act1)
        stage_b(2 * g, act0, coef0)

    @pl.when(jnp.logical_and(g > 0, g < last))
    def _():
        stage_a(0, act0)
        stage_b(2 * g - 1, act1, coef1)
        stage_a(1, act1)
        stage_c()
        stage_b(2 * g, act0, coef0)

    @pl.when(g == last)
    def _():
        stage_b(2 * g - 1, act1, coef1)
        stage_c()
        o_ref[...] = acc_scr[...].astype(BF16)


def _peer(h2t, route, u_bf, v_bf, tt):
    T = h2t.shape[1]
    thr, s2, e1, e2 = route
    n_eb = N_EXPERTS // PEER_EB
    spec = pl.BlockSpec((PEER_HEADS, N_KEYS, tt), lambda i, g: (0, 0, i))
    gspec = pl.BlockSpec((PEER_HEADS, N_KEYS // SUBLANES, SUBLANES, tt), lambda i, g: (0, 0, 0, i))
    assert SUBLANES == 2 * PEER_ROWS
    return pl.pallas_call(
        _peer_kernel,
        grid=(T // tt, n_eb + 1),
        in_specs=[
            pl.BlockSpec((D_MODEL, tt), lambda i, g: (0, i)),
            gspec, spec, gspec, spec,
            pl.BlockSpec((PEER_EB, D_MODEL), lambda i, g: (jnp.minimum(g, n_eb - 1), 0)),
            pl.BlockSpec((PEER_EB, D_MODEL), lambda i, g: (jnp.maximum(g - 1, 0), 0)),
        ],
        out_specs=pl.BlockSpec((tt, D_MODEL), lambda i, g: (i, 0)),
        out_shape=jax.ShapeDtypeStruct((T, D_MODEL), BF16),
        scratch_shapes=[pltpu.VMEM((PEER_SUB, tt), F32), pltpu.VMEM((PEER_SUB, tt), F32),
                        pltpu.VMEM((PEER_SUB, tt), F32), pltpu.VMEM((PEER_SUB, tt), F32),
                        pltpu.VMEM((tt, D_MODEL), F32)],
        compiler_params=_cparams("parallel", "arbitrary"),
        name="peer_dense",
    )(h2t, thr, s2, e1, e2, u_bf, v_bf)


def _ple_kernel(x1_ref, po_ref, p_ref, wg_ref, wp_ref, gp_ref, gf_ref, y_ref):
    x2 = x1_ref[...] + po_ref[...].astype(F32)
    h3 = _rms(x2, gp_ref[...]).astype(BF16)
    gate = _sigmoid(_dot(h3, wg_ref[...]))
    proj = _dot(p_ref[...].astype(BF16), wp_ref[...])
    y_ref[...] = _rms(x2 + proj * gate, gf_ref[...])


def _ple(x1, peer_out, p2d, wg_bf, wp_bf, g_ple, g_final, tm):
    T = x1.shape[0]
    row = lambda n: pl.BlockSpec((tm, n), lambda i: (i, 0))
    const = lambda s: pl.BlockSpec(s, lambda i: (0, 0))
    return pl.pallas_call(
        _ple_kernel,
        grid=(T // tm,),
        in_specs=[row(D_MODEL), row(D_MODEL), row(PLE_DIM),
                  const((D_MODEL, D_MODEL)), const((PLE_DIM, D_MODEL)), const((1, D_MODEL)), const((1, D_MODEL))],
        out_specs=row(D_MODEL),
        out_shape=jax.ShapeDtypeStruct((T, D_MODEL), F32),
        compiler_params=_cparams("parallel"),
        name="ple_final",
    )(x1, peer_out, p2d, wg_bf, wp_bf, g_ple, g_final)


def _tile(n, pref):
    t = min(n, pref)
    assert n % t == 0, (n, t)
    return t


def _layer(x, p, h0_re, h0_im, k_past, v_past, lp, lambda_init):
    bn, seq, _ = x.shape
    past = k_past.shape[1]
    T = bn * seq
    x2d = x.reshape(T, D_MODEL)
    row = lambda a: a.astype(F32).reshape(1, -1)

    tm_in = _tile(T, 256)
    cos_t, sin_t = _rope_tables(seq, past, max(seq, tm_in))
    u, q, k_f32, k_bf, v_f32, v_bf, sig_a, sig_b = _inproj(
        x2d, row(lp['g_mix']), lp['w_in'].astype(BF16), cos_t, sin_t, tm_in)

    seg_len = _tile(seq // SUBLANES, 32)
    h0 = jnp.concatenate([h0_re.reshape(bn, 1, N_STATE), h0_im.reshape(bn, 1, N_STATE)], axis=-1).astype(F32)
    y_ssm, h_fin = _s5(u.reshape(bn, seq, D_SSM), h0, _s5_tables(lp, seg_len), seg_len)
    gated_a = _glu(y_ssm.reshape(T, D_SSM), lp['w_glu'].astype(BF16), sig_a, _tile(T, 1024))

    k_all = jnp.concatenate([k_past.reshape(bn, past, Q_COLS).astype(BF16), k_bf.reshape(bn, seq, Q_COLS)], axis=1)
    v_all = jnp.concatenate([v_past.reshape(bn, past, V_COLS).astype(BF16), v_bf.reshape(bn, seq, V_COLS)], axis=1)
    seq_q = max(seq, LANES // 2)
    q3 = jnp.pad(q.reshape(bn, seq, Q_COLS), ((0, 0), (0, seq_q - seq), (0, 0)))
    tq = _tile(seq_q, 512)
    tk = tq // 2 if past == 0 else past + seq
    o = _attention(q3, k_all, v_all, lp, tq, tk, past, lambda_init)[:, :seq]

    x1, h2t = _merge(x2d, o.reshape(T, V_COLS), gated_a, sig_b, lp['w_attn_out'].astype(BF16),
                    lp['w_out'].astype(BF16), row(lp['g_ffn']), _tile(T, 256))

    tt = _tile(T, 512)
    route = _route(h2t, lp['peer_w_q'].T.astype(BF16), lp['peer_keys'].astype(F32), tt)
    peer_out = _peer(h2t, route, lp['peer_u'].astype(BF16), lp['peer_v'].astype(BF16), tt)

    y = _ple(x1, peer_out, p.reshape(T, PLE_DIM), lp['w_ple_gate'].astype(BF16), lp['w_ple_proj'].astype(BF16),
             row(lp['g_ple']), row(lp['g_final']), _tile(T, 256))

    new_k = k_f32.reshape(bn, seq, N_HEADS, 2 * HEAD_DIM)
    new_v = v_f32.reshape(bn, seq, N_HEADS, V_DIM)
    hr = h_fin[:, 0, :N_STATE].reshape(bn, N_SSM_GROUPS, SSM_STATE)
    hi = h_fin[:, 0, N_STATE:].reshape(bn, N_SSM_GROUPS, SSM_STATE)
    return y.reshape(bn, seq, D_MODEL), new_k, new_v, hr, hi


def _trunk(x, p, h0_re, h0_im, k_past, v_past, lp):
    assert DEPTH == 1
    lambda_init = 0.8 - 0.6 * math.exp(-0.3 * 0)
    y, k_new, v_new, hr, hi = _layer(x, p[0], h0_re[0], h0_im[0], k_past[0], v_past[0], lp, lambda_init)
    return y, k_new[None], v_new[None], hr[None], hi[None]


def kernel(x_prompt, x_sample, p_prompt, p_sample, cache_k, cache_v, state_ssm_re, state_ssm_im,
           g_mix_norm, w_in, ssm_lambda_re, ssm_lambda_im, ssm_log_step, ssm_b_re, ssm_b_im,
           ssm_c_re, ssm_c_im, ssm_d, w_glu, diff_lambda_q1, diff_lambda_k1, diff_lambda_q2,
           diff_lambda_k2, g_subln, w_attn_out, w_out, g_ffn_norm, peer_w_q, peer_keys, peer_u,
           peer_v, g_ple_norm, w_ple_gate, w_ple_proj, g_final):
    lp = dict(g_mix=g_mix_norm[0], w_in=w_in[0], lam_re=ssm_lambda_re[0], lam_im=ssm_lambda_im[0],
              log_step=ssm_log_step[0], b_re=ssm_b_re[0], b_im=ssm_b_im[0], c_re=ssm_c_re[0],
              c_im=ssm_c_im[0], d=ssm_d[0], w_glu=w_glu[0], lq1=diff_lambda_q1[0],
              lk1=diff_lambda_k1[0], lq2=diff_lambda_q2[0], lk2=diff_lambda_k2[0],
              g_subln=g_subln[0], w_attn_out=w_attn_out[0], w_out=w_out[0], g_ffn=g_ffn_norm[0],
              peer_w_q=peer_w_q[0], peer_keys=peer_keys[0], peer_u=peer_u[0], peer_v=peer_v[0],
              g_ple=g_ple_norm[0], w_ple_gate=w_ple_gate[0], w_ple_proj=w_ple_proj[0], g_final=g_final)
    bn = x_prompt.shape[0]
    zeros_state = jnp.zeros((DEPTH, bn, N_SSM_GROUPS, SSM_STATE), F32)
    k_none = jnp.zeros((DEPTH, bn, 0, N_HEADS, 2 * HEAD_DIM), x_prompt.dtype)
    v_none = jnp.zeros((DEPTH, bn, 0, N_HEADS, V_DIM), x_prompt.dtype)
    y_p, k_p, v_p, r_p, i_p = _trunk(x_prompt, p_prompt, zeros_state, zeros_state, k_none, v_none, lp)
    y_s, k_s, v_s, r_s, i_s = _trunk(x_sample, p_sample, state_ssm_re, state_ssm_im, cache_k, cache_v, lp)
    return (y_p, y_s, k_p, v_p, r_p, i_p, k_s, v_s, r_s, i_s)
```

```python
import functools
import math

import jax
import jax.numpy as jnp
from jax import lax
from jax.experimental import pallas as pl
from jax.experimental.pallas import tpu as pltpu

F32 = jnp.float32
BF16 = jnp.bfloat16

D_MODEL = 2048
DEPTH = 1
CHUNK = 64
CHUNK_SHIFT = CHUNK.bit_length() - 1
assert 1 << CHUNK_SHIFT == CHUNK
PLE_DIM = 256
RMS_EPS = 1e-6
NEG_INF = -1e30
D_SSM = 1024
SSM_GROUP = 16
N_SSM_GROUPS = D_SSM // SSM_GROUP
SSM_STATE = 64
N_STATE = N_SSM_GROUPS * SSM_STATE
N_HEADS = 8
HEAD_DIM = 64
V_DIM = 2 * HEAD_DIM
ROPE_THETA = 10000.0
Q_COLS = N_HEADS * 2 * HEAD_DIM
V_COLS = N_HEADS * V_DIM
IN_COLS = D_SSM + 2 * Q_COLS + V_COLS + 2 * D_MODEL
PEER_HEADS = 8
N_KEYS = 128
N_EXPERTS = N_KEYS * N_KEYS
PEER_QUERY = 256
PEER_HALF = PEER_QUERY // 2
PEER_TOPK = 16

LANES = 128
SUBLANES = 8
VMEM_LIMIT = 56 * 1024 * 1024


def _cparams(*sem):
    return pltpu.CompilerParams(dimension_semantics=sem, vmem_limit_bytes=VMEM_LIMIT)


def _rms(x, g):
    return x * lax.rsqrt(jnp.mean(x * x, axis=-1, keepdims=True) + RMS_EPS) * g


def _gelu(x):
    return 0.5 * x * (1.0 + lax.erf(x * (2.0 ** -0.5)))


def _sigmoid(x):
    return 1.0 / (1.0 + jnp.exp(-x))


def _dot(a, b):
    return jnp.dot(a, b, preferred_element_type=F32)


IN_TN = 512
Q_SCALE = HEAD_DIM ** -0.5 * math.log2(math.e)
IN_MIX_COLS = D_SSM + 2 * Q_COLS + V_COLS


def _column_tiles(h_scr, w_ref):
    return [functools.partial(lambda c: _dot(h_scr[...], w_ref[:, c * IN_TN:(c + 1) * IN_TN]), c)
            for c in range(w_ref.shape[1] // IN_TN)]


def _inproj_mix_kernel(x_ref, g_ref, w_ref, cos_ref, sin_ref, u_ref, q_ref, kf_ref, kb_ref, vf_ref, vb_ref, h_scr):
    h_scr[...] = _rms(x_ref[...], g_ref[...]).astype(BF16)
    z = _column_tiles(h_scr, w_ref)

    def rope(t):
        n = t.shape[-1]
        lane = lax.broadcasted_iota(jnp.int32, t.shape, 1)
        first = jnp.bitwise_and(lane, HEAD_DIM - 1) < (HEAD_DIM // 2)
        partner = jnp.where(first, pltpu.roll(t, n - HEAD_DIM // 2, 1), pltpu.roll(t, HEAD_DIM // 2, 1))
        reps = n // LANES
        return t * jnp.tile(cos_ref[...], (1, reps)) + partner * jnp.tile(sin_ref[...], (1, reps))

    per = Q_COLS // IN_TN
    for c in range(per):
        cols = slice(c * IN_TN, (c + 1) * IN_TN)
        u_ref[:, cols] = z[c]().astype(BF16)
        q_ref[:, cols] = (rope(z[per + c]()) * Q_SCALE).astype(BF16)
        k = rope(z[2 * per + c]())
        kf_ref[:, cols] = k
        kb_ref[:, cols] = k.astype(BF16)
        v = z[3 * per + c]()
        vf_ref[:, cols] = v
        vb_ref[:, cols] = v.astype(BF16)


def _inproj_gate_kernel(x_ref, g_ref, w_ref, ga_ref, gb_ref, h_scr):
    h_scr[...] = _rms(x_ref[...], g_ref[...]).astype(BF16)
    z = _column_tiles(h_scr, w_ref)
    per = D_MODEL // IN_TN
    for c in range(per):
        cols = slice(c * IN_TN, (c + 1) * IN_TN)
        ga_ref[:, cols] = _sigmoid(z[c]()).astype(BF16)
        gb_ref[:, cols] = _sigmoid(z[per + c]()).astype(BF16)


def _inproj(x2d, g, w_bf, cos_t, sin_t, tm):
    assert D_SSM == Q_COLS == V_COLS
    T = x2d.shape[0]
    n_tab = cos_t.shape[0] // tm
    row = lambda n: pl.BlockSpec((tm, n), lambda i: (i, 0))
    const = lambda s: pl.BlockSpec(s, lambda i: (0, 0))
    table = pl.BlockSpec((tm, LANES), lambda i: (i % n_tab, 0))
    arr = lambda n, dt: jax.ShapeDtypeStruct((T, n), dt)
    common = dict(grid=(T // tm,), scratch_shapes=[pltpu.VMEM((tm, D_MODEL), BF16)],
                  compiler_params=_cparams("parallel"))
    u, q, kf, kb, vf, vb = pl.pallas_call(
        _inproj_mix_kernel,
        in_specs=[row(D_MODEL), const((1, D_MODEL)), const((D_MODEL, IN_MIX_COLS)), table, table],
        out_specs=[row(Q_COLS)] * 6,
        out_shape=[arr(Q_COLS, BF16), arr(Q_COLS, BF16), arr(Q_COLS, F32), arr(Q_COLS, BF16),
                   arr(Q_COLS, F32), arr(Q_COLS, BF16)],
        name="inproj_mix", **common,
    )(x2d, g, w_bf[:, :IN_MIX_COLS], cos_t, sin_t)
    ga, gb = pl.pallas_call(
        _inproj_gate_kernel,
        in_specs=[row(D_MODEL), const((1, D_MODEL)), const((D_MODEL, 2 * D_MODEL))],
        out_specs=[row(D_MODEL)] * 2,
        out_shape=[arr(D_MODEL, BF16)] * 2,
        name="inproj_gate", **common,
    )(x2d, g, w_bf[:, IN_MIX_COLS:])
    return u, q, kf, kb, vf, vb, ga, gb


def _rope_tables(seq, past, rows):
    half = HEAD_DIM // 2
    inv = ROPE_THETA ** (-jnp.arange(half, dtype=F32) * 2.0 / HEAD_DIM)
    pos = (past + (jnp.arange(rows, dtype=jnp.int32) % seq)).astype(F32)
    ang = pos[:, None] * inv[None, :]
    cos, sin = jnp.cos(ang), jnp.sin(ang)
    reps = LANES // HEAD_DIM
    cos_t = jnp.tile(jnp.concatenate([cos, cos], axis=-1), (1, reps))
    sin_t = jnp.tile(jnp.concatenate([-sin, sin], axis=-1), (1, reps))
    return cos_t, sin_t


S5_COLS = 512
S5_FIX_UNROLL = SUBLANES
S5_UBLK = LANES
S5_N_UBLK = D_SSM // S5_UBLK
S5_XBLK = (S5_UBLK // SSM_GROUP) * SSM_STATE
S5_YBLK = 256
S5_N_YBLK = D_SSM // S5_YBLK
S5_HBLK = (S5_YBLK // SSM_GROUP) * SSM_STATE


def _s5_kernel(u_ref, h0_ref, perm_ref, permt_ref, wx_ref, apow_ref, cr_ref, ci_ref, d_ref,
               y_ref, hfin_ref, x_scr, c_scr, carry_scr, *, seg_len):
    t = pl.program_id(1)
    rows = SUBLANES * seg_len
    n = N_STATE

    @pl.when(t == 0)
    def _():
        carry_scr[...] = h0_ref[...]

    up = _dot(perm_ref[...], u_ref[...])
    upb = up.astype(BF16)
    assert S5_COLS == S5_XBLK
    for a in range(S5_N_UBLK):
        xa = _dot(upb[:, a * S5_UBLK:(a + 1) * S5_UBLK], wx_ref[a])
        lo = a * S5_XBLK
        x_scr[:, lo:lo + S5_XBLK] = xa[:, :S5_XBLK]
        x_scr[:, n + lo:n + lo + S5_XBLK] = xa[:, S5_XBLK:]

        ar = jnp.broadcast_to(apow_ref[0:1, lo:lo + S5_COLS], (SUBLANES, S5_COLS))
        ai = jnp.broadcast_to(apow_ref[0:1, n + lo:n + lo + S5_COLS], (SUBLANES, S5_COLS))
        hr = hi = jnp.zeros((SUBLANES, S5_COLS), F32)
        for k in range(seg_len):
            step = slice(k * SUBLANES, (k + 1) * SUBLANES)
            xr = x_scr[step, lo:lo + S5_COLS]
            xi = x_scr[step, n + lo:n + lo + S5_COLS]
            hr, hi = ar * hr - ai * hi + xr, ar * hi + ai * hr + xi
            x_scr[step, lo:lo + S5_COLS] = hr
            x_scr[step, n + lo:n + lo + S5_COLS] = hi

    alr = apow_ref[seg_len - 1:seg_len, :n]
    ali = apow_ref[seg_len - 1:seg_len, n:]
    cr = carry_scr[:, :n]
    ci = carry_scr[:, n:]
    for s in range(SUBLANES):
        c_scr[s:s + 1, :n] = cr
        c_scr[s:s + 1, n:] = ci
        lr = x_scr[rows - SUBLANES + s:rows - SUBLANES + s + 1, :n]
        li = x_scr[rows - SUBLANES + s:rows - SUBLANES + s + 1, n:]
        cr, ci = alr * cr - ali * ci + lr, alr * ci + ali * cr + li
    carry_scr[:, :n] = cr
    carry_scr[:, n:] = ci

    for j in range(S5_N_YBLK):
        for c in range(j * (S5_HBLK // S5_COLS), (j + 1) * (S5_HBLK // S5_COLS)):
            lo = c * S5_COLS
            sr = c_scr[:, lo:lo + S5_COLS]
            si = c_scr[:, n + lo:n + lo + S5_COLS]
            unroll = math.gcd(seg_len, S5_FIX_UNROLL)

            def fix_steps(kk, carry, lo=lo, sr=sr, si=si, unroll=unroll):
                k0 = kk * unroll if isinstance(kk, int) else pl.multiple_of(kk * unroll, unroll)
                pr_all = apow_ref[pl.ds(k0, unroll), lo:lo + S5_COLS]
                pi_all = apow_ref[pl.ds(k0, unroll), n + lo:n + lo + S5_COLS]
                for d in range(unroll):
                    r0 = (k0 + d) * SUBLANES
                    step = pl.ds(r0 if isinstance(r0, int) else pl.multiple_of(r0, SUBLANES), SUBLANES)
                    pr, pi = pr_all[d:d + 1], pi_all[d:d + 1]
                    x_scr[step, lo:lo + S5_COLS] += pr * sr - pi * si
                    x_scr[step, n + lo:n + lo + S5_COLS] += pr * si + pi * sr
                return carry

            if seg_len == unroll:
                fix_steps(0, 0)
            else:
                lax.fori_loop(0, seg_len // unroll, fix_steps, 0)

        hr = x_scr[:, j * S5_HBLK:(j + 1) * S5_HBLK].astype(BF16)
        hi = x_scr[:, n + j * S5_HBLK:n + (j + 1) * S5_HBLK].astype(BF16)
        yj = (_dot(hr, cr_ref[j]) + _dot(hi, ci_ref[j])
              + d_ref[:, j * S5_YBLK:(j + 1) * S5_YBLK] * up[:, j * S5_YBLK:(j + 1) * S5_YBLK])
        yj = _gelu(yj).astype(BF16)
        y_ref[:, j * S5_YBLK:(j + 1) * S5_YBLK] = _dot(permt_ref[...], yj).astype(BF16)

    @pl.when(t == pl.num_programs(1) - 1)
    def _():
        hfin_ref[...] = carry_scr[...]


def _s5_tables(lp, seg_len):
    g, p, c = N_SSM_GROUPS, SSM_STATE, SSM_GROUP
    dt = jnp.exp(lp['log_step'].astype(F32))[:, None]
    lr, li = lp['lam_re'].astype(F32), lp['lam_im'].astype(F32)
    mag = jnp.exp(lr * dt)
    ar, ai = mag * jnp.cos(li * dt), mag * jnp.sin(li * dt)
    den = lr * lr + li * li
    fr = ((ar - 1.0) * lr + ai * li) / den
    fi = (ai * lr - (ar - 1.0) * li) / den
    br, bi = lp['b_re'].astype(F32), lp['b_im'].astype(F32)
    bbr = fr[..., None] * br - fi[..., None] * bi
    bbi = fr[..., None] * bi + fi[..., None] * br

    gpb = S5_UBLK // c
    eye = jnp.eye(gpb, dtype=F32)

    def in_blocks(bb):
        bb = bb.reshape(S5_N_UBLK, gpb, p, c)
        return jnp.einsum('xy,axpc->axcyp', eye, bb).reshape(S5_N_UBLK, S5_UBLK, S5_XBLK)

    wx = jnp.concatenate([in_blocks(bbr), in_blocks(bbi)], axis=-1).astype(BF16)

    pr, pi = ar.reshape(1, g * p), ai.reshape(1, g * p)
    while pr.shape[0] < seg_len:
        tr, ti = pr[-1:], pi[-1:]
        pr, pi = (jnp.concatenate([pr, pr * tr - pi * ti], axis=0),
                  jnp.concatenate([pi, pr * ti + pi * tr], axis=0))
    apow = jnp.concatenate([pr[:seg_len], pi[:seg_len]], axis=-1)

    gpy = S5_YBLK // c
    eye_y = jnp.eye(gpy, dtype=F32)

    def out_blocks(cm):
        cm = cm.astype(F32).reshape(S5_N_YBLK, gpy, c, p)
        return jnp.einsum('xy,jxcp->jxpyc', eye_y, cm).reshape(S5_N_YBLK, S5_HBLK, S5_YBLK)

    cr = out_blocks(lp['c_re']).astype(BF16)
    ci = (-out_blocks(lp['c_im'])).astype(BF16)
    d = lp['d'].astype(F32).reshape(1, D_SSM)
    return wx, apow, cr, ci, d


def _s5(u, h0, tables, seg_len):
    bn, seq, _ = u.shape
    rows = SUBLANES * seg_len
    wx, apow, cr, ci, d = tables
    r = jnp.arange(rows)
    perm = (r[None, :] == ((r % SUBLANES) * seg_len + r // SUBLANES)[:, None]).astype(BF16)
    const2 = lambda b, t: (0, 0)
    const3 = lambda b, t: (0, 0, 0)
    return pl.pallas_call(
        functools.partial(_s5_kernel, seg_len=seg_len),
        grid=(bn, seq // rows),
        in_specs=[
            pl.BlockSpec((None, rows, D_SSM), lambda b, t: (b, t, 0)),
            pl.BlockSpec((None, 1, 2 * N_STATE), lambda b, t: (b, 0, 0)),
            pl.BlockSpec((rows, rows), const2),
            pl.BlockSpec((rows, rows), const2),
            pl.BlockSpec(wx.shape, const3),
            pl.BlockSpec(apow.shape, const2),
            pl.BlockSpec(cr.shape, const3),
            pl.BlockSpec(ci.shape, const3),
            pl.BlockSpec((1, D_SSM), const2),
        ],
        out_specs=[
            pl.BlockSpec((None, rows, D_SSM), lambda b, t: (b, t, 0)),
            pl.BlockSpec((None, 1, 2 * N_STATE), lambda b, t: (b, 0, 0)),
        ],
        out_shape=[jax.ShapeDtypeStruct((bn, seq, D_SSM), BF16),
                   jax.ShapeDtypeStruct((bn, 1, 2 * N_STATE), F32)],
        scratch_shapes=[pltpu.VMEM((rows, 2 * N_STATE), F32),
                        pltpu.VMEM((SUBLANES, 2 * N_STATE), F32),
                        pltpu.VMEM((1, 2 * N_STATE), F32)],
        compiler_params=_cparams("parallel", "arbitrary"),
        name="s5",
    )(u, h0, perm, perm.T, wx, apow, cr, ci, d)


GLU_TN = 512


def _glu_kernel(y_ref, w_ref, ga_ref, o_ref):
    y = y_ref[...]
    for c in range(D_MODEL // GLU_TN):
        cols = slice(c * GLU_TN, (c + 1) * GLU_TN)
        a = _dot(y, w_ref[:, cols])
        b = _dot(y, w_ref[:, D_MODEL + c * GLU_TN:D_MODEL + (c + 1) * GLU_TN])
        o_ref[:, cols] = (ga_ref[:, cols].astype(F32) * a * _sigmoid(b)).astype(BF16)


def _glu(y, w_glu_bf, sig_a, tm):
    T = y.shape[0]
    return pl.pallas_call(
        _glu_kernel,
        grid=(T // tm,),
        in_specs=[
            pl.BlockSpec((tm, D_SSM), lambda i: (i, 0)),
            pl.BlockSpec((D_SSM, 2 * D_MODEL), lambda i: (0, 0)),
            pl.BlockSpec((tm, D_MODEL), lambda i: (i, 0)),
        ],
        out_specs=pl.BlockSpec((tm, D_MODEL), lambda i: (i, 0)),
        out_shape=jax.ShapeDtypeStruct((T, D_MODEL), BF16),
        compiler_params=_cparams("parallel"),
        name="glu",
    )(y, w_glu_bf, sig_a)


def _attn_kernel(q_ref, k_ref, v_ref, lq1_ref, lk1_ref, lq2_ref, lk2_ref, g_ref, o_ref,
                 qt_scr, s0_scr, s1_scr, m_scr, l_scr, acc_scr, *, tq, tk, past, paired, lambda_init):
    i = pl.program_id(2)
    nk = k_ref.shape[0] // tk
    q = q_ref[...].astype(F32)
    lane = lax.broadcasted_iota(jnp.int32, q.shape, 1)
    qs = jnp.concatenate([jnp.where(lane < HEAD_DIM, q, 0.0), jnp.where(lane >= HEAD_DIM, q, 0.0)], axis=0)
    qt_scr[...] = qs.T.astype(BF16)

    m_scr[...] = jnp.full(m_scr.shape, NEG_INF, F32)
    l_scr[...] = jnp.zeros(l_scr.shape, F32)
    acc_scr[...] = jnp.zeros(acc_scr.shape, F32)

    q_lo = past + i * tq
    min_qc = q_lo // CHUNK
    max_qc = (q_lo + tq - 1) // CHUNK
    n_proc = jnp.minimum(nk, (max_qc * CHUNK + CHUNK - 1) // tk + 1)
    n_full = jnp.clip((min_qc * CHUNK + CHUNK) // tk, 0, n_proc)

    def scores(j, s_ref):
        k0 = pl.multiple_of(j * tk, tk)
        s_ref[...] = _dot(k_ref[pl.ds(k0, tk), :], qt_scr[...])

    def update(j, s_ref, cols):
        k0 = pl.multiple_of(j * tk, tk)
        vb = v_ref[pl.ds(k0, tk), :]
        for lo, hi, masked in cols:
            s = s_ref[:, lo:hi]
            if masked:
                kpos = k0 + lax.broadcasted_iota(jnp.int32, s.shape, 0)
                qpos = q_lo + jnp.bitwise_and(lo + lax.broadcasted_iota(jnp.int32, s.shape, 1), tq - 1)
                s = jnp.where(jnp.right_shift(kpos, CHUNK_SHIFT) <= jnp.right_shift(qpos, CHUNK_SHIFT), s, NEG_INF)
            m_old = m_scr[:, lo:hi]
            m_new = jnp.maximum(m_old, jnp.max(s, axis=0, keepdims=True))
            alpha = jnp.exp2(m_old - m_new)
            p = jnp.exp2(s - m_new)
            l_scr[:, lo:hi] = alpha * l_scr[:, lo:hi] + jnp.sum(p, axis=0, keepdims=True)
            pv = lax.dot_general(vb, p.astype(BF16), (((0,), (0,)), ((), ())), preferred_element_type=F32)
            acc_scr[:, lo:hi] = alpha * acc_scr[:, lo:hi] + pv
            m_scr[:, lo:hi] = m_new

    everything = ((0, 2 * tq, False),)
    everything_masked = ((0, 2 * tq, True),)

    if paired:
        scores(0, s0_scr)

        def pair(p, c):
            scores(2 * p + 1, s1_scr)
            update(2 * p, s0_scr, everything)
            scores(2 * p + 2, s0_scr)
            update(2 * p + 1, s1_scr, everything)
            return c

        lax.fori_loop(0, i, pair, 0)
        scores(2 * i + 1, s1_scr)
        update(2 * i, s0_scr, tuple(r for c0 in (0, tq) for r in ((c0, c0 + tk, True), (c0 + tk, c0 + tq, False))))
        update(2 * i + 1, s1_scr, tuple((c0 + tk, c0 + tq, True) for c0 in (0, tq)))
    else:
        def full_body(j, c):
            scores(j, s0_scr)
            update(j, s0_scr, everything)
            return c

        def masked_body(j, c):
            scores(j, s0_scr)
            update(j, s0_scr, everything_masked)
            return c

        lax.fori_loop(0, n_full, full_body, 0)
        lax.fori_loop(n_full, n_proc, masked_body, 0)

    lam = (jnp.exp(jnp.sum(lq1_ref[...] * lk1_ref[...], axis=-1, keepdims=True))
           - jnp.exp(jnp.sum(lq2_ref[...] * lk2_ref[...], axis=-1, keepdims=True)) + lambda_init)
    ot = acc_scr[:, :tq] / l_scr[:, :tq] - lam * (acc_scr[:, tq:] / l_scr[:, tq:])
    o_ref[...] = (_rms(ot.T, g_ref[...]) * (1.0 - lambda_init)).astype(BF16)


def _attention(q, k, v, lp, tq, tk, past, lambda_init):
    bn, seq, _ = q.shape
    lk = k.shape[1]
    vec = lambda a: a.astype(F32).reshape(1, -1)
    small = lambda n: pl.BlockSpec((1, n), lambda b, h, i: (0, 0))
    paired = past == 0 and tq == 2 * tk and lk == seq
    s_scr = pltpu.VMEM((tk, 2 * tq), F32)
    return pl.pallas_call(
        functools.partial(_attn_kernel, tq=tq, tk=tk, past=past, paired=paired, lambda_init=lambda_init),
        grid=(bn, N_HEADS, seq // tq),
        in_specs=[
            pl.BlockSpec((None, tq, V_DIM), lambda b, h, i: (b, i, h)),
            pl.BlockSpec((None, lk, V_DIM), lambda b, h, i: (b, 0, h)),
            pl.BlockSpec((None, lk, V_DIM), lambda b, h, i: (b, 0, h)),
            small(HEAD_DIM), small(HEAD_DIM), small(HEAD_DIM), small(HEAD_DIM), small(V_DIM),
        ],
        out_specs=pl.BlockSpec((None, tq, V_DIM), lambda b, h, i: (b, i, h)),
        out_shape=jax.ShapeDtypeStruct((bn, seq, V_COLS), BF16),
        scratch_shapes=[pltpu.VMEM((V_DIM, 2 * tq), BF16), s_scr, s_scr, pltpu.VMEM((1, 2 * tq), F32),
                        pltpu.VMEM((1, 2 * tq), F32), pltpu.VMEM((V_DIM, 2 * tq), F32)],
        compiler_params=_cparams("parallel", "parallel", "arbitrary"),
        name="diff_attn",
    )(q, k, v, vec(lp['lq1']), vec(lp['lk1']), vec(lp['lq2']), vec(lp['lk2']), vec(lp['g_subln']))


def _attn_cached_kernel(q_ref, kc_ref, vc_ref, kn_ref, vn_ref, lq1_ref, lk1_ref, lq2_ref, lk2_ref, g_ref, o_ref,
                        qt_scr, sc_scr, sn_scr, m_scr, l_scr, acc_scr, *, tq, past, lambda_init):
    lam = (jnp.exp(jnp.sum(lq1_ref[...] * lk1_ref[...], axis=-1, keepdims=True))
           - jnp.exp(jnp.sum(lq2_ref[...] * lk2_ref[...], axis=-1, keepdims=True)) + lambda_init)

    def update(s_ref, vb, k0, masked):
        s = s_ref[...]
        if masked:
            kpos = k0 + lax.broadcasted_iota(jnp.int32, s.shape, 0)
            qpos = past + jnp.bitwise_and(lax.broadcasted_iota(jnp.int32, s.shape, 1), tq - 1)
            s = jnp.where(jnp.right_shift(kpos, CHUNK_SHIFT) <= jnp.right_shift(qpos, CHUNK_SHIFT), s, NEG_INF)
        m_old = m_scr[...]
        m_new = jnp.maximum(m_old, jnp.max(s, axis=0, keepdims=True))
        alpha = jnp.exp2(m_old - m_new)
        p = jnp.exp2(s - m_new)
        l_scr[...] = alpha * l_scr[...] + jnp.sum(p, axis=0, keepdims=True)
        pv = lax.dot_general(vb, p.astype(BF16), (((0,), (0,)), ((), ())), preferred_element_type=F32)
        acc_scr[...] = alpha * acc_scr[...] + pv
        m_scr[...] = m_new

    for h in range(N_HEADS):
        cols = slice(h * V_DIM, (h + 1) * V_DIM)
        q = q_ref[:, cols].astype(F32)
        lane = lax.broadcasted_iota(jnp.int32, q.shape, 1)
        qs = jnp.concatenate([jnp.where(lane < HEAD_DIM, q, 0.0), jnp.where(lane >= HEAD_DIM, q, 0.0)], axis=0)
        qt_scr[...] = qs.T.astype(BF16)
        m_scr[...] = jnp.full(m_scr.shape, NEG_INF, F32)
        l_scr[...] = jnp.zeros(l_scr.shape, F32)
        acc_scr[...] = jnp.zeros(acc_scr.shape, F32)

        sc_scr[...] = _dot(kc_ref[:, h, :].astype(BF16), qt_scr[...])
        update(sc_scr, vc_ref[:, h, :].astype(BF16), 0, False)
        sn_scr[...] = _dot(kn_ref[:, cols], qt_scr[...])
        update(sn_scr, vn_ref[:, cols], past, True)

        ot = acc_scr[:, :tq] / l_scr[:, :tq] - lam * (acc_scr[:, tq:] / l_scr[:, tq:])
        o_ref[:, cols] = (_rms(ot.T, g_ref[...]) * (1.0 - lambda_init)).astype(BF16)


def _attention_cached(q, cache_k, cache_v, k_new, v_new, lp, tq, lambda_init):
    bn, past = cache_k.shape[:2]
    seq = k_new.shape[1]
    assert q.shape[1] == tq
    vec = lambda a: a.astype(F32).reshape(1, -1)
    small = lambda n: pl.BlockSpec((1, n), lambda b: (0, 0))
    rows = lambda n: pl.BlockSpec((None, n, Q_COLS), lambda b: (b, 0, 0))
    cache = pl.BlockSpec((None, past, N_HEADS, V_DIM), lambda b: (b, 0, 0, 0))
    return pl.pallas_call(
        functools.partial(_attn_cached_kernel, tq=tq, past=past, lambda_init=lambda_init),
        grid=(bn,),
        in_specs=[rows(tq), cache, cache, rows(seq), rows(seq),
                  small(HEAD_DIM), small(HEAD_DIM), small(HEAD_DIM), small(HEAD_DIM), small(V_DIM)],
        out_specs=rows(tq),
        out_shape=jax.ShapeDtypeStruct((bn, tq, V_COLS), BF16),
        scratch_shapes=[pltpu.VMEM((V_DIM, 2 * tq), BF16), pltpu.VMEM((past, 2 * tq), F32),
                        pltpu.VMEM((seq, 2 * tq), F32), pltpu.VMEM((1, 2 * tq), F32),
                        pltpu.VMEM((1, 2 * tq), F32), pltpu.VMEM((V_DIM, 2 * tq), F32)],
        compiler_params=_cparams("parallel"),
        name="diff_attn_cached",
    )(q, cache_k, cache_v, k_new, v_new,
      vec(lp['lq1']), vec(lp['lk1']), vec(lp['lq2']), vec(lp['lk2']), vec(lp['g_subln']))


def _merge_kernel(x_ref, o_ref, ga_ref, sb_ref, wa_ref, wo_ref, g_ref, x1_ref, h2t_ref):
    branch_b = _dot(o_ref[...], wa_ref[...])
    merged = ga_ref[...].astype(F32) + sb_ref[...].astype(F32) * branch_b
    x1 = x_ref[...] + _dot(merged.astype(BF16), wo_ref[...])
    x1_ref[...] = x1
    h2t_ref[...] = _rms(x1, g_ref[...]).T.astype(BF16)


def _merge(x2d, o, gated_a, sig_b, wa_bf, wo_bf, g_ffn, tm):
    T = x2d.shape[0]
    row = lambda n: pl.BlockSpec((tm, n), lambda i: (i, 0))
    const = lambda s: pl.BlockSpec(s, lambda i: (0, 0))
    return pl.pallas_call(
        _merge_kernel,
        grid=(T // tm,),
        in_specs=[row(D_MODEL), row(V_COLS), row(D_MODEL), row(D_MODEL),
                  const((V_COLS, D_MODEL)), const((D_MODEL, D_MODEL)), const((1, D_MODEL))],
        out_specs=[row(D_MODEL), pl.BlockSpec((D_MODEL, tm), lambda i: (0, i))],
        out_shape=[jax.ShapeDtypeStruct((T, D_MODEL), F32), jax.ShapeDtypeStruct((D_MODEL, T), BF16)],
        compiler_params=_cparams("parallel"),
        name="merge_out",
    )(x2d, o, gated_a, sig_b, wa_bf, wo_bf, g_ffn)


ROUTE_LC = 256
TAKEN = -3.0e38


def _split_bf16(x):
    hi = x.astype(BF16)
    return hi, (x - hi.astype(F32)).astype(BF16)


def _oddeven_merge(lo, hi, r):
    step = r * 2
    if step < hi - lo:
        yield from _oddeven_merge(lo, hi, step)
        yield from _oddeven_merge(lo + r, hi, step)
        yield from [(i, i + r) for i in range(lo + r, hi - r, step)]
    else:
        yield (lo, lo + r)


def _oddeven_merge_sort(lo, hi):
    if hi - lo >= 1:
        mid = lo + (hi - lo) // 2
        yield from _oddeven_merge_sort(lo, mid)
        yield from _oddeven_merge_sort(mid + 1, hi)
        yield from _oddeven_merge(lo, hi, 1)


def _top_rows_sorted(s, k):
    n = s.shape[0] // SUBLANES
    rows = [s[SUBLANES * g:SUBLANES * (g + 1), :] for g in range(n)]
    n_pow2 = 1 << (n - 1).bit_length()
    for i, j in _oddeven_merge_sort(0, n_pow2 - 1):
        if j < n:
            rows[i], rows[j] = jnp.maximum(rows[i], rows[j]), jnp.minimum(rows[i], rows[j])
    taken = jnp.full_like(rows[0], TAKEN)
    out = []
    for r in range(k):
        m = jnp.max(rows[0], axis=0, keepdims=True)
        out.append(m)
        hit = rows[0] == m
        depth = min(n, k - r - 1)
        for d in range(depth):
            rows[d] = jnp.where(hit, rows[d + 1] if d + 1 < n else taken, rows[d])
    return out


def _stack_rows(rows):
    n, w = len(rows), rows[0].shape[1]
    idx = lax.broadcasted_iota(jnp.int32, (n, w), 0)
    out = jnp.zeros((n, w), F32)
    for r, v in enumerate(rows):
        out = jnp.where(idx == r, v, out)
    return out


def _route_kernel(h2t_ref, wq_ref, keys_ref, thr_ref, s2_ref, e1_ref, e2_ref, s1_scr):
    for h in range(PEER_HEADS):
        _route_head(h, h2t_ref, wq_ref, keys_ref, thr_ref, s2_ref, e1_ref, e2_ref, s1_scr)


def _route_head(h, h2t_ref, wq_ref, keys_ref, thr_ref, s2_ref, e1_ref, e2_ref, s1_scr):
    tt = h2t_ref.shape[1]
    n_top = PEER_TOPK + 1
    qt = _dot(wq_ref[h * PEER_QUERY:(h + 1) * PEER_QUERY, :], h2t_ref[...])
    halves = []
    for c in range(2):
        q_hi, q_lo = _split_bf16(qt[c * PEER_HALF:(c + 1) * PEER_HALF, :])
        k_hi, k_lo = _split_bf16(keys_ref[h, c])
        halves.append(_dot(k_hi, q_hi) + _dot(k_hi, q_lo) + _dot(k_lo, q_hi))
    s1_scr[h] = halves[0]
    s2_ref[h] = halves[1]

    width = min(tt, ROUTE_LC)
    for lc in range(tt // width):
        sl = slice(lc * width, (lc + 1) * width)
        s1 = s1_scr[h, :, sl]
        s2 = s2_ref[h, :, sl]
        a = _top_rows_sorted(s1, n_top)
        b = _top_rows_sorted(s2, n_top)
        taken = jnp.full_like(b[0], TAKEN)
        b_all = _stack_rows(b + [taken] * (3 * SUBLANES - n_top))
        tail = [a[i] + b[j] for i in range(4, n_top) for j in range(n_top // (i + 1))]
        tail += [taken] * (-len(tail) % SUBLANES)
        cand = jnp.concatenate([a[0] + b_all] + [a[i] + b_all[:SUBLANES] for i in range(1, 4)]
                               + [_stack_rows(tail)], axis=0)
        top = _top_rows_sorted(cand, n_top)
        z = jnp.zeros_like(top[0])
        for r in range(PEER_TOPK):
            z = z + jnp.exp(top[r] - top[0])
        tau = 0.5 * (top[PEER_TOPK - 1] + top[PEER_TOPK])
        grouped = (N_KEYS // SUBLANES, SUBLANES, width)
        thr_ref[h, :, :, sl] = (tau - s1).reshape(grouped)
        e1_ref[h, :, :, sl] = (jnp.exp(s1 - a[0]) * (0.5 / z)).reshape(grouped)
        e2_ref[h, :, sl] = jnp.exp(s2 - b[0])


def _route(h2t, wq_t_bf, keys, tt):
    T = h2t.shape[1]
    arr = jax.ShapeDtypeStruct((PEER_HEADS, N_KEYS, T), F32)
    spec = pl.BlockSpec((PEER_HEADS, N_KEYS, tt), lambda i: (0, 0, i))
    n_grp = N_KEYS // SUBLANES
    garr = jax.ShapeDtypeStruct((PEER_HEADS, n_grp, SUBLANES, T), F32)
    gspec = pl.BlockSpec((PEER_HEADS, n_grp, SUBLANES, tt), lambda i: (0, 0, 0, i))
    return pl.pallas_call(
        _route_kernel,
        grid=(T // tt,),
        in_specs=[
            pl.BlockSpec((D_MODEL, tt), lambda i: (0, i)),
            pl.BlockSpec((PEER_HEADS * PEER_QUERY, D_MODEL), lambda i: (0, 0)),
            pl.BlockSpec((PEER_HEADS, 2, N_KEYS, PEER_HALF), lambda i: (0, 0, 0, 0)),
        ],
        out_specs=[gspec, spec, gspec, spec],
        out_shape=[garr, arr, garr, arr],
        scratch_shapes=[pltpu.VMEM((PEER_HEADS, N_KEYS, tt), F32)],
        compiler_params=_cparams("parallel"),
        name="peer_route",
    )(h2t, wq_t_bf, keys)


PEER_SUB = 512
PEER_EB = 2 * PEER_SUB
PEER_ROWS = PEER_SUB // N_KEYS
N_SUB = N_EXPERTS // PEER_SUB
PEER_JH = 64


def _peer_kernel(h2t_ref, thr_ref, s2_ref, e1_ref, e2_ref, u_ref, v_ref, o_ref,
                 act0, act1, coef0, coef1, acc_scr):
    g = pl.program_id(1)
    tt = h2t_ref.shape[1]

    def stage_a(half, act):
        act[...] = _dot(u_ref[half * PEER_SUB:(half + 1) * PEER_SUB, :], h2t_ref[...])

    def stage_b(b, act, coef):
        b = jnp.clip(b, 0, N_SUB - 1)
        grp = b // 2
        odd = (b % 2) == 1
        for lc in range(tt // LANES):
            ln = slice(lc * LANES, (lc + 1) * LANES)
            for jh in range(N_KEYS // PEER_JH):
                js = slice(jh * PEER_JH, (jh + 1) * PEER_JH)
                gates = [jnp.zeros((PEER_JH, LANES), F32) for _ in range(PEER_ROWS)]
                for h in range(PEER_HEADS):
                    s2 = s2_ref[h, js, ln]
                    e2 = e2_ref[h, js, ln]
                    thr_grp = thr_ref[h, grp, :, ln]
                    e1_grp = e1_ref[h, grp, :, ln]
                    thr_rows = jnp.where(odd, thr_grp[PEER_ROWS:], thr_grp[:PEER_ROWS])
                    e1_rows = jnp.where(odd, e1_grp[PEER_ROWS:], e1_grp[:PEER_ROWS])
                    for r in range(PEER_ROWS):
                        gates[r] = gates[r] + jnp.where(s2 >= thr_rows[r:r + 1], e1_rows[r:r + 1] * e2, 0.0)
                for r in range(PEER_ROWS):
                    rows = slice(r * N_KEYS + jh * PEER_JH, r * N_KEYS + (jh + 1) * PEER_JH)
                    x = act[rows, ln]
                    coef[rows, ln] = gates[r] * (x * (1.0 + lax.erf(x * (2.0 ** -0.5))))

    def stage_c():
        coef = jnp.concatenate([coef0[...].astype(BF16), coef1[...].astype(BF16)], axis=0)
        acc_scr[...] += lax.dot_general(coef, v_ref[...], (((0,), (0,)), ((), ())), preferred_element_type=F32)

    last = pl.num_programs(1) - 1

    @pl.when(g == 0)
    def _():
        acc_scr[...] = jnp.zeros(acc_scr.shape, F32)
        stage_a(0, act0)
        stage_a(1, act1)
        stage_b(2 * g, act0, coef0)

    @pl.when(jnp.logical_and(g > 0, g < last))
    def _():
        stage_a(0, act0)
        stage_b(2 * g - 1, act1, coef1)
        stage_a(1, act1)
        stage_c()
        stage_b(2 * g, act0, coef0)

    @pl.when(g == last)
    def _():
        stage_b(2 * g - 1, act1, coef1)
        stage_c()
        o_ref[...] = acc_scr[...].astype(BF16)


def _peer(h2t, route, u_bf, v_bf, tt):
    T = h2t.shape[1]
    thr, s2, e1, e2 = route
    n_eb = N_EXPERTS // PEER_EB
    spec = pl.BlockSpec((PEER_HEADS, N_KEYS, tt), lambda i, g: (0, 0, i))
    gspec = pl.BlockSpec((PEER_HEADS, N_KEYS // SUBLANES, SUBLANES, tt), lambda i, g: (0, 0, 0, i))
    assert SUBLANES == 2 * PEER_ROWS
    return pl.pallas_call(
        _peer_kernel,
        grid=(T // tt, n_eb + 1),
        in_specs=[
            pl.BlockSpec((D_MODEL, tt), lambda i, g: (0, i)),
            gspec, spec, gspec, spec,
            pl.BlockSpec((PEER_EB, D_MODEL), lambda i, g: (jnp.minimum(g, n_eb - 1), 0)),
            pl.BlockSpec((PEER_EB, D_MODEL), lambda i, g: (jnp.maximum(g - 1, 0), 0)),
        ],
        out_specs=pl.BlockSpec((tt, D_MODEL), lambda i, g: (i, 0)),
        out_shape=jax.ShapeDtypeStruct((T, D_MODEL), BF16),
        scratch_shapes=[pltpu.VMEM((PEER_SUB, tt), F32), pltpu.VMEM((PEER_SUB, tt), F32),
                        pltpu.VMEM((PEER_SUB, tt), F32), pltpu.VMEM((PEER_SUB, tt), F32),
                        pltpu.VMEM((tt, D_MODEL), F32)],
        compiler_params=_cparams("parallel", "arbitrary"),
        name="peer_dense",
    )(h2t, thr, s2, e1, e2, u_bf, v_bf)


def _ple_kernel(x1_ref, po_ref, p_ref, wg_ref, wp_ref, gp_ref, gf_ref, y_ref):
    x2 = x1_ref[...] + po_ref[...].astype(F32)
    h3 = _rms(x2, gp_ref[...]).astype(BF16)
    gate = _sigmoid(_dot(h3, wg_ref[...]))
    proj = _dot(p_ref[...].astype(BF16), wp_ref[...])
    y_ref[...] = _rms(x2 + proj * gate, gf_ref[...])


def _ple(x1, peer_out, p2d, wg_bf, wp_bf, g_ple, g_final, tm):
    T = x1.shape[0]
    row = lambda n: pl.BlockSpec((tm, n), lambda i: (i, 0))
    const = lambda s: pl.BlockSpec(s, lambda i: (0, 0))
    return pl.pallas_call(
        _ple_kernel,
        grid=(T // tm,),
        in_specs=[row(D_MODEL), row(D_MODEL), row(PLE_DIM),
                  const((D_MODEL, D_MODEL)), const((PLE_DIM, D_MODEL)), const((1, D_MODEL)), const((1, D_MODEL))],
        out_specs=row(D_MODEL),
        out_shape=jax.ShapeDtypeStruct((T, D_MODEL), F32),
        compiler_params=_cparams("parallel"),
        name="ple_final",
    )(x1, peer_out, p2d, wg_bf, wp_bf, g_ple, g_final)


def _tile(n, pref):
    t = min(n, pref)
    assert n % t == 0, (n, t)
    return t


def _layer(x, p, h0_re, h0_im, k_past, v_past, lp, lambda_init):
    bn, seq, _ = x.shape
    past = k_past.shape[1]
    T = bn * seq
    x2d = x.reshape(T, D_MODEL)
    row = lambda a: a.astype(F32).reshape(1, -1)

    tm_in = _tile(T, 256)
    cos_t, sin_t = _rope_tables(seq, past, max(seq, tm_in))
    u, q, k_f32, k_bf, v_f32, v_bf, sig_a, sig_b = _inproj(
        x2d, row(lp['g_mix']), lp['w_in'].astype(BF16), cos_t, sin_t, tm_in)

    seg_len = _tile(seq // SUBLANES, 32)
    h0 = jnp.concatenate([h0_re.reshape(bn, 1, N_STATE), h0_im.reshape(bn, 1, N_STATE)], axis=-1).astype(F32)
    y_ssm, h_fin = _s5(u.reshape(bn, seq, D_SSM), h0, _s5_tables(lp, seg_len), seg_len)
    gated_a = _glu(y_ssm.reshape(T, D_SSM), lp['w_glu'].astype(BF16), sig_a, _tile(T, 1024))

    seq_q = max(seq, LANES // 2)
    q3 = jnp.pad(q.reshape(bn, seq, Q_COLS), ((0, 0), (0, seq_q - seq), (0, 0)))
    k_new, v_new = k_bf.reshape(bn, seq, Q_COLS), v_bf.reshape(bn, seq, V_COLS)
    if past == 0:
        tq = _tile(seq_q, 512)
        o = _attention(q3, k_new, v_new, lp, tq, tq // 2, past, lambda_init)
    else:
        o = _attention_cached(q3, k_past.astype(F32), v_past.astype(F32), k_new, v_new, lp, seq_q, lambda_init)
    o = o[:, :seq]

    x1, h2t = _merge(x2d, o.reshape(T, V_COLS), gated_a, sig_b, lp['w_attn_out'].astype(BF16),
                    lp['w_out'].astype(BF16), row(lp['g_ffn']), _tile(T, 256))

    tt = _tile(T, 512)
    route = _route(h2t, lp['peer_w_q'].T.astype(BF16), lp['peer_keys'].astype(F32), tt)
    peer_out = _peer(h2t, route, lp['peer_u'].astype(BF16), lp['peer_v'].astype(BF16), tt)

    y = _ple(x1, peer_out, p.reshape(T, PLE_DIM), lp['w_ple_gate'].astype(BF16), lp['w_ple_proj'].astype(BF16),
             row(lp['g_ple']), row(lp['g_final']), _tile(T, 256))

    new_k = k_f32.reshape(bn, seq, N_HEADS, 2 * HEAD_DIM)
    new_v = v_f32.reshape(bn, seq, N_HEADS, V_DIM)
    hr = h_fin[:, 0, :N_STATE].reshape(bn, N_SSM_GROUPS, SSM_STATE)
    hi = h_fin[:, 0, N_STATE:].reshape(bn, N_SSM_GROUPS, SSM_STATE)
    return y.reshape(bn, seq, D_MODEL), new_k, new_v, hr, hi


def _trunk(x, p, h0_re, h0_im, k_past, v_past, lp):
    assert DEPTH == 1
    lambda_init = 0.8 - 0.6 * math.exp(-0.3 * 0)
    y, k_new, v_new, hr, hi = _layer(x, p[0], h0_re[0], h0_im[0], k_past[0], v_past[0], lp, lambda_init)
    return y, k_new[None], v_new[None], hr[None], hi[None]


def kernel(x_prompt, x_sample, p_prompt, p_sample, cache_k, cache_v, state_ssm_re, state_ssm_im,
           g_mix_norm, w_in, ssm_lambda_re, ssm_lambda_im, ssm_log_step, ssm_b_re, ssm_b_im,
           ssm_c_re, ssm_c_im, ssm_d, w_glu, diff_lambda_q1, diff_lambda_k1, diff_lambda_q2,
           diff_lambda_k2, g_subln, w_attn_out, w_out, g_ffn_norm, peer_w_q, peer_keys, peer_u,
           peer_v, g_ple_norm, w_ple_gate, w_ple_proj, g_final):
    lp = dict(g_mix=g_mix_norm[0], w_in=w_in[0], lam_re=ssm_lambda_re[0], lam_im=ssm_lambda_im[0],
              log_step=ssm_log_step[0], b_re=ssm_b_re[0], b_im=ssm_b_im[0], c_re=ssm_c_re[0],
              c_im=ssm_c_im[0], d=ssm_d[0], w_glu=w_glu[0], lq1=diff_lambda_q1[0],
              lk1=diff_lambda_k1[0], lq2=diff_lambda_q2[0], lk2=diff_lambda_k2[0],
              g_subln=g_subln[0], w_attn_out=w_attn_out[0], w_out=w_out[0], g_ffn=g_ffn_norm[0],
              peer_w_q=peer_w_q[0], peer_keys=peer_keys[0], peer_u=peer_u[0], peer_v=peer_v[0],
              g_ple=g_ple_norm[0], w_ple_gate=w_ple_gate[0], w_ple_proj=w_ple_proj[0], g_final=g_final)
    bn = x_prompt.shape[0]
    zeros_state = jnp.zeros((DEPTH, bn, N_SSM_GROUPS, SSM_STATE), F32)
    k_none = jnp.zeros((DEPTH, bn, 0, N_HEADS, 2 * HEAD_DIM), x_prompt.dtype)
    v_none = jnp.zeros((DEPTH, bn, 0, N_HEADS, V_DIM), x_prompt.dtype)
    y_p, k_p, v_p, r_p, i_p = _trunk(x_prompt, p_prompt, zeros_state, zeros_state, k_none, v_none, lp)
    y_s, k_s, v_s, r_s, i_s = _trunk(x_sample, p_sample, state_ssm_re, state_ssm_im, cache_k, cache_v, lp)
    return (y_p, y_s, k_p, v_p, r_p, i_p, k_s, v_s, r_s, i_s)
```

```python
import functools
import math

import jax
import jax.numpy as jnp
from jax import lax
from jax.experimental import pallas as pl
from jax.experimental.pallas import tpu as pltpu

F32 = jnp.float32
BF16 = jnp.bfloat16

D_MODEL = 2048
DEPTH = 1
CHUNK = 64
CHUNK_SHIFT = CHUNK.bit_length() - 1
assert 1 << CHUNK_SHIFT == CHUNK
PLE_DIM = 256
RMS_EPS = 1e-6
NEG_INF = -1e30
D_SSM = 1024
SSM_GROUP = 16
N_SSM_GROUPS = D_SSM // SSM_GROUP
SSM_STATE = 64
N_STATE = N_SSM_GROUPS * SSM_STATE
N_HEADS = 8
HEAD_DIM = 64
V_DIM = 2 * HEAD_DIM
ROPE_THETA = 10000.0
Q_COLS = N_HEADS * 2 * HEAD_DIM
V_COLS = N_HEADS * V_DIM
IN_COLS = D_SSM + 2 * Q_COLS + V_COLS + 2 * D_MODEL
PEER_HEADS = 8
N_KEYS = 128
N_EXPERTS = N_KEYS * N_KEYS
PEER_QUERY = 256
PEER_HALF = PEER_QUERY // 2
PEER_TOPK = 16

LANES = 128
SUBLANES = 8
VMEM_LIMIT = 56 * 1024 * 1024


def _cparams(*sem):
    return pltpu.CompilerParams(dimension_semantics=sem, vmem_limit_bytes=VMEM_LIMIT)


def _rms(x, g):
    return x * lax.rsqrt(jnp.mean(x * x, axis=-1, keepdims=True) + RMS_EPS) * g


def _gelu(x):
    return 0.5 * x * (1.0 + lax.erf(x * (2.0 ** -0.5)))


def _sigmoid(x):
    return 1.0 / (1.0 + jnp.exp(-x))


def _dot(a, b):
    return jnp.dot(a, b, preferred_element_type=F32)


IN_TN = 512
Q_SCALE = HEAD_DIM ** -0.5 * math.log2(math.e)
IN_MIX_COLS = D_SSM + 2 * Q_COLS + V_COLS


def _column_tiles(h_scr, w_ref):
    return [functools.partial(lambda c: _dot(h_scr[...], w_ref[:, c * IN_TN:(c + 1) * IN_TN]), c)
            for c in range(w_ref.shape[1] // IN_TN)]


def _inproj_mix_kernel(x_ref, g_ref, w_ref, cos_ref, sin_ref, u_ref, q_ref, kf_ref, kb_ref, vf_ref, vb_ref, h_scr):
    h_scr[...] = _rms(x_ref[...], g_ref[...]).astype(BF16)
    z = _column_tiles(h_scr, w_ref)

    def rope(t):
        n = t.shape[-1]
        lane = lax.broadcasted_iota(jnp.int32, t.shape, 1)
        first = jnp.bitwise_and(lane, HEAD_DIM - 1) < (HEAD_DIM // 2)
        partner = jnp.where(first, pltpu.roll(t, n - HEAD_DIM // 2, 1), pltpu.roll(t, HEAD_DIM // 2, 1))
        reps = n // LANES
        return t * jnp.tile(cos_ref[...], (1, reps)) + partner * jnp.tile(sin_ref[...], (1, reps))

    per = Q_COLS // IN_TN
    for c in range(per):
        cols = slice(c * IN_TN, (c + 1) * IN_TN)
        u_ref[:, cols] = z[c]().astype(BF16)
        q_ref[:, cols] = (rope(z[per + c]()) * Q_SCALE).astype(BF16)
        k = rope(z[2 * per + c]())
        kf_ref[:, cols] = k
        kb_ref[:, cols] = k.astype(BF16)
        v = z[3 * per + c]()
        vf_ref[:, cols] = v
        vb_ref[:, cols] = v.astype(BF16)


def _inproj_gate_kernel(x_ref, g_ref, w_ref, ga_ref, gb_ref, h_scr):
    h_scr[...] = _rms(x_ref[...], g_ref[...]).astype(BF16)
    z = _column_tiles(h_scr, w_ref)
    per = D_MODEL // IN_TN
    for c in range(per):
        cols = slice(c * IN_TN, (c + 1) * IN_TN)
        ga_ref[:, cols] = _sigmoid(z[c]()).astype(BF16)
        gb_ref[:, cols] = _sigmoid(z[per + c]()).astype(BF16)


def _inproj(x2d, g, w_bf, cos_t, sin_t, tm):
    assert D_SSM == Q_COLS == V_COLS
    T = x2d.shape[0]
    n_tab = cos_t.shape[0] // tm
    row = lambda n: pl.BlockSpec((tm, n), lambda i: (i, 0))
    const = lambda s: pl.BlockSpec(s, lambda i: (0, 0))
    table = pl.BlockSpec((tm, LANES), lambda i: (i % n_tab, 0))
    arr = lambda n, dt: jax.ShapeDtypeStruct((T, n), dt)
    common = dict(grid=(T // tm,), scratch_shapes=[pltpu.VMEM((tm, D_MODEL), BF16)],
                  compiler_params=_cparams("parallel"))
    u, q, kf, kb, vf, vb = pl.pallas_call(
        _inproj_mix_kernel,
        in_specs=[row(D_MODEL), const((1, D_MODEL)), const((D_MODEL, IN_MIX_COLS)), table, table],
        out_specs=[row(Q_COLS)] * 6,
        out_shape=[arr(Q_COLS, BF16), arr(Q_COLS, BF16), arr(Q_COLS, F32), arr(Q_COLS, BF16),
                   arr(Q_COLS, F32), arr(Q_COLS, BF16)],
        name="inproj_mix", **common,
    )(x2d, g, w_bf[:, :IN_MIX_COLS], cos_t, sin_t)
    ga, gb = pl.pallas_call(
        _inproj_gate_kernel,
        in_specs=[row(D_MODEL), const((1, D_MODEL)), const((D_MODEL, 2 * D_MODEL))],
        out_specs=[row(D_MODEL)] * 2,
        out_shape=[arr(D_MODEL, BF16)] * 2,
        name="inproj_gate", **common,
    )(x2d, g, w_bf[:, IN_MIX_COLS:])
    return u, q, kf, kb, vf, vb, ga, gb


def _rope_tables(seq, past, rows):
    half = HEAD_DIM // 2
    inv = ROPE_THETA ** (-jnp.arange(half, dtype=F32) * 2.0 / HEAD_DIM)
    pos = (past + (jnp.arange(rows, dtype=jnp.int32) % seq)).astype(F32)
    ang = pos[:, None] * inv[None, :]
    cos, sin = jnp.cos(ang), jnp.sin(ang)
    reps = LANES // HEAD_DIM
    cos_t = jnp.tile(jnp.concatenate([cos, cos], axis=-1), (1, reps))
    sin_t = jnp.tile(jnp.concatenate([-sin, sin], axis=-1), (1, reps))
    return cos_t, sin_t


S5_COLS = 512
S5_FIX_UNROLL = SUBLANES
S5_UBLK = LANES
S5_N_UBLK = D_SSM // S5_UBLK
S5_XBLK = (S5_UBLK // SSM_GROUP) * SSM_STATE
S5_YBLK = 256
S5_N_YBLK = D_SSM // S5_YBLK
S5_HBLK = (S5_YBLK // SSM_GROUP) * SSM_STATE


def _s5_kernel(u_ref, h0_ref, perm_ref, permt_ref, wx_ref, apow_ref, cr_ref, ci_ref, d_ref,
               y_ref, hfin_ref, x_scr, c_scr, carry_scr, *, seg_len):
    t = pl.program_id(1)
    rows = SUBLANES * seg_len
    n = N_STATE

    @pl.when(t == 0)
    def _():
        carry_scr[...] = h0_ref[...]

    up = _dot(perm_ref[...], u_ref[...])
    upb = up.astype(BF16)
    assert S5_COLS == S5_XBLK
    for a in range(S5_N_UBLK):
        xa = _dot(upb[:, a * S5_UBLK:(a + 1) * S5_UBLK], wx_ref[a])
        lo = a * S5_XBLK
        x_scr[:, lo:lo + S5_XBLK] = xa[:, :S5_XBLK]
        x_scr[:, n + lo:n + lo + S5_XBLK] = xa[:, S5_XBLK:]

        ar = jnp.broadcast_to(apow_ref[0:1, lo:lo + S5_COLS], (SUBLANES, S5_COLS))
        ai = jnp.broadcast_to(apow_ref[0:1, n + lo:n + lo + S5_COLS], (SUBLANES, S5_COLS))
        hr = hi = jnp.zeros((SUBLANES, S5_COLS), F32)
        for k in range(seg_len):
            step = slice(k * SUBLANES, (k + 1) * SUBLANES)
            xr = x_scr[step, lo:lo + S5_COLS]
            xi = x_scr[step, n + lo:n + lo + S5_COLS]
            hr, hi = ar * hr - ai * hi + xr, ar * hi + ai * hr + xi
            x_scr[step, lo:lo + S5_COLS] = hr
            x_scr[step, n + lo:n + lo + S5_COLS] = hi

    alr = apow_ref[seg_len - 1:seg_len, :n]
    ali = apow_ref[seg_len - 1:seg_len, n:]
    cr = carry_scr[:, :n]
    ci = carry_scr[:, n:]
    for s in range(SUBLANES):
        c_scr[s:s + 1, :n] = cr
        c_scr[s:s + 1, n:] = ci
        lr = x_scr[rows - SUBLANES + s:rows - SUBLANES + s + 1, :n]
        li = x_scr[rows - SUBLANES + s:rows - SUBLANES + s + 1, n:]
        cr, ci = alr * cr - ali * ci + lr, alr * ci + ali * cr + li
    carry_scr[:, :n] = cr
    carry_scr[:, n:] = ci

    unroll = math.gcd(seg_len, S5_FIX_UNROLL)
    for c in range(n // S5_COLS):
        lo = c * S5_COLS
        sr = c_scr[:, lo:lo + S5_COLS]
        si = c_scr[:, n + lo:n + lo + S5_COLS]

        def fix_steps(kk, carry, lo=lo, sr=sr, si=si):
            k0 = kk * unroll if isinstance(kk, int) else pl.multiple_of(kk * unroll, unroll)
            pr_all = apow_ref[pl.ds(k0, unroll), lo:lo + S5_COLS]
            pi_all = apow_ref[pl.ds(k0, unroll), n + lo:n + lo + S5_COLS]
            for d in range(unroll):
                r0 = (k0 + d) * SUBLANES
                step = pl.ds(r0 if isinstance(r0, int) else pl.multiple_of(r0, SUBLANES), SUBLANES)
                pr, pi = pr_all[d:d + 1], pi_all[d:d + 1]
                x_scr[step, lo:lo + S5_COLS] += pr * sr - pi * si
                x_scr[step, n + lo:n + lo + S5_COLS] += pr * si + pi * sr
            return carry

        if seg_len == unroll:
            fix_steps(0, 0)
        else:
            lax.fori_loop(0, seg_len // unroll, fix_steps, 0)

    for j in range(S5_N_YBLK):
        hr = x_scr[:, j * S5_HBLK:(j + 1) * S5_HBLK].astype(BF16)
        hi = x_scr[:, n + j * S5_HBLK:n + (j + 1) * S5_HBLK].astype(BF16)
        yj = (_dot(hr, cr_ref[j]) + _dot(hi, ci_ref[j])
              + d_ref[:, j * S5_YBLK:(j + 1) * S5_YBLK] * up[:, j * S5_YBLK:(j + 1) * S5_YBLK])
        yj = _gelu(yj).astype(BF16)
        y_ref[:, j * S5_YBLK:(j + 1) * S5_YBLK] = _dot(permt_ref[...], yj).astype(BF16)

    @pl.when(t == pl.num_programs(1) - 1)
    def _():
        hfin_ref[...] = carry_scr[...]


def _s5_tables(lp, seg_len):
    g, p, c = N_SSM_GROUPS, SSM_STATE, SSM_GROUP
    dt = jnp.exp(lp['log_step'].astype(F32))[:, None]
    lr, li = lp['lam_re'].astype(F32), lp['lam_im'].astype(F32)
    mag = jnp.exp(lr * dt)
    ar, ai = mag * jnp.cos(li * dt), mag * jnp.sin(li * dt)
    den = lr * lr + li * li
    fr = ((ar - 1.0) * lr + ai * li) / den
    fi = (ai * lr - (ar - 1.0) * li) / den
    br, bi = lp['b_re'].astype(F32), lp['b_im'].astype(F32)
    bbr = fr[..., None] * br - fi[..., None] * bi
    bbi = fr[..., None] * bi + fi[..., None] * br

    gpb = S5_UBLK // c
    eye = jnp.eye(gpb, dtype=F32)

    def in_blocks(bb):
        bb = bb.reshape(S5_N_UBLK, gpb, p, c)
        return jnp.einsum('xy,axpc->axcyp', eye, bb).reshape(S5_N_UBLK, S5_UBLK, S5_XBLK)

    wx = jnp.concatenate([in_blocks(bbr), in_blocks(bbi)], axis=-1).astype(BF16)

    pr, pi = ar.reshape(1, g * p), ai.reshape(1, g * p)
    while pr.shape[0] < seg_len:
        tr, ti = pr[-1:], pi[-1:]
        pr, pi = (jnp.concatenate([pr, pr * tr - pi * ti], axis=0),
                  jnp.concatenate([pi, pr * ti + pi * tr], axis=0))
    apow = jnp.concatenate([pr[:seg_len], pi[:seg_len]], axis=-1)

    gpy = S5_YBLK // c
    eye_y = jnp.eye(gpy, dtype=F32)

    def out_blocks(cm):
        cm = cm.astype(F32).reshape(S5_N_YBLK, gpy, c, p)
        return jnp.einsum('xy,jxcp->jxpyc', eye_y, cm).reshape(S5_N_YBLK, S5_HBLK, S5_YBLK)

    cr = out_blocks(lp['c_re']).astype(BF16)
    ci = (-out_blocks(lp['c_im'])).astype(BF16)
    d = lp['d'].astype(F32).reshape(1, D_SSM)
    return wx, apow, cr, ci, d


def _s5(u, h0, tables, seg_len):
    bn, seq, _ = u.shape
    rows = SUBLANES * seg_len
    wx, apow, cr, ci, d = tables
    r = jnp.arange(rows)
    perm = (r[None, :] == ((r % SUBLANES) * seg_len + r // SUBLANES)[:, None]).astype(BF16)
    const2 = lambda b, t: (0, 0)
    const3 = lambda b, t: (0, 0, 0)
    return pl.pallas_call(
        functools.partial(_s5_kernel, seg_len=seg_len),
        grid=(bn, seq // rows),
        in_specs=[
            pl.BlockSpec((None, rows, D_SSM), lambda b, t: (b, t, 0)),
            pl.BlockSpec((None, 1, 2 * N_STATE), lambda b, t: (b, 0, 0)),
            pl.BlockSpec((rows, rows), const2),
            pl.BlockSpec((rows, rows), const2),
            pl.BlockSpec(wx.shape, const3),
            pl.BlockSpec(apow.shape, const2),
            pl.BlockSpec(cr.shape, const3),
            pl.BlockSpec(ci.shape, const3),
            pl.BlockSpec((1, D_SSM), const2),
        ],
        out_specs=[
            pl.BlockSpec((None, rows, D_SSM), lambda b, t: (b, t, 0)),
            pl.BlockSpec((None, 1, 2 * N_STATE), lambda b, t: (b, 0, 0)),
        ],
        out_shape=[jax.ShapeDtypeStruct((bn, seq, D_SSM), BF16),
                   jax.ShapeDtypeStruct((bn, 1, 2 * N_STATE), F32)],
        scratch_shapes=[pltpu.VMEM((rows, 2 * N_STATE), F32),
                        pltpu.VMEM((SUBLANES, 2 * N_STATE), F32),
                        pltpu.VMEM((1, 2 * N_STATE), F32)],
        compiler_params=_cparams("parallel", "arbitrary"),
        name="s5",
    )(u, h0, perm, perm.T, wx, apow, cr, ci, d)


GLU_TN = 512


def _glu_kernel(y_ref, w_ref, ga_ref, o_ref):
    y = y_ref[...]
    for c in range(D_MODEL // GLU_TN):
        cols = slice(c * GLU_TN, (c + 1) * GLU_TN)
        a = _dot(y, w_ref[:, cols])
        b = _dot(y, w_ref[:, D_MODEL + c * GLU_TN:D_MODEL + (c + 1) * GLU_TN])
        o_ref[:, cols] = (ga_ref[:, cols].astype(F32) * a * _sigmoid(b)).astype(BF16)


def _glu(y, w_glu_bf, sig_a, tm):
    T = y.shape[0]
    return pl.pallas_call(
        _glu_kernel,
        grid=(T // tm,),
        in_specs=[
            pl.BlockSpec((tm, D_SSM), lambda i: (i, 0)),
            pl.BlockSpec((D_SSM, 2 * D_MODEL), lambda i: (0, 0)),
            pl.BlockSpec((tm, D_MODEL), lambda i: (i, 0)),
        ],
        out_specs=pl.BlockSpec((tm, D_MODEL), lambda i: (i, 0)),
        out_shape=jax.ShapeDtypeStruct((T, D_MODEL), BF16),
        compiler_params=_cparams("parallel"),
        name="glu",
    )(y, w_glu_bf, sig_a)


def _attn_kernel(q_ref, k_ref, v_ref, lq1_ref, lk1_ref, lq2_ref, lk2_ref, g_ref, o_ref,
                 qt_scr, s0_scr, s1_scr, m_scr, l_scr, acc_scr, *, tq, tk, past, paired, lambda_init):
    i = pl.program_id(2)
    nk = k_ref.shape[0] // tk
    q = q_ref[...].astype(F32)
    lane = lax.broadcasted_iota(jnp.int32, q.shape, 1)
    qs = jnp.concatenate([jnp.where(lane < HEAD_DIM, q, 0.0), jnp.where(lane >= HEAD_DIM, q, 0.0)], axis=0)
    qt_scr[...] = qs.T.astype(BF16)

    m_scr[...] = jnp.full(m_scr.shape, NEG_INF, F32)
    l_scr[...] = jnp.zeros(l_scr.shape, F32)
    acc_scr[...] = jnp.zeros(acc_scr.shape, F32)

    q_lo = past + i * tq
    min_qc = q_lo // CHUNK
    max_qc = (q_lo + tq - 1) // CHUNK
    n_proc = jnp.minimum(nk, (max_qc * CHUNK + CHUNK - 1) // tk + 1)
    n_full = jnp.clip((min_qc * CHUNK + CHUNK) // tk, 0, n_proc)

    def scores(j, s_ref):
        k0 = pl.multiple_of(j * tk, tk)
        s_ref[...] = _dot(k_ref[pl.ds(k0, tk), :], qt_scr[...])

    def update(j, s_ref, cols):
        k0 = pl.multiple_of(j * tk, tk)
        vb = v_ref[pl.ds(k0, tk), :]
        for lo, hi, masked in cols:
            s = s_ref[:, lo:hi]
            if masked:
                kpos = k0 + lax.broadcasted_iota(jnp.int32, s.shape, 0)
                qpos = q_lo + jnp.bitwise_and(lo + lax.broadcasted_iota(jnp.int32, s.shape, 1), tq - 1)
                s = jnp.where(jnp.right_shift(kpos, CHUNK_SHIFT) <= jnp.right_shift(qpos, CHUNK_SHIFT), s, NEG_INF)
            m_old = m_scr[:, lo:hi]
            m_new = jnp.maximum(m_old, jnp.max(s, axis=0, keepdims=True))
            alpha = jnp.exp2(m_old - m_new)
            p = jnp.exp2(s - m_new)
            l_scr[:, lo:hi] = alpha * l_scr[:, lo:hi] + jnp.sum(p, axis=0, keepdims=True)
            pv = lax.dot_general(vb, p.astype(BF16), (((0,), (0,)), ((), ())), preferred_element_type=F32)
            acc_scr[:, lo:hi] = alpha * acc_scr[:, lo:hi] + pv
            m_scr[:, lo:hi] = m_new

    everything = ((0, 2 * tq, False),)
    everything_masked = ((0, 2 * tq, True),)

    if paired:
        scores(0, s0_scr)

        def pair(p, c):
            scores(2 * p + 1, s1_scr)
            update(2 * p, s0_scr, everything)
            scores(2 * p + 2, s0_scr)
            update(2 * p + 1, s1_scr, everything)
            return c

        lax.fori_loop(0, i, pair, 0)
        scores(2 * i + 1, s1_scr)
        update(2 * i, s0_scr, tuple(r for c0 in (0, tq) for r in ((c0, c0 + tk, True), (c0 + tk, c0 + tq, False))))
        update(2 * i + 1, s1_scr, tuple((c0 + tk, c0 + tq, True) for c0 in (0, tq)))
    else:
        def full_body(j, c):
            scores(j, s0_scr)
            update(j, s0_scr, everything)
            return c

        def masked_body(j, c):
            scores(j, s0_scr)
            update(j, s0_scr, everything_masked)
            return c

        lax.fori_loop(0, n_full, full_body, 0)
        lax.fori_loop(n_full, n_proc, masked_body, 0)

    lam = (jnp.exp(jnp.sum(lq1_ref[...] * lk1_ref[...], axis=-1, keepdims=True))
           - jnp.exp(jnp.sum(lq2_ref[...] * lk2_ref[...], axis=-1, keepdims=True)) + lambda_init)
    ot = acc_scr[:, :tq] / l_scr[:, :tq] - lam * (acc_scr[:, tq:] / l_scr[:, tq:])
    o_ref[...] = (_rms(ot.T, g_ref[...]) * (1.0 - lambda_init)).astype(BF16)


def _attention(q, k, v, lp, tq, tk, past, lambda_init):
    bn, seq, _ = q.shape
    lk = k.shape[1]
    vec = lambda a: a.astype(F32).reshape(1, -1)
    small = lambda n: pl.BlockSpec((1, n), lambda b, h, i: (0, 0))
    paired = past == 0 and tq == 2 * tk and lk == seq
    s_scr = pltpu.VMEM((tk, 2 * tq), F32)
    return pl.pallas_call(
        functools.partial(_attn_kernel, tq=tq, tk=tk, past=past, paired=paired, lambda_init=lambda_init),
        grid=(bn, N_HEADS, seq // tq),
        in_specs=[
            pl.BlockSpec((None, tq, V_DIM), lambda b, h, i: (b, i, h)),
            pl.BlockSpec((None, lk, V_DIM), lambda b, h, i: (b, 0, h)),
            pl.BlockSpec((None, lk, V_DIM), lambda b, h, i: (b, 0, h)),
            small(HEAD_DIM), small(HEAD_DIM), small(HEAD_DIM), small(HEAD_DIM), small(V_DIM),
        ],
        out_specs=pl.BlockSpec((None, tq, V_DIM), lambda b, h, i: (b, i, h)),
        out_shape=jax.ShapeDtypeStruct((bn, seq, V_COLS), BF16),
        scratch_shapes=[pltpu.VMEM((V_DIM, 2 * tq), BF16), s_scr, s_scr, pltpu.VMEM((1, 2 * tq), F32),
                        pltpu.VMEM((1, 2 * tq), F32), pltpu.VMEM((V_DIM, 2 * tq), F32)],
        compiler_params=_cparams("parallel", "parallel", "arbitrary"),
        name="diff_attn",
    )(q, k, v, vec(lp['lq1']), vec(lp['lk1']), vec(lp['lq2']), vec(lp['lk2']), vec(lp['g_subln']))


def _attn_cached_kernel(q_ref, kc_ref, vc_ref, kn_ref, vn_ref, lq1_ref, lk1_ref, lq2_ref, lk2_ref, g_ref, o_ref,
                        qt_scr, sc_scr, sn_scr, m_scr, l_scr, acc_scr, *, tq, past, lambda_init):
    lam = (jnp.exp(jnp.sum(lq1_ref[...] * lk1_ref[...], axis=-1, keepdims=True))
           - jnp.exp(jnp.sum(lq2_ref[...] * lk2_ref[...], axis=-1, keepdims=True)) + lambda_init)

    def update(s_ref, vb, k0, masked):
        s = s_ref[...]
        if masked:
            kpos = k0 + lax.broadcasted_iota(jnp.int32, s.shape, 0)
            qpos = past + jnp.bitwise_and(lax.broadcasted_iota(jnp.int32, s.shape, 1), tq - 1)
            s = jnp.where(jnp.right_shift(kpos, CHUNK_SHIFT) <= jnp.right_shift(qpos, CHUNK_SHIFT), s, NEG_INF)
        m_old = m_scr[...]
        m_new = jnp.maximum(m_old, jnp.max(s, axis=0, keepdims=True))
        alpha = jnp.exp2(m_old - m_new)
        p = jnp.exp2(s - m_new)
        l_scr[...] = alpha * l_scr[...] + jnp.sum(p, axis=0, keepdims=True)
        pv = lax.dot_general(vb, p.astype(BF16), (((0,), (0,)), ((), ())), preferred_element_type=F32)
        acc_scr[...] = alpha * acc_scr[...] + pv
        m_scr[...] = m_new

    for h in range(N_HEADS):
        cols = slice(h * V_DIM, (h + 1) * V_DIM)
        q = q_ref[:, cols].astype(F32)
        lane = lax.broadcasted_iota(jnp.int32, q.shape, 1)
        qs = jnp.concatenate([jnp.where(lane < HEAD_DIM, q, 0.0), jnp.where(lane >= HEAD_DIM, q, 0.0)], axis=0)
        qt_scr[...] = qs.T.astype(BF16)
        m_scr[...] = jnp.full(m_scr.shape, NEG_INF, F32)
        l_scr[...] = jnp.zeros(l_scr.shape, F32)
        acc_scr[...] = jnp.zeros(acc_scr.shape, F32)

        sc_scr[...] = _dot(kc_ref[:, h, :].astype(BF16), qt_scr[...])
        update(sc_scr, vc_ref[:, h, :].astype(BF16), 0, False)
        sn_scr[...] = _dot(kn_ref[:, cols], qt_scr[...])
        update(sn_scr, vn_ref[:, cols], past, True)

        ot = acc_scr[:, :tq] / l_scr[:, :tq] - lam * (acc_scr[:, tq:] / l_scr[:, tq:])
        o_ref[:, cols] = (_rms(ot.T, g_ref[...]) * (1.0 - lambda_init)).astype(BF16)


def _attention_cached(q, cache_k, cache_v, k_new, v_new, lp, tq, lambda_init):
    bn, past = cache_k.shape[:2]
    seq = k_new.shape[1]
    assert q.shape[1] == tq
    vec = lambda a: a.astype(F32).reshape(1, -1)
    small = lambda n: pl.BlockSpec((1, n), lambda b: (0, 0))
    rows = lambda n: pl.BlockSpec((None, n, Q_COLS), lambda b: (b, 0, 0))
    cache = pl.BlockSpec((None, past, N_HEADS, V_DIM), lambda b: (b, 0, 0, 0))
    return pl.pallas_call(
        functools.partial(_attn_cached_kernel, tq=tq, past=past, lambda_init=lambda_init),
        grid=(bn,),
        in_specs=[rows(tq), cache, cache, rows(seq), rows(seq),
                  small(HEAD_DIM), small(HEAD_DIM), small(HEAD_DIM), small(HEAD_DIM), small(V_DIM)],
        out_specs=rows(tq),
        out_shape=jax.ShapeDtypeStruct((bn, tq, V_COLS), BF16),
        scratch_shapes=[pltpu.VMEM((V_DIM, 2 * tq), BF16), pltpu.VMEM((past, 2 * tq), F32),
                        pltpu.VMEM((seq, 2 * tq), F32), pltpu.VMEM((1, 2 * tq), F32),
                        pltpu.VMEM((1, 2 * tq), F32), pltpu.VMEM((V_DIM, 2 * tq), F32)],
        compiler_params=_cparams("parallel"),
        name="diff_attn_cached",
    )(q, cache_k, cache_v, k_new, v_new,
      vec(lp['lq1']), vec(lp['lk1']), vec(lp['lq2']), vec(lp['lk2']), vec(lp['g_subln']))


def _merge_kernel(x_ref, o_ref, ga_ref, sb_ref, wa_ref, wo_ref, g_ref, x1_ref, h2t_ref):
    branch_b = _dot(o_ref[...], wa_ref[...])
    merged = ga_ref[...].astype(F32) + sb_ref[...].astype(F32) * branch_b
    x1 = x_ref[...] + _dot(merged.astype(BF16), wo_ref[...])
    x1_ref[...] = x1
    h2t_ref[...] = _rms(x1, g_ref[...]).T.astype(BF16)


def _merge(x2d, o, gated_a, sig_b, wa_bf, wo_bf, g_ffn, tm):
    T = x2d.shape[0]
    row = lambda n: pl.BlockSpec((tm, n), lambda i: (i, 0))
    const = lambda s: pl.BlockSpec(s, lambda i: (0, 0))
    return pl.pallas_call(
        _merge_kernel,
        grid=(T // tm,),
        in_specs=[row(D_MODEL), row(V_COLS), row(D_MODEL), row(D_MODEL),
                  const((V_COLS, D_MODEL)), const((D_MODEL, D_MODEL)), const((1, D_MODEL))],
        out_specs=[row(D_MODEL), pl.BlockSpec((D_MODEL, tm), lambda i: (0, i))],
        out_shape=[jax.ShapeDtypeStruct((T, D_MODEL), F32), jax.ShapeDtypeStruct((D_MODEL, T), BF16)],
        compiler_params=_cparams("parallel"),
        name="merge_out",
    )(x2d, o, gated_a, sig_b, wa_bf, wo_bf, g_ffn)


ROUTE_LC = 256
TAKEN = -3.0e38


def _split_bf16(x):
    hi = x.astype(BF16)
    return hi, (x - hi.astype(F32)).astype(BF16)


def _oddeven_merge(lo, hi, r):
    step = r * 2
    if step < hi - lo:
        yield from _oddeven_merge(lo, hi, step)
        yield from _oddeven_merge(lo + r, hi, step)
        yield from [(i, i + r) for i in range(lo + r, hi - r, step)]
    else:
        yield (lo, lo + r)


def _oddeven_merge_sort(lo, hi):
    if hi - lo >= 1:
        mid = lo + (hi - lo) // 2
        yield from _oddeven_merge_sort(lo, mid)
        yield from _oddeven_merge_sort(mid + 1, hi)
        yield from _oddeven_merge(lo, hi, 1)


def _top_rows_sorted(s, k):
    n = s.shape[0] // SUBLANES
    rows = [s[SUBLANES * g:SUBLANES * (g + 1), :] for g in range(n)]
    n_pow2 = 1 << (n - 1).bit_length()
    for i, j in _oddeven_merge_sort(0, n_pow2 - 1):
        if j < n:
            rows[i], rows[j] = jnp.maximum(rows[i], rows[j]), jnp.minimum(rows[i], rows[j])
    taken = jnp.full_like(rows[0], TAKEN)
    out = []
    for r in range(k):
        m = jnp.max(rows[0], axis=0, keepdims=True)
        out.append(m)
        hit = rows[0] == m
        depth = min(n, k - r - 1)
        for d in range(depth):
            rows[d] = jnp.where(hit, rows[d + 1] if d + 1 < n else taken, rows[d])
    return out


def _stack_rows(rows):
    n, w = len(rows), rows[0].shape[1]
    idx = lax.broadcasted_iota(jnp.int32, (n, w), 0)
    out = jnp.zeros((n, w), F32)
    for r, v in enumerate(rows):
        out = jnp.where(idx == r, v, out)
    return out


def _route_kernel(h2t_ref, wq_ref, keys_ref, thr_ref, s2_ref, e1_ref, e2_ref, s1_scr):
    for h in range(PEER_HEADS):
        _route_head(h, h2t_ref, wq_ref, keys_ref, thr_ref, s2_ref, e1_ref, e2_ref, s1_scr)


def _route_head(h, h2t_ref, wq_ref, keys_ref, thr_ref, s2_ref, e1_ref, e2_ref, s1_scr):
    tt = h2t_ref.shape[1]
    n_top = PEER_TOPK + 1
    qt = _dot(wq_ref[h * PEER_QUERY:(h + 1) * PEER_QUERY, :], h2t_ref[...])
    halves = []
    for c in range(2):
        q_hi, q_lo = _split_bf16(qt[c * PEER_HALF:(c + 1) * PEER_HALF, :])
        k_hi, k_lo = _split_bf16(keys_ref[h, c])
        halves.append(_dot(k_hi, q_hi) + _dot(k_hi, q_lo) + _dot(k_lo, q_hi))
    s1_scr[h] = halves[0]
    s2_ref[h] = halves[1]

    width = min(tt, ROUTE_LC)
    for lc in range(tt // width):
        sl = slice(lc * width, (lc + 1) * width)
        s1 = s1_scr[h, :, sl]
        s2 = s2_ref[h, :, sl]
        a = _top_rows_sorted(s1, n_top)
        b = _top_rows_sorted(s2, n_top)
        taken = jnp.full_like(b[0], TAKEN)
        b_all = _stack_rows(b + [taken] * (3 * SUBLANES - n_top))
        tail = [a[i] + b[j] for i in range(4, n_top) for j in range(n_top // (i + 1))]
        tail += [taken] * (-len(tail) % SUBLANES)
        cand = jnp.concatenate([a[0] + b_all] + [a[i] + b_all[:SUBLANES] for i in range(1, 4)]
                               + [_stack_rows(tail)], axis=0)
        top = _top_rows_sorted(cand, n_top)
        z = jnp.zeros_like(top[0])
        for r in range(PEER_TOPK):
            z = z + jnp.exp(top[r] - top[0])
        tau = 0.5 * (top[PEER_TOPK - 1] + top[PEER_TOPK])
        grouped = (N_KEYS // SUBLANES, SUBLANES, width)
        thr_ref[h, :, :, sl] = (tau - s1).reshape(grouped)
        e1_ref[h, :, :, sl] = (jnp.exp(s1 - a[0]) * (0.5 / z)).reshape(grouped)
        e2_ref[h, :, sl] = jnp.exp(s2 - b[0])


def _route(h2t, wq_t_bf, keys, tt):
    T = h2t.shape[1]
    arr = jax.ShapeDtypeStruct((PEER_HEADS, N_KEYS, T), F32)
    spec = pl.BlockSpec((PEER_HEADS, N_KEYS, tt), lambda i: (0, 0, i))
    n_grp = N_KEYS // SUBLANES
    garr = jax.ShapeDtypeStruct((PEER_HEADS, n_grp, SUBLANES, T), F32)
    gspec = pl.BlockSpec((PEER_HEADS, n_grp, SUBLANES, tt), lambda i: (0, 0, 0, i))
    return pl.pallas_call(
        _route_kernel,
        grid=(T // tt,),
        in_specs=[
            pl.BlockSpec((D_MODEL, tt), lambda i: (0, i)),
            pl.BlockSpec((PEER_HEADS * PEER_QUERY, D_MODEL), lambda i: (0, 0)),
            pl.BlockSpec((PEER_HEADS, 2, N_KEYS, PEER_HALF), lambda i: (0, 0, 0, 0)),
        ],
        out_specs=[gspec, spec, gspec, spec],
        out_shape=[garr, arr, garr, arr],
        scratch_shapes=[pltpu.VMEM((PEER_HEADS, N_KEYS, tt), F32)],
        compiler_params=_cparams("parallel"),
        name="peer_route",
    )(h2t, wq_t_bf, keys)


PEER_SUB = 512
PEER_EB = 2 * PEER_SUB
PEER_ROWS = PEER_SUB // N_KEYS
N_SUB = N_EXPERTS // PEER_SUB
PEER_JH = 64


def _peer_kernel(h2t_ref, thr_ref, s2_ref, e1_ref, e2_ref, u_ref, v_ref, o_ref,
                 act0, act1, coef0, coef1, acc_scr):
    g = pl.program_id(1)
    tt = h2t_ref.shape[1]

    def stage_a(half, act):
        act[...] = _dot(u_ref[half * PEER_SUB:(half + 1) * PEER_SUB, :], h2t_ref[...])

    def stage_b(b, act, coef):
        b = jnp.clip(b, 0, N_SUB - 1)
        grp = b // 2
        odd = (b % 2) == 1
        for lc in range(tt // LANES):
            ln = slice(lc * LANES, (lc + 1) * LANES)
            for jh in range(N_KEYS // PEER_JH):
                js = slice(jh * PEER_JH, (jh + 1) * PEER_JH)
                gates = [jnp.zeros((PEER_JH, LANES), F32) for _ in range(PEER_ROWS)]
                for h in range(PEER_HEADS):
                    s2 = s2_ref[h, js, ln]
                    e2 = e2_ref[h, js, ln]
                    thr_grp = thr_ref[h, grp, :, ln]
                    e1_grp = e1_ref[h, grp, :, ln]
                    thr_rows = jnp.where(odd, thr_grp[PEER_ROWS:], thr_grp[:PEER_ROWS])
                    e1_rows = jnp.where(odd, e1_grp[PEER_ROWS:], e1_grp[:PEER_ROWS])
                    for r in range(PEER_ROWS):
                        gates[r] = gates[r] + jnp.where(s2 >= thr_rows[r:r + 1], e1_rows[r:r + 1] * e2, 0.0)
                for r in range(PEER_ROWS):
                    rows = slice(r * N_KEYS + jh * PEER_JH, r * N_KEYS + (jh + 1) * PEER_JH)
                    x = act[rows, ln]
                    coef[rows, ln] = gates[r] * (x * (1.0 + lax.erf(x * (2.0 ** -0.5))))

    def stage_c():
        coef = jnp.concatenate([coef0[...].astype(BF16), coef1[...].astype(BF16)], axis=0)
        acc_scr[...] += lax.dot_general(coef, v_ref[...], (((0,), (0,)), ((), ())), preferred_element_type=F32)

    last = pl.num_programs(1) - 1

    @pl.when(g == 0)
    def _():
        acc_scr[...] = jnp.zeros(acc_scr.shape, F32)
        stage_a(0, act0)
        stage_a(1, act1)
        stage_b(2 * g, act0, coef0)

    @pl.when(jnp.logical_and(g > 0, g < last))
    def _():
        stage_a(0, act0)
        stage_b(2 * g - 1, act1, coef1)
        stage_a(1, act1)
        stage_c()
        stage_b(2 * g, act0, coef0)

    @pl.when(g == last)
    def _():
        stage_b(2 * g - 1, act1, coef1)
        stage_c()
        o_ref[...] = acc_scr[...].astype(BF16)


def _peer(h2t, route, u_bf, v_bf, tt):
    T = h2t.shape[1]
    thr, s2, e1, e2 = route
    n_eb = N_EXPERTS // PEER_EB
    spec = pl.BlockSpec((PEER_HEADS, N_KEYS, tt), lambda i, g: (0, 0, i))
    gspec = pl.BlockSpec((PEER_HEADS, N_KEYS // SUBLANES, SUBLANES, tt), lambda i, g: (0, 0, 0, i))
    assert SUBLANES == 2 * PEER_ROWS
    return pl.pallas_call(
        _peer_kernel,
        grid=(T // tt, n_eb + 1),
        in_specs=[
            pl.BlockSpec((D_MODEL, tt), lambda i, g: (0, i)),
            gspec, spec, gspec, spec,
            pl.BlockSpec((PEER_EB, D_MODEL), lambda i, g: (jnp.minimum(g, n_eb - 1), 0)),
            pl.BlockSpec((PEER_EB, D_MODEL), lambda i, g: (jnp.maximum(g - 1, 0), 0)),
        ],
        out_specs=pl.BlockSpec((tt, D_MODEL), lambda i, g: (i, 0)),
        out_shape=jax.ShapeDtypeStruct((T, D_MODEL), BF16),
        scratch_shapes=[pltpu.VMEM((PEER_SUB, tt), F32), pltpu.VMEM((PEER_SUB, tt), F32),
                        pltpu.VMEM((PEER_SUB, tt), F32), pltpu.VMEM((PEER_SUB, tt), F32),
                        pltpu.VMEM((tt, D_MODEL), F32)],
        compiler_params=_cparams("parallel", "arbitrary"),
        name="peer_dense",
    )(h2t, thr, s2, e1, e2, u_bf, v_bf)


def _ple_kernel(x1_ref, po_ref, p_ref, wg_ref, wp_ref, gp_ref, gf_ref, y_ref):
    x2 = x1_ref[...] + po_ref[...].astype(F32)
    h3 = _rms(x2, gp_ref[...]).astype(BF16)
    gate = _sigmoid(_dot(h3, wg_ref[...]))
    proj = _dot(p_ref[...].astype(BF16), wp_ref[...])
    y_ref[...] = _rms(x2 + proj * gate, gf_ref[...])


def _ple(x1, peer_out, p2d, wg_bf, wp_bf, g_ple, g_final, tm):
    T = x1.shape[0]
    row = lambda n: pl.BlockSpec((tm, n), lambda i: (i, 0))
    const = lambda s: pl.BlockSpec(s, lambda i: (0, 0))
    return pl.pallas_call(
        _ple_kernel,
        grid=(T // tm,),
        in_specs=[row(D_MODEL), row(D_MODEL), row(PLE_DIM),
                  const((D_MODEL, D_MODEL)), const((PLE_DIM, D_MODEL)), const((1, D_MODEL)), const((1, D_MODEL))],
        out_specs=row(D_MODEL),
        out_shape=jax.ShapeDtypeStruct((T, D_MODEL), F32),
        compiler_params=_cparams("parallel"),
        name="ple_final",
    )(x1, peer_out, p2d, wg_bf, wp_bf, g_ple, g_final)


def _tile(n, pref):
    t = min(n, pref)
    assert n % t == 0, (n, t)
    return t


def _layer(x, p, h0_re, h0_im, k_past, v_past, lp, lambda_init):
    bn, seq, _ = x.shape
    past = k_past.shape[1]
    T = bn * seq
    x2d = x.reshape(T, D_MODEL)
    row = lambda a: a.astype(F32).reshape(1, -1)

    tm_in = _tile(T, 256)
    cos_t, sin_t = _rope_tables(seq, past, max(seq, tm_in))
    u, q, k_f32, k_bf, v_f32, v_bf, sig_a, sig_b = _inproj(
        x2d, row(lp['g_mix']), lp['w_in'].astype(BF16), cos_t, sin_t, tm_in)

    seg_len = _tile(seq // SUBLANES, 32)
    h0 = jnp.concatenate([h0_re.reshape(bn, 1, N_STATE), h0_im.reshape(bn, 1, N_STATE)], axis=-1).astype(F32)
    y_ssm, h_fin = _s5(u.reshape(bn, seq, D_SSM), h0, _s5_tables(lp, seg_len), seg_len)
    gated_a = _glu(y_ssm.reshape(T, D_SSM), lp['w_glu'].astype(BF16), sig_a, _tile(T, 1024))

    seq_q = max(seq, LANES // 2)
    q3 = jnp.pad(q.reshape(bn, seq, Q_COLS), ((0, 0), (0, seq_q - seq), (0, 0)))
    k_new, v_new = k_bf.reshape(bn, seq, Q_COLS), v_bf.reshape(bn, seq, V_COLS)
    if past == 0:
        tq = _tile(seq_q, 512)
        o = _attention(q3, k_new, v_new, lp, tq, tq // 2, past, lambda_init)
    else:
        o = _attention_cached(q3, k_past.astype(F32), v_past.astype(F32), k_new, v_new, lp, seq_q, lambda_init)
    o = o[:, :seq]

    x1, h2t = _merge(x2d, o.reshape(T, V_COLS), gated_a, sig_b, lp['w_attn_out'].astype(BF16),
                    lp['w_out'].astype(BF16), row(lp['g_ffn']), _tile(T, 256))

    tt = _tile(T, 512)
    route = _route(h2t, lp['peer_w_q'].T.astype(BF16), lp['peer_keys'].astype(F32), tt)
    peer_out = _peer(h2t, route, lp['peer_u'].astype(BF16), lp['peer_v'].astype(BF16), tt)

    y = _ple(x1, peer_out, p.reshape(T, PLE_DIM), lp['w_ple_gate'].astype(BF16), lp['w_ple_proj'].astype(BF16),
             row(lp['g_ple']), row(lp['g_final']), _tile(T, 256))

    new_k = k_f32.reshape(bn, seq, N_HEADS, 2 * HEAD_DIM)
    new_v = v_f32.reshape(bn, seq, N_HEADS, V_DIM)
    hr = h_fin[:, 0, :N_STATE].reshape(bn, N_SSM_GROUPS, SSM_STATE)
    hi = h_fin[:, 0, N_STATE:].reshape(bn, N_SSM_GROUPS, SSM_STATE)
    return y.reshape(bn, seq, D_MODEL), new_k, new_v, hr, hi


def _trunk(x, p, h0_re, h0_im, k_past, v_past, lp):
    assert DEPTH == 1
    lambda_init = 0.8 - 0.6 * math.exp(-0.3 * 0)
    y, k_new, v_new, hr, hi = _layer(x, p[0], h0_re[0], h0_im[0], k_past[0], v_past[0], lp, lambda_init)
    return y, k_new[None], v_new[None], hr[None], hi[None]


def kernel(x_prompt, x_sample, p_prompt, p_sample, cache_k, cache_v, state_ssm_re, state_ssm_im,
           g_mix_norm, w_in, ssm_lambda_re, ssm_lambda_im, ssm_log_step, ssm_b_re, ssm_b_im,
           ssm_c_re, ssm_c_im, ssm_d, w_glu, diff_lambda_q1, diff_lambda_k1, diff_lambda_q2,
           diff_lambda_k2, g_subln, w_attn_out, w_out, g_ffn_norm, peer_w_q, peer_keys, peer_u,
           peer_v, g_ple_norm, w_ple_gate, w_ple_proj, g_final):
    lp = dict(g_mix=g_mix_norm[0], w_in=w_in[0], lam_re=ssm_lambda_re[0], lam_im=ssm_lambda_im[0],
              log_step=ssm_log_step[0], b_re=ssm_b_re[0], b_im=ssm_b_im[0], c_re=ssm_c_re[0],
              c_im=ssm_c_im[0], d=ssm_d[0], w_glu=w_glu[0], lq1=diff_lambda_q1[0],
              lk1=diff_lambda_k1[0], lq2=diff_lambda_q2[0], lk2=diff_lambda_k2[0],
              g_subln=g_subln[0], w_attn_out=w_attn_out[0], w_out=w_out[0], g_ffn=g_ffn_norm[0],
              peer_w_q=peer_w_q[0], peer_keys=peer_keys[0], peer_u=peer_u[0], peer_v=peer_v[0],
              g_ple=g_ple_norm[0], w_ple_gate=w_ple_gate[0], w_ple_proj=w_ple_proj[0], g_final=g_final)
    bn = x_prompt.shape[0]
    zeros_state = jnp.zeros((DEPTH, bn, N_SSM_GROUPS, SSM_STATE), F32)
    k_none = jnp.zeros((DEPTH, bn, 0, N_HEADS, 2 * HEAD_DIM), x_prompt.dtype)
    v_none = jnp.zeros((DEPTH, bn, 0, N_HEADS, V_DIM), x_prompt.dtype)
    y_p, k_p, v_p, r_p, i_p = _trunk(x_prompt, p_prompt, zeros_state, zeros_state, k_none, v_none, lp)
    y_s, k_s, v_s, r_s, i_s = _trunk(x_sample, p_sample, state_ssm_re, state_ssm_im, cache_k, cache_v, lp)
    return (y_p, y_s, k_p, v_p, r_p, i_p, k_s, v_s, r_s, i_s)
```

```python
import functools
import math

import jax
import jax.numpy as jnp
from jax import lax
from jax.experimental import pallas as pl
from jax.experimental.pallas import tpu as pltpu

F32 = jnp.float32
BF16 = jnp.bfloat16

D_MODEL = 2048
DEPTH = 1
CHUNK = 64
CHUNK_SHIFT = CHUNK.bit_length() - 1
assert 1 << CHUNK_SHIFT == CHUNK
PLE_DIM = 256
RMS_EPS = 1e-6
NEG_INF = -1e30
D_SSM = 1024
SSM_GROUP = 16
N_SSM_GROUPS = D_SSM // SSM_GROUP
SSM_STATE = 64
N_STATE = N_SSM_GROUPS * SSM_STATE
N_HEADS = 8
HEAD_DIM = 64
V_DIM = 2 * HEAD_DIM
ROPE_THETA = 10000.0
Q_COLS = N_HEADS * 2 * HEAD_DIM
V_COLS = N_HEADS * V_DIM
IN_COLS = D_SSM + 2 * Q_COLS + V_COLS + 2 * D_MODEL
PEER_HEADS = 8
N_KEYS = 128
N_EXPERTS = N_KEYS * N_KEYS
PEER_QUERY = 256
PEER_HALF = PEER_QUERY // 2
PEER_TOPK = 16

LANES = 128
SUBLANES = 8
VMEM_LIMIT = 56 * 1024 * 1024


def _cparams(*sem):
    return pltpu.CompilerParams(dimension_semantics=sem, vmem_limit_bytes=VMEM_LIMIT)


def _rms(x, g):
    return x * lax.rsqrt(jnp.mean(x * x, axis=-1, keepdims=True) + RMS_EPS) * g


def _gelu(x):
    return 0.5 * x * (1.0 + lax.erf(x * (2.0 ** -0.5)))


def _sigmoid(x):
    return 1.0 / (1.0 + jnp.exp(-x))


def _dot(a, b):
    return jnp.dot(a, b, preferred_element_type=F32)


IN_TN = 512
Q_SCALE = HEAD_DIM ** -0.5 * math.log2(math.e)
IN_MIX_COLS = D_SSM + 2 * Q_COLS + V_COLS


def _column_tiles(h_scr, w_ref):
    return [functools.partial(lambda c: _dot(h_scr[...], w_ref[:, c * IN_TN:(c + 1) * IN_TN]), c)
            for c in range(w_ref.shape[1] // IN_TN)]


def _inproj_mix_kernel(x_ref, g_ref, w_ref, cos_ref, sin_ref, u_ref, q_ref, kf_ref, kb_ref, vf_ref, vb_ref, h_scr):
    h_scr[...] = _rms(x_ref[...], g_ref[...]).astype(BF16)
    z = _column_tiles(h_scr, w_ref)

    def rope(t):
        n = t.shape[-1]
        lane = lax.broadcasted_iota(jnp.int32, t.shape, 1)
        first = jnp.bitwise_and(lane, HEAD_DIM - 1) < (HEAD_DIM // 2)
        partner = jnp.where(first, pltpu.roll(t, n - HEAD_DIM // 2, 1), pltpu.roll(t, HEAD_DIM // 2, 1))
        reps = n // LANES
        return t * jnp.tile(cos_ref[...], (1, reps)) + partner * jnp.tile(sin_ref[...], (1, reps))

    per = Q_COLS // IN_TN
    for c in range(per):
        cols = slice(c * IN_TN, (c + 1) * IN_TN)
        u_ref[:, cols] = z[c]().astype(BF16)
        q_ref[:, cols] = (rope(z[per + c]()) * Q_SCALE).astype(BF16)
        k = rope(z[2 * per + c]())
        kf_ref[:, cols] = k
        kb_ref[:, cols] = k.astype(BF16)
        v = z[3 * per + c]()
        vf_ref[:, cols] = v
        vb_ref[:, cols] = v.astype(BF16)


def _inproj_gate_kernel(x_ref, g_ref, w_ref, ga_ref, gb_ref, h_scr):
    h_scr[...] = _rms(x_ref[...], g_ref[...]).astype(BF16)
    z = _column_tiles(h_scr, w_ref)
    per = D_MODEL // IN_TN
    for c in range(per):
        cols = slice(c * IN_TN, (c + 1) * IN_TN)
        ga_ref[:, cols] = _sigmoid(z[c]()).astype(BF16)
        gb_ref[:, cols] = _sigmoid(z[per + c]()).astype(BF16)


def _inproj(x2d, g, w_bf, cos_t, sin_t, tm):
    assert D_SSM == Q_COLS == V_COLS
    T = x2d.shape[0]
    n_tab = cos_t.shape[0] // tm
    row = lambda n: pl.BlockSpec((tm, n), lambda i: (i, 0))
    const = lambda s: pl.BlockSpec(s, lambda i: (0, 0))
    table = pl.BlockSpec((tm, LANES), lambda i: (i % n_tab, 0))
    arr = lambda n, dt: jax.ShapeDtypeStruct((T, n), dt)
    common = dict(grid=(T // tm,), scratch_shapes=[pltpu.VMEM((tm, D_MODEL), BF16)],
                  compiler_params=_cparams("parallel"))
    u, q, kf, kb, vf, vb = pl.pallas_call(
        _inproj_mix_kernel,
        in_specs=[row(D_MODEL), const((1, D_MODEL)), const((D_MODEL, IN_MIX_COLS)), table, table],
        out_specs=[row(Q_COLS)] * 6,
        out_shape=[arr(Q_COLS, BF16), arr(Q_COLS, BF16), arr(Q_COLS, F32), arr(Q_COLS, BF16),
                   arr(Q_COLS, F32), arr(Q_COLS, BF16)],
        name="inproj_mix", **common,
    )(x2d, g, w_bf[:, :IN_MIX_COLS], cos_t, sin_t)
    ga, gb = pl.pallas_call(
        _inproj_gate_kernel,
        in_specs=[row(D_MODEL), const((1, D_MODEL)), const((D_MODEL, 2 * D_MODEL))],
        out_specs=[row(D_MODEL)] * 2,
        out_shape=[arr(D_MODEL, BF16)] * 2,
        name="inproj_gate", **common,
    )(x2d, g, w_bf[:, IN_MIX_COLS:])
    return u, q, kf, kb, vf, vb, ga, gb


def _rope_tables(seq, past, rows):
    half = HEAD_DIM // 2
    inv = ROPE_THETA ** (-jnp.arange(half, dtype=F32) * 2.0 / HEAD_DIM)
    pos = (past + (jnp.arange(rows, dtype=jnp.int32) % seq)).astype(F32)
    ang = pos[:, None] * inv[None, :]
    cos, sin = jnp.cos(ang), jnp.sin(ang)
    reps = LANES // HEAD_DIM
    cos_t = jnp.tile(jnp.concatenate([cos, cos], axis=-1), (1, reps))
    sin_t = jnp.tile(jnp.concatenate([-sin, sin], axis=-1), (1, reps))
    return cos_t, sin_t


S5_COLS = 512
S5_FIX_UNROLL = SUBLANES
S5_UBLK = LANES
S5_N_UBLK = D_SSM // S5_UBLK
S5_XBLK = (S5_UBLK // SSM_GROUP) * SSM_STATE
S5_YBLK = 256
S5_N_YBLK = D_SSM // S5_YBLK
S5_HBLK = (S5_YBLK // SSM_GROUP) * SSM_STATE


def _s5_kernel(u_ref, h0_ref, perm_ref, permt_ref, wx_ref, apow_ref, cr_ref, ci_ref, d_ref,
               y_ref, hfin_ref, x_scr, c_scr, carry_scr, *, seg_len):
    t = pl.program_id(1)
    rows = SUBLANES * seg_len
    n = N_STATE

    @pl.when(t == 0)
    def _():
        carry_scr[...] = h0_ref[...]

    up = _dot(perm_ref[...], u_ref[...])
    upb = up.astype(BF16)
    assert S5_COLS == S5_XBLK
    for a in range(S5_N_UBLK):
        xa = _dot(upb[:, a * S5_UBLK:(a + 1) * S5_UBLK], wx_ref[a])
        lo = a * S5_XBLK
        x_scr[:, lo:lo + S5_XBLK] = xa[:, :S5_XBLK]
        x_scr[:, n + lo:n + lo + S5_XBLK] = xa[:, S5_XBLK:]

        ar = jnp.broadcast_to(apow_ref[0:1, lo:lo + S5_COLS], (SUBLANES, S5_COLS))
        ai = jnp.broadcast_to(apow_ref[0:1, n + lo:n + lo + S5_COLS], (SUBLANES, S5_COLS))
        hr = hi = jnp.zeros((SUBLANES, S5_COLS), F32)
        for k in range(seg_len):
            step = slice(k * SUBLANES, (k + 1) * SUBLANES)
            xr = x_scr[step, lo:lo + S5_COLS]
            xi = x_scr[step, n + lo:n + lo + S5_COLS]
            hr, hi = ar * hr - ai * hi + xr, ar * hi + ai * hr + xi
            x_scr[step, lo:lo + S5_COLS] = hr
            x_scr[step, n + lo:n + lo + S5_COLS] = hi

    alr = apow_ref[seg_len - 1:seg_len, :n]
    ali = apow_ref[seg_len - 1:seg_len, n:]
    cr = carry_scr[:, :n]
    ci = carry_scr[:, n:]
    for s in range(SUBLANES):
        c_scr[s:s + 1, :n] = cr
        c_scr[s:s + 1, n:] = ci
        lr = x_scr[rows - SUBLANES + s:rows - SUBLANES + s + 1, :n]
        li = x_scr[rows - SUBLANES + s:rows - SUBLANES + s + 1, n:]
        cr, ci = alr * cr - ali * ci + lr, alr * ci + ali * cr + li
    carry_scr[:, :n] = cr
    carry_scr[:, n:] = ci

    unroll = math.gcd(seg_len, S5_FIX_UNROLL)
    for c in range(n // S5_COLS):
        lo = c * S5_COLS
        sr = c_scr[:, lo:lo + S5_COLS]
        si = c_scr[:, n + lo:n + lo + S5_COLS]

        def fix_steps(kk, carry, lo=lo, sr=sr, si=si):
            k0 = kk * unroll if isinstance(kk, int) else pl.multiple_of(kk * unroll, unroll)
            pr_all = apow_ref[pl.ds(k0, unroll), lo:lo + S5_COLS]
            pi_all = apow_ref[pl.ds(k0, unroll), n + lo:n + lo + S5_COLS]
            for d in range(unroll):
                r0 = (k0 + d) * SUBLANES
                step = pl.ds(r0 if isinstance(r0, int) else pl.multiple_of(r0, SUBLANES), SUBLANES)
                pr, pi = pr_all[d:d + 1], pi_all[d:d + 1]
                x_scr[step, lo:lo + S5_COLS] += pr * sr - pi * si
                x_scr[step, n + lo:n + lo + S5_COLS] += pr * si + pi * sr
            return carry

        if seg_len == unroll:
            fix_steps(0, 0)
        else:
            lax.fori_loop(0, seg_len // unroll, fix_steps, 0)

    for j in range(S5_N_YBLK):
        hr = x_scr[:, j * S5_HBLK:(j + 1) * S5_HBLK].astype(BF16)
        hi = x_scr[:, n + j * S5_HBLK:n + (j + 1) * S5_HBLK].astype(BF16)
        yj = (_dot(hr, cr_ref[j]) + _dot(hi, ci_ref[j])
              + d_ref[:, j * S5_YBLK:(j + 1) * S5_YBLK] * up[:, j * S5_YBLK:(j + 1) * S5_YBLK])
        yj = _gelu(yj).astype(BF16)
        y_ref[:, j * S5_YBLK:(j + 1) * S5_YBLK] = _dot(permt_ref[...], yj).astype(BF16)

    @pl.when(t == pl.num_programs(1) - 1)
    def _():
        hfin_ref[...] = carry_scr[...]


def _s5_tables(lp, seg_len):
    g, p, c = N_SSM_GROUPS, SSM_STATE, SSM_GROUP
    dt = jnp.exp(lp['log_step'].astype(F32))[:, None]
    lr, li = lp['lam_re'].astype(F32), lp['lam_im'].astype(F32)
    mag = jnp.exp(lr * dt)
    ar, ai = mag * jnp.cos(li * dt), mag * jnp.sin(li * dt)
    den = lr * lr + li * li
    fr = ((ar - 1.0) * lr + ai * li) / den
    fi = (ai * lr - (ar - 1.0) * li) / den
    br, bi = lp['b_re'].astype(F32), lp['b_im'].astype(F32)
    bbr = fr[..., None] * br - fi[..., None] * bi
    bbi = fr[..., None] * bi + fi[..., None] * br

    gpb = S5_UBLK // c
    eye = jnp.eye(gpb, dtype=F32)

    def in_blocks(bb):
        bb = bb.reshape(S5_N_UBLK, gpb, p, c)
        return jnp.einsum('xy,axpc->axcyp', eye, bb).reshape(S5_N_UBLK, S5_UBLK, S5_XBLK)

    wx = jnp.concatenate([in_blocks(bbr), in_blocks(bbi)], axis=-1).astype(BF16)

    pr, pi = ar.reshape(1, g * p), ai.reshape(1, g * p)
    while pr.shape[0] < seg_len:
        tr, ti = pr[-1:], pi[-1:]
        pr, pi = (jnp.concatenate([pr, pr * tr - pi * ti], axis=0),
                  jnp.concatenate([pi, pr * ti + pi * tr], axis=0))
    apow = jnp.concatenate([pr[:seg_len], pi[:seg_len]], axis=-1)

    gpy = S5_YBLK // c
    eye_y = jnp.eye(gpy, dtype=F32)

    def out_blocks(cm):
        cm = cm.astype(F32).reshape(S5_N_YBLK, gpy, c, p)
        return jnp.einsum('xy,jxcp->jxpyc', eye_y, cm).reshape(S5_N_YBLK, S5_HBLK, S5_YBLK)

    cr = out_blocks(lp['c_re']).astype(BF16)
    ci = (-out_blocks(lp['c_im'])).astype(BF16)
    d = lp['d'].astype(F32).reshape(1, D_SSM)
    return wx, apow, cr, ci, d


def _s5(u, h0, tables, seg_len):
    bn, seq, _ = u.shape
    rows = SUBLANES * seg_len
    wx, apow, cr, ci, d = tables
    r = jnp.arange(rows)
    perm = (r[None, :] == ((r % SUBLANES) * seg_len + r // SUBLANES)[:, None]).astype(BF16)
    const2 = lambda b, t: (0, 0)
    const3 = lambda b, t: (0, 0, 0)
    return pl.pallas_call(
        functools.partial(_s5_kernel, seg_len=seg_len),
        grid=(bn, seq // rows),
        in_specs=[
            pl.BlockSpec((None, rows, D_SSM), lambda b, t: (b, t, 0)),
            pl.BlockSpec((None, 1, 2 * N_STATE), lambda b, t: (b, 0, 0)),
            pl.BlockSpec((rows, rows), const2),
            pl.BlockSpec((rows, rows), const2),
            pl.BlockSpec(wx.shape, const3),
            pl.BlockSpec(apow.shape, const2),
            pl.BlockSpec(cr.shape, const3),
            pl.BlockSpec(ci.shape, const3),
            pl.BlockSpec((1, D_SSM), const2),
        ],
        out_specs=[
            pl.BlockSpec((None, rows, D_SSM), lambda b, t: (b, t, 0)),
            pl.BlockSpec((None, 1, 2 * N_STATE), lambda b, t: (b, 0, 0)),
        ],
        out_shape=[jax.ShapeDtypeStruct((bn, seq, D_SSM), BF16),
                   jax.ShapeDtypeStruct((bn, 1, 2 * N_STATE), F32)],
        scratch_shapes=[pltpu.VMEM((rows, 2 * N_STATE), F32),
                        pltpu.VMEM((SUBLANES, 2 * N_STATE), F32),
                        pltpu.VMEM((1, 2 * N_STATE), F32)],
        compiler_params=_cparams("parallel", "arbitrary"),
        name="s5",
    )(u, h0, perm, perm.T, wx, apow, cr, ci, d)


GLU_TN = 512


def _glu_kernel(y_ref, w_ref, ga_ref, o_ref):
    y = y_ref[...]
    for c in range(D_MODEL // GLU_TN):
        cols = slice(c * GLU_TN, (c + 1) * GLU_TN)
        a = _dot(y, w_ref[:, cols])
        b = _dot(y, w_ref[:, D_MODEL + c * GLU_TN:D_MODEL + (c + 1) * GLU_TN])
        o_ref[:, cols] = (ga_ref[:, cols].astype(F32) * a * _sigmoid(b)).astype(BF16)


def _glu(y, w_glu_bf, sig_a, tm):
    T = y.shape[0]
    return pl.pallas_call(
        _glu_kernel,
        grid=(T // tm,),
        in_specs=[
            pl.BlockSpec((tm, D_SSM), lambda i: (i, 0)),
            pl.BlockSpec((D_SSM, 2 * D_MODEL), lambda i: (0, 0)),
            pl.BlockSpec((tm, D_MODEL), lambda i: (i, 0)),
        ],
        out_specs=pl.BlockSpec((tm, D_MODEL), lambda i: (i, 0)),
        out_shape=jax.ShapeDtypeStruct((T, D_MODEL), BF16),
        compiler_params=_cparams("parallel"),
        name="glu",
    )(y, w_glu_bf, sig_a)


def _attn_kernel(q_ref, k_ref, v_ref, lq1_ref, lk1_ref, lq2_ref, lk2_ref, g_ref, o_ref,
                 qt_scr, s0_scr, s1_scr, m_scr, l_scr, acc_scr, *, tq, tk, past, paired, lambda_init):
    i = pl.program_id(2)
    nk = k_ref.shape[0] // tk
    q = q_ref[...].astype(F32)
    lane = lax.broadcasted_iota(jnp.int32, q.shape, 1)
    qs = jnp.concatenate([jnp.where(lane < HEAD_DIM, q, 0.0), jnp.where(lane >= HEAD_DIM, q, 0.0)], axis=0)
    qt_scr[...] = qs.T.astype(BF16)

    m_scr[...] = jnp.full(m_scr.shape, NEG_INF, F32)
    l_scr[...] = jnp.zeros(l_scr.shape, F32)
    acc_scr[...] = jnp.zeros(acc_scr.shape, F32)

    q_lo = past + i * tq
    min_qc = q_lo // CHUNK
    max_qc = (q_lo + tq - 1) // CHUNK
    n_proc = jnp.minimum(nk, (max_qc * CHUNK + CHUNK - 1) // tk + 1)
    n_full = jnp.clip((min_qc * CHUNK + CHUNK) // tk, 0, n_proc)

    def scores(j, s_ref):
        k0 = pl.multiple_of(j * tk, tk)
        s_ref[...] = _dot(k_ref[pl.ds(k0, tk), :], qt_scr[...])

    def update(j, s_ref, cols):
        k0 = pl.multiple_of(j * tk, tk)
        vb = v_ref[pl.ds(k0, tk), :]
        for lo, hi, masked in cols:
            s = s_ref[:, lo:hi]
            if masked:
                kpos = k0 + lax.broadcasted_iota(jnp.int32, s.shape, 0)
                qpos = q_lo + jnp.bitwise_and(lo + lax.broadcasted_iota(jnp.int32, s.shape, 1), tq - 1)
                s = jnp.where(jnp.right_shift(kpos, CHUNK_SHIFT) <= jnp.right_shift(qpos, CHUNK_SHIFT), s, NEG_INF)
            m_old = m_scr[:, lo:hi]
            m_new = jnp.maximum(m_old, jnp.max(s, axis=0, keepdims=True))
            alpha = jnp.exp2(m_old - m_new)
            p = jnp.exp2(s - m_new)
            l_scr[:, lo:hi] = alpha * l_scr[:, lo:hi] + jnp.sum(p, axis=0, keepdims=True)
            pv = lax.dot_general(vb, p.astype(BF16), (((0,), (0,)), ((), ())), preferred_element_type=F32)
            acc_scr[:, lo:hi] = alpha * acc_scr[:, lo:hi] + pv
            m_scr[:, lo:hi] = m_new

    everything = ((0, 2 * tq, False),)
    everything_masked = ((0, 2 * tq, True),)

    if paired:
        scores(0, s0_scr)

        def pair(p, c):
            scores(2 * p + 1, s1_scr)
            update(2 * p, s0_scr, everything)
            scores(2 * p + 2, s0_scr)
            update(2 * p + 1, s1_scr, everything)
            return c

        lax.fori_loop(0, i, pair, 0)
        scores(2 * i + 1, s1_scr)
        update(2 * i, s0_scr, tuple(r for c0 in (0, tq) for r in ((c0, c0 + tk, True), (c0 + tk, c0 + tq, False))))
        update(2 * i + 1, s1_scr, tuple((c0 + tk, c0 + tq, True) for c0 in (0, tq)))
    else:
        def full_body(j, c):
            scores(j, s0_scr)
            update(j, s0_scr, everything)
            return c

        def masked_body(j, c):
            scores(j, s0_scr)
            update(j, s0_scr, everything_masked)
            return c

        lax.fori_loop(0, n_full, full_body, 0)
        lax.fori_loop(n_full, n_proc, masked_body, 0)

    lam = (jnp.exp(jnp.sum(lq1_ref[...] * lk1_ref[...], axis=-1, keepdims=True))
           - jnp.exp(jnp.sum(lq2_ref[...] * lk2_ref[...], axis=-1, keepdims=True)) + lambda_init)
    ot = acc_scr[:, :tq] / l_scr[:, :tq] - lam * (acc_scr[:, tq:] / l_scr[:, tq:])
    o_ref[...] = (_rms(ot.T, g_ref[...]) * (1.0 - lambda_init)).astype(BF16)


def _attention(q, k, v, lp, tq, tk, past, lambda_init):
    bn, seq, _ = q.shape
    lk = k.shape[1]
    vec = lambda a: a.astype(F32).reshape(1, -1)
    small = lambda n: pl.BlockSpec((1, n), lambda b, h, i: (0, 0))
    paired = past == 0 and tq == 2 * tk and lk == seq
    s_scr = pltpu.VMEM((tk, 2 * tq), F32)
    return pl.pallas_call(
        functools.partial(_attn_kernel, tq=tq, tk=tk, past=past, paired=paired, lambda_init=lambda_init),
        grid=(bn, N_HEADS, seq // tq),
        in_specs=[
            pl.BlockSpec((None, tq, V_DIM), lambda b, h, i: (b, i, h)),
            pl.BlockSpec((None, lk, V_DIM), lambda b, h, i: (b, 0, h)),
            pl.BlockSpec((None, lk, V_DIM), lambda b, h, i: (b, 0, h)),
            small(HEAD_DIM), small(HEAD_DIM), small(HEAD_DIM), small(HEAD_DIM), small(V_DIM),
        ],
        out_specs=pl.BlockSpec((None, tq, V_DIM), lambda b, h, i: (b, i, h)),
        out_shape=jax.ShapeDtypeStruct((bn, seq, V_COLS), BF16),
        scratch_shapes=[pltpu.VMEM((V_DIM, 2 * tq), BF16), s_scr, s_scr, pltpu.VMEM((1, 2 * tq), F32),
                        pltpu.VMEM((1, 2 * tq), F32), pltpu.VMEM((V_DIM, 2 * tq), F32)],
        compiler_params=_cparams("parallel", "parallel", "arbitrary"),
        name="diff_attn",
    )(q, k, v, vec(lp['lq1']), vec(lp['lk1']), vec(lp['lq2']), vec(lp['lk2']), vec(lp['g_subln']))


def _attn_cached_kernel(q_ref, kc_ref, vc_ref, kn_ref, vn_ref, lq1_ref, lk1_ref, lq2_ref, lk2_ref, g_ref, o_ref,
                        qt_scr, sc_scr, sn_scr, m_scr, l_scr, acc_scr, *, tq, past, lambda_init):
    lam = (jnp.exp(jnp.sum(lq1_ref[...] * lk1_ref[...], axis=-1, keepdims=True))
           - jnp.exp(jnp.sum(lq2_ref[...] * lk2_ref[...], axis=-1, keepdims=True)) + lambda_init)

    def update(s_ref, vb, k0, masked):
        s = s_ref[...]
        if masked:
            kpos = k0 + lax.broadcasted_iota(jnp.int32, s.shape, 0)
            qpos = past + jnp.bitwise_and(lax.broadcasted_iota(jnp.int32, s.shape, 1), tq - 1)
            s = jnp.where(jnp.right_shift(kpos, CHUNK_SHIFT) <= jnp.right_shift(qpos, CHUNK_SHIFT), s, NEG_INF)
        m_old = m_scr[...]
        m_new = jnp.maximum(m_old, jnp.max(s, axis=0, keepdims=True))
        alpha = jnp.exp2(m_old - m_new)
        p = jnp.exp2(s - m_new)
        l_scr[...] = alpha * l_scr[...] + jnp.sum(p, axis=0, keepdims=True)
        pv = lax.dot_general(vb, p.astype(BF16), (((0,), (0,)), ((), ())), preferred_element_type=F32)
        acc_scr[...] = alpha * acc_scr[...] + pv
        m_scr[...] = m_new

    for h in range(N_HEADS):
        cols = slice(h * V_DIM, (h + 1) * V_DIM)
        q = q_ref[:, cols].astype(F32)
        lane = lax.broadcasted_iota(jnp.int32, q.shape, 1)
        qs = jnp.concatenate([jnp.where(lane < HEAD_DIM, q, 0.0), jnp.where(lane >= HEAD_DIM, q, 0.0)], axis=0)
        qt_scr[...] = qs.T.astype(BF16)
        m_scr[...] = jnp.full(m_scr.shape, NEG_INF, F32)
        l_scr[...] = jnp.zeros(l_scr.shape, F32)
        acc_scr[...] = jnp.zeros(acc_scr.shape, F32)

        sc_scr[...] = _dot(kc_ref[:, h, :].astype(BF16), qt_scr[...])
        update(sc_scr, vc_ref[:, h, :].astype(BF16), 0, False)
        sn_scr[...] = _dot(kn_ref[:, cols], qt_scr[...])
        update(sn_scr, vn_ref[:, cols], past, True)

        ot = acc_scr[:, :tq] / l_scr[:, :tq] - lam * (acc_scr[:, tq:] / l_scr[:, tq:])
        o_ref[:, cols] = (_rms(ot.T, g_ref[...]) * (1.0 - lambda_init)).astype(BF16)


def _attention_cached(q, cache_k, cache_v, k_new, v_new, lp, tq, lambda_init):
    bn, past = cache_k.shape[:2]
    seq = k_new.shape[1]
    assert q.shape[1] == tq
    vec = lambda a: a.astype(F32).reshape(1, -1)
    small = lambda n: pl.BlockSpec((1, n), lambda b: (0, 0))
    rows = lambda n: pl.BlockSpec((None, n, Q_COLS), lambda b: (b, 0, 0))
    cache = pl.BlockSpec((None, past, N_HEADS, V_DIM), lambda b: (b, 0, 0, 0))
    return pl.pallas_call(
        functools.partial(_attn_cached_kernel, tq=tq, past=past, lambda_init=lambda_init),
        grid=(bn,),
        in_specs=[rows(tq), cache, cache, rows(seq), rows(seq),
                  small(HEAD_DIM), small(HEAD_DIM), small(HEAD_DIM), small(HEAD_DIM), small(V_DIM)],
        out_specs=rows(tq),
        out_shape=jax.ShapeDtypeStruct((bn, tq, V_COLS), BF16),
        scratch_shapes=[pltpu.VMEM((V_DIM, 2 * tq), BF16), pltpu.VMEM((past, 2 * tq), F32),
                        pltpu.VMEM((seq, 2 * tq), F32), pltpu.VMEM((1, 2 * tq), F32),
                        pltpu.VMEM((1, 2 * tq), F32), pltpu.VMEM((V_DIM, 2 * tq), F32)],
        compiler_params=_cparams("parallel"),
        name="diff_attn_cached",
    )(q, cache_k, cache_v, k_new, v_new,
      vec(lp['lq1']), vec(lp['lk1']), vec(lp['lq2']), vec(lp['lk2']), vec(lp['g_subln']))


def _merge_kernel(x_ref, o_ref, ga_ref, sb_ref, wa_ref, wo_ref, g_ref, x1_ref, h2t_ref):
    branch_b = _dot(o_ref[...], wa_ref[...])
    merged = ga_ref[...].astype(F32) + sb_ref[...].astype(F32) * branch_b
    x1 = x_ref[...] + _dot(merged.astype(BF16), wo_ref[...])
    x1_ref[...] = x1
    h2t_ref[...] = _rms(x1, g_ref[...]).T.astype(BF16)


def _merge(x2d, o, gated_a, sig_b, wa_bf, wo_bf, g_ffn, tm):
    T = x2d.shape[0]
    row = lambda n: pl.BlockSpec((tm, n), lambda i: (i, 0))
    const = lambda s: pl.BlockSpec(s, lambda i: (0, 0))
    return pl.pallas_call(
        _merge_kernel,
        grid=(T // tm,),
        in_specs=[row(D_MODEL), row(V_COLS), row(D_MODEL), row(D_MODEL),
                  const((V_COLS, D_MODEL)), const((D_MODEL, D_MODEL)), const((1, D_MODEL))],
        out_specs=[row(D_MODEL), pl.BlockSpec((D_MODEL, tm), lambda i: (0, i))],
        out_shape=[jax.ShapeDtypeStruct((T, D_MODEL), F32), jax.ShapeDtypeStruct((D_MODEL, T), BF16)],
        compiler_params=_cparams("parallel"),
        name="merge_out",
    )(x2d, o, gated_a, sig_b, wa_bf, wo_bf, g_ffn)


ROUTE_LC = 256
TAKEN = -3.0e38


def _split_bf16(x):
    hi = x.astype(BF16)
    return hi, (x - hi.astype(F32)).astype(BF16)


def _oddeven_merge(lo, hi, r):
    step = r * 2
    if step < hi - lo:
        yield from _oddeven_merge(lo, hi, step)
        yield from _oddeven_merge(lo + r, hi, step)
        yield from [(i, i + r) for i in range(lo + r, hi - r, step)]
    else:
        yield (lo, lo + r)


def _oddeven_merge_sort(lo, hi):
    if hi - lo >= 1:
        mid = lo + (hi - lo) // 2
        yield from _oddeven_merge_sort(lo, mid)
        yield from _oddeven_merge_sort(mid + 1, hi)
        yield from _oddeven_merge(lo, hi, 1)


def _top_rows_sorted(s, k):
    n = s.shape[0] // SUBLANES
    rows = [s[SUBLANES * g:SUBLANES * (g + 1), :] for g in range(n)]
    n_pow2 = 1 << (n - 1).bit_length()
    for i, j in _oddeven_merge_sort(0, n_pow2 - 1):
        if j < n:
            rows[i], rows[j] = jnp.maximum(rows[i], rows[j]), jnp.minimum(rows[i], rows[j])
    taken = jnp.full_like(rows[0], TAKEN)
    out = []
    for r in range(k):
        m = jnp.max(rows[0], axis=0, keepdims=True)
        out.append(m)
        hit = rows[0] == m
        depth = min(n, k - r - 1)
        for d in range(depth):
            rows[d] = jnp.where(hit, rows[d + 1] if d + 1 < n else taken, rows[d])
    return out


def _stack_rows(rows):
    n, w = len(rows), rows[0].shape[1]
    idx = lax.broadcasted_iota(jnp.int32, (n, w), 0)
    out = jnp.zeros((n, w), F32)
    for r, v in enumerate(rows):
        out = jnp.where(idx == r, v, out)
    return out


def _route_kernel(h2t_ref, wq_ref, keys_ref, thr_ref, s2_ref, e1_ref, e2_ref, s1_scr):
    for h in range(PEER_HEADS):
        _route_head(h, h2t_ref, wq_ref, keys_ref, thr_ref, s2_ref, e1_ref, e2_ref, s1_scr)


def _route_head(h, h2t_ref, wq_ref, keys_ref, thr_ref, s2_ref, e1_ref, e2_ref, s1_scr):
    tt = h2t_ref.shape[1]
    n_top = PEER_TOPK + 1
    qt = _dot(wq_ref[h * PEER_QUERY:(h + 1) * PEER_QUERY, :], h2t_ref[...])
    halves = []
    for c in range(2):
        q_hi, q_lo = _split_bf16(qt[c * PEER_HALF:(c + 1) * PEER_HALF, :])
        k_hi, k_lo = _split_bf16(keys_ref[h, c])
        halves.append(_dot(k_hi, q_hi) + _dot(k_hi, q_lo) + _dot(k_lo, q_hi))
    s1_scr[h] = halves[0]
    s2_ref[h] = halves[1]

    width = min(tt, ROUTE_LC)
    for lc in range(tt // width):
        sl = slice(lc * width, (lc + 1) * width)
        s1 = s1_scr[h, :, sl]
        s2 = s2_ref[h, :, sl]
        a = _top_rows_sorted(s1, n_top)
        b = _top_rows_sorted(s2, n_top)
        taken = jnp.full_like(b[0], TAKEN)
        b_all = _stack_rows(b + [taken] * (3 * SUBLANES - n_top))
        tail = [a[i] + b[j] for i in range(4, n_top) for j in range(n_top // (i + 1))]
        tail += [taken] * (-len(tail) % SUBLANES)
        cand = jnp.concatenate([a[0] + b_all] + [a[i] + b_all[:SUBLANES] for i in range(1, 4)]
                               + [_stack_rows(tail)], axis=0)
        top = _top_rows_sorted(cand, n_top)
        z = jnp.zeros_like(top[0])
        for r in range(PEER_TOPK):
            z = z + jnp.exp(top[r] - top[0])
        tau = 0.5 * (top[PEER_TOPK - 1] + top[PEER_TOPK])
        grouped = (N_KEYS // SUBLANES, SUBLANES, width)
        thr_ref[h, :, :, sl] = (tau - s1).reshape(grouped)
        e1_ref[h, :, :, sl] = (jnp.exp(s1 - a[0]) * (0.5 * GELU_ARG_SCALE ** -1 / z)).reshape(grouped)
        e2_ref[h, :, sl] = jnp.exp(s2 - b[0])


def _route(h2t, wq_t_bf, keys, tt):
    T = h2t.shape[1]
    arr = jax.ShapeDtypeStruct((PEER_HEADS, N_KEYS, T), F32)
    spec = pl.BlockSpec((PEER_HEADS, N_KEYS, tt), lambda i: (0, 0, i))
    n_grp = N_KEYS // SUBLANES
    garr = jax.ShapeDtypeStruct((PEER_HEADS, n_grp, SUBLANES, T), F32)
    gspec = pl.BlockSpec((PEER_HEADS, n_grp, SUBLANES, tt), lambda i: (0, 0, 0, i))
    return pl.pallas_call(
        _route_kernel,
        grid=(T // tt,),
        in_specs=[
            pl.BlockSpec((D_MODEL, tt), lambda i: (0, i)),
            pl.BlockSpec((PEER_HEADS * PEER_QUERY, D_MODEL), lambda i: (0, 0)),
            pl.BlockSpec((PEER_HEADS, 2, N_KEYS, PEER_HALF), lambda i: (0, 0, 0, 0)),
        ],
        out_specs=[gspec, spec, gspec, spec],
        out_shape=[garr, arr, garr, arr],
        scratch_shapes=[pltpu.VMEM((PEER_HEADS, N_KEYS, tt), F32)],
        compiler_params=_cparams("parallel"),
        name="peer_route",
    )(h2t, wq_t_bf, keys)


GELU_ARG_SCALE = 2.0 ** -0.5
PEER_SUB = 512
PEER_EB = 2 * PEER_SUB
PEER_ROWS = PEER_SUB // N_KEYS
N_SUB = N_EXPERTS // PEER_SUB
PEER_JH = 64


def _peer_kernel(h2t_ref, thr_ref, s2_ref, e1_ref, e2_ref, u_ref, v_ref, o_ref,
                 act0, act1, coef0, coef1, acc_scr):
    g = pl.program_id(1)
    tt = h2t_ref.shape[1]

    def stage_a(half, act):
        act[...] = _dot(u_ref[half * PEER_SUB:(half + 1) * PEER_SUB, :], h2t_ref[...])

    def stage_b(b, act, coef):
        b = jnp.clip(b, 0, N_SUB - 1)
        grp = b // 2
        odd = (b % 2) == 1
        for lc in range(tt // LANES):
            ln = slice(lc * LANES, (lc + 1) * LANES)
            for jh in range(N_KEYS // PEER_JH):
                js = slice(jh * PEER_JH, (jh + 1) * PEER_JH)
                gates = [jnp.zeros((PEER_JH, LANES), F32) for _ in range(PEER_ROWS)]
                for h in range(PEER_HEADS):
                    s2 = s2_ref[h, js, ln]
                    e2 = e2_ref[h, js, ln]
                    thr_grp = thr_ref[h, grp, :, ln]
                    e1_grp = e1_ref[h, grp, :, ln]
                    thr_rows = jnp.where(odd, thr_grp[PEER_ROWS:], thr_grp[:PEER_ROWS])
                    e1_rows = jnp.where(odd, e1_grp[PEER_ROWS:], e1_grp[:PEER_ROWS])
                    for r in range(PEER_ROWS):
                        gates[r] = gates[r] + jnp.where(s2 >= thr_rows[r:r + 1], e1_rows[r:r + 1] * e2, 0.0)
                for r in range(PEER_ROWS):
                    rows = slice(r * N_KEYS + jh * PEER_JH, r * N_KEYS + (jh + 1) * PEER_JH)
                    x = act[rows, ln]
                    coef[rows, ln] = gates[r] * (x * (1.0 + lax.erf(x)))

    def stage_c():
        coef = jnp.concatenate([coef0[...].astype(BF16), coef1[...].astype(BF16)], axis=0)
        acc_scr[...] += lax.dot_general(coef, v_ref[...], (((0,), (0,)), ((), ())), preferred_element_type=F32)

    last = pl.num_programs(1) - 1

    @pl.when(g == 0)
    def _():
        acc_scr[...] = jnp.zeros(acc_scr.shape, F32)
        stage_a(0, act0)
        stage_a(1, act1)
        stage_b(2 * g, act0, coef0)

    @pl.when(jnp.logical_and(g > 0, g < last))
    def _():
        stage_a(0, act0)
        stage_b(2 * g - 1, act1, coef1)
        stage_a(1, act1)
        stage_c()
        stage_b(2 * g, act0, coef0)

    @pl.when(g == last)
    def _():
        stage_b(2 * g - 1, act1, coef1)
        stage_c()
        o_ref[...] = acc_scr[...].astype(BF16)


def _peer(h2t, route, u_bf, v_bf, tt):
    T = h2t.shape[1]
    thr, s2, e1, e2 = route
    n_eb = N_EXPERTS // PEER_EB
    spec = pl.BlockSpec((PEER_HEADS, N_KEYS, tt), lambda i, g: (0, 0, i))
    gspec = pl.BlockSpec((PEER_HEADS, N_KEYS // SUBLANES, SUBLANES, tt), lambda i, g: (0, 0, 0, i))
    assert SUBLANES == 2 * PEER_ROWS
    return pl.pallas_call(
        _peer_kernel,
        grid=(T // tt, n_eb + 1),
        in_specs=[
            pl.BlockSpec((D_MODEL, tt), lambda i, g: (0, i)),
            gspec, spec, gspec, spec,
            pl.BlockSpec((PEER_EB, D_MODEL), lambda i, g: (jnp.minimum(g, n_eb - 1), 0)),
            pl.BlockSpec((PEER_EB, D_MODEL), lambda i, g: (jnp.maximum(g - 1, 0), 0)),
        ],
        out_specs=pl.BlockSpec((tt, D_MODEL), lambda i, g: (i, 0)),
        out_shape=jax.ShapeDtypeStruct((T, D_MODEL), BF16),
        scratch_shapes=[pltpu.VMEM((PEER_SUB, tt), F32), pltpu.VMEM((PEER_SUB, tt), F32),
                        pltpu.VMEM((PEER_SUB, tt), F32), pltpu.VMEM((PEER_SUB, tt), F32),
                        pltpu.VMEM((tt, D_MODEL), F32)],
        compiler_params=_cparams("parallel", "arbitrary"),
        name="peer_dense",
    )(h2t, thr, s2, e1, e2, u_bf, v_bf)


def _ple_kernel(x1_ref, po_ref, p_ref, wg_ref, wp_ref, gp_ref, gf_ref, y_ref):
    x2 = x1_ref[...] + po_ref[...].astype(F32)
    h3 = _rms(x2, gp_ref[...]).astype(BF16)
    gate = _sigmoid(_dot(h3, wg_ref[...]))
    proj = _dot(p_ref[...].astype(BF16), wp_ref[...])
    y_ref[...] = _rms(x2 + proj * gate, gf_ref[...])


def _ple(x1, peer_out, p2d, wg_bf, wp_bf, g_ple, g_final, tm):
    T = x1.shape[0]
    row = lambda n: pl.BlockSpec((tm, n), lambda i: (i, 0))
    const = lambda s: pl.BlockSpec(s, lambda i: (0, 0))
    return pl.pallas_call(
        _ple_kernel,
        grid=(T // tm,),
        in_specs=[row(D_MODEL), row(D_MODEL), row(PLE_DIM),
                  const((D_MODEL, D_MODEL)), const((PLE_DIM, D_MODEL)), const((1, D_MODEL)), const((1, D_MODEL))],
        out_specs=row(D_MODEL),
        out_shape=jax.ShapeDtypeStruct((T, D_MODEL), F32),
        compiler_params=_cparams("parallel"),
        name="ple_final",
    )(x1, peer_out, p2d, wg_bf, wp_bf, g_ple, g_final)


def _tile(n, pref):
    t = min(n, pref)
    assert n % t == 0, (n, t)
    return t


def _layer(x, p, h0_re, h0_im, k_past, v_past, lp, lambda_init):
    bn, seq, _ = x.shape
    past = k_past.shape[1]
    T = bn * seq
    x2d = x.reshape(T, D_MODEL)
    row = lambda a: a.astype(F32).reshape(1, -1)

    tm_in = _tile(T, 256)
    cos_t, sin_t = _rope_tables(seq, past, max(seq, tm_in))
    u, q, k_f32, k_bf, v_f32, v_bf, sig_a, sig_b = _inproj(
        x2d, row(lp['g_mix']), lp['w_in'].astype(BF16), cos_t, sin_t, tm_in)

    seg_len = _tile(seq // SUBLANES, 32)
    h0 = jnp.concatenate([h0_re.reshape(bn, 1, N_STATE), h0_im.reshape(bn, 1, N_STATE)], axis=-1).astype(F32)
    y_ssm, h_fin = _s5(u.reshape(bn, seq, D_SSM), h0, _s5_tables(lp, seg_len), seg_len)
    gated_a = _glu(y_ssm.reshape(T, D_SSM), lp['w_glu'].astype(BF16), sig_a, _tile(T, 1024))

    seq_q = max(seq, LANES // 2)
    q3 = jnp.pad(q.reshape(bn, seq, Q_COLS), ((0, 0), (0, seq_q - seq), (0, 0)))
    k_new, v_new = k_bf.reshape(bn, seq, Q_COLS), v_bf.reshape(bn, seq, V_COLS)
    if past == 0:
        tq = _tile(seq_q, 512)
        o = _attention(q3, k_new, v_new, lp, tq, tq // 2, past, lambda_init)
    else:
        o = _attention_cached(q3, k_past.astype(F32), v_past.astype(F32), k_new, v_new, lp, seq_q, lambda_init)
    o = o[:, :seq]

    x1, h2t = _merge(x2d, o.reshape(T, V_COLS), gated_a, sig_b, lp['w_attn_out'].astype(BF16),
                    lp['w_out'].astype(BF16), row(lp['g_ffn']), _tile(T, 256))

    tt = _tile(T, 512)
    route = _route(h2t, lp['peer_w_q'].T.astype(BF16), lp['peer_keys'].astype(F32), tt)
    peer_out = _peer(h2t, route, (lp['peer_u'] * GELU_ARG_SCALE).astype(BF16), lp['peer_v'].astype(BF16), tt)

    y = _ple(x1, peer_out, p.reshape(T, PLE_DIM), lp['w_ple_gate'].astype(BF16), lp['w_ple_proj'].astype(BF16),
             row(lp['g_ple']), row(lp['g_final']), _tile(T, 256))

    new_k = k_f32.reshape(bn, seq, N_HEADS, 2 * HEAD_DIM)
    new_v = v_f32.reshape(bn, seq, N_HEADS, V_DIM)
    hr = h_fin[:, 0, :N_STATE].reshape(bn, N_SSM_GROUPS, SSM_STATE)
    hi = h_fin[:, 0, N_STATE:].reshape(bn, N_SSM_GROUPS, SSM_STATE)
    return y.reshape(bn, seq, D_MODEL), new_k, new_v, hr, hi


def _trunk(x, p, h0_re, h0_im, k_past, v_past, lp):
    assert DEPTH == 1
    lambda_init = 0.8 - 0.6 * math.exp(-0.3 * 0)
    y, k_new, v_new, hr, hi = _layer(x, p[0], h0_re[0], h0_im[0], k_past[0], v_past[0], lp, lambda_init)
    return y, k_new[None], v_new[None], hr[None], hi[None]


def kernel(x_prompt, x_sample, p_prompt, p_sample, cache_k, cache_v, state_ssm_re, state_ssm_im,
           g_mix_norm, w_in, ssm_lambda_re, ssm_lambda_im, ssm_log_step, ssm_b_re, ssm_b_im,
           ssm_c_re, ssm_c_im, ssm_d, w_glu, diff_lambda_q1, diff_lambda_k1, diff_lambda_q2,
           diff_lambda_k2, g_subln, w_attn_out, w_out, g_ffn_norm, peer_w_q, peer_keys, peer_u,
           peer_v, g_ple_norm, w_ple_gate, w_ple_proj, g_final):
    lp = dict(g_mix=g_mix_norm[0], w_in=w_in[0], lam_re=ssm_lambda_re[0], lam_im=ssm_lambda_im[0],
              log_step=ssm_log_step[0], b_re=ssm_b_re[0], b_im=ssm_b_im[0], c_re=ssm_c_re[0],
              c_im=ssm_c_im[0], d=ssm_d[0], w_glu=w_glu[0], lq1=diff_lambda_q1[0],
              lk1=diff_lambda_k1[0], lq2=diff_lambda_q2[0], lk2=diff_lambda_k2[0],
              g_subln=g_subln[0], w_attn_out=w_attn_out[0], w_out=w_out[0], g_ffn=g_ffn_norm[0],
              peer_w_q=peer_w_q[0], peer_keys=peer_keys[0], peer_u=peer_u[0], peer_v=peer_v[0],
              g_ple=g_ple_norm[0], w_ple_gate=w_ple_gate[0], w_ple_proj=w_ple_proj[0], g_final=g_final)
    bn = x_prompt.shape[0]
    zeros_state = jnp.zeros((DEPTH, bn, N_SSM_GROUPS, SSM_STATE), F32)
    k_none = jnp.zeros((DEPTH, bn, 0, N_HEADS, 2 * HEAD_DIM), x_prompt.dtype)
    v_none = jnp.zeros((DEPTH, bn, 0, N_HEADS, V_DIM), x_prompt.dtype)
    y_p, k_p, v_p, r_p, i_p = _trunk(x_prompt, p_prompt, zeros_state, zeros_state, k_none, v_none, lp)
    y_s, k_s, v_s, r_s, i_s = _trunk(x_sample, p_sample, state_ssm_re, state_ssm_im, cache_k, cache_v, lp)
    return (y_p, y_s, k_p, v_p, r_p, i_p, k_s, v_s, r_s, i_s)
```
